```python
import math
import jax, jax.numpy as jnp
from jax import lax
import numpy as np

D_MODEL = 1024
BATCH = 2
SEQ = 16384
DEPTH = 2

HEAD_DIM = 64
GRID_W = 64
EPS = 1e-6

ATT_HEADS = 8
ATT_KV_HEADS = 2
ATT_WIDTH = ATT_HEADS * HEAD_DIM
KV_WIDTH = ATT_KV_HEADS * HEAD_DIM
Q_BLOCK = 128
ROPE_THETA = 10000.0
ROPE_AXIS_DIM = HEAD_DIM // 2

CONV_GROUPS = 4
CONV_WIDTH = CONV_GROUPS * HEAD_DIM
CONV_KERNEL = 31

SG_HEADS = 4
SG_WIDTH = SG_HEADS * HEAD_DIM
SG_CHUNK = 128

D_MIX = ATT_WIDTH + CONV_WIDTH + SG_WIDTH

IN_SPLIT_SIZES = (
    ATT_WIDTH,
    KV_WIDTH,
    KV_WIDTH,
    ATT_WIDTH,
    2 * CONV_WIDTH,
    CONV_WIDTH,
    SG_WIDTH,
    SG_WIDTH,
    SG_WIDTH,
)
D_IN = sum(IN_SPLIT_SIZES)

kernel_name = "hybrid_parallel_conv_gqa_sgu_encoder"


def _split_points():
    pts, acc = [], 0
    for s in IN_SPLIT_SIZES[:-1]:
        acc += s
        pts.append(acc)
    return pts


def rms_norm(x, g):
    xf = x.astype(jnp.float32)
    y = xf * lax.rsqrt(jnp.mean(xf * xf, axis=-1, keepdims=True) + EPS) * g.astype(jnp.float32)
    return y.astype(x.dtype)


def layer_norm(x, g, b):
    xf = x.astype(jnp.float32)
    mu = jnp.mean(xf, axis=-1, keepdims=True)
    xc = xf - mu
    var = jnp.mean(xc * xc, axis=-1, keepdims=True)
    y = xc * lax.rsqrt(var + EPS) * g.astype(jnp.float32) + b.astype(jnp.float32)
    return y.astype(x.dtype)


def rope_1d(x, pos):
    d = x.shape[-1]
    half = d // 2
    inv_freq = ROPE_THETA ** (-jnp.arange(half, dtype=jnp.float32) / half)
    ang = pos[:, None] * inv_freq[None, :]
    cos = jnp.cos(ang)[:, None, :]
    sin = jnp.sin(ang)[:, None, :]
    xf = x.astype(jnp.float32)
    x1, x2 = xf[..., :half], xf[..., half:]
    out = jnp.concatenate([x1 * cos - x2 * sin, x2 * cos + x1 * sin], axis=-1)
    return out.astype(x.dtype)


def axial_rope(x, row, col):
    return jnp.concatenate([rope_1d(x[..., :ROPE_AXIS_DIM], row),
                            rope_1d(x[..., ROPE_AXIS_DIM:], col)], axis=-1)


def attention_group(q, k, v):
    B, S = q.shape[0], q.shape[1]
    G = ATT_HEADS // ATT_KV_HEADS
    nblk = S // Q_BLOCK
    qb = q.reshape(B, nblk, Q_BLOCK, ATT_KV_HEADS, G, HEAD_DIM).transpose(1, 0, 3, 4, 2, 5)
    kt = k.transpose(0, 2, 1, 3)
    vt = v.transpose(0, 2, 1, 3)
    scale = HEAD_DIM ** -0.5

    def one_block(qi):
        s = jnp.einsum('bkgqd,bksd->bkgqs', qi, kt, preferred_element_type=jnp.float32) * scale
        p = jax.nn.softmax(s, axis=-1)
        return jnp.einsum('bkgqs,bksd->bkgqd', p.astype(vt.dtype), vt)

    o = lax.map(one_block, qb)
    return o.transpose(1, 0, 4, 2, 3, 5).reshape(B, S, ATT_WIDTH)


def conv_group(a, dw_w, dw_b, ln_g, ln_b):
    h = a[..., :CONV_WIDTH] * jax.nn.sigmoid(a[..., CONV_WIDTH:])
    pad = CONV_KERNEL // 2
    h = lax.conv_general_dilated(
        h, dw_w[:, None, :].astype(h.dtype), window_strides=(1,), padding=[(pad, pad)],
        dimension_numbers=('NWC', 'WIO', 'NWC'), feature_group_count=CONV_WIDTH) + dw_b
    h = layer_norm(h, ln_g, ln_b)
    return jax.nn.silu(h)


def spatial_gating_group(u, v, ln_g, ln_b, w_s, b_s):
    B, S = u.shape[0], u.shape[1]
    u = jax.nn.gelu(u, approximate=False)
    v = layer_norm(jax.nn.gelu(v, approximate=False), ln_g, ln_b)
    n = S // SG_CHUNK
    vc = v.reshape(B, n, SG_CHUNK, SG_HEADS, HEAD_DIM)
    mixed = jnp.einsum('hpq,bnqhd->bnphd', w_s, vc) + b_s.T[None, None, :, :, None]
    return u * mixed.reshape(B, S, SG_WIDTH)


def setup_inputs(seed: int = 0) -> dict:
    key = jax.random.key(seed)
    ks = jax.random.split(key, 16)
    f32 = jnp.float32
    x = jax.random.normal(ks[0], (BATCH, SEQ, D_MODEL), f32)
    pre_norm = 1.0 + 0.05 * jax.random.normal(ks[1], (DEPTH, D_MODEL), f32)
    post_norm = 1.0 + 0.05 * jax.random.normal(ks[2], (DEPTH, D_MODEL), f32)
    w_in = jax.random.normal(ks[3], (DEPTH, D_MODEL, D_IN), f32) * D_MODEL ** -0.5
    w_out = jax.random.normal(ks[4], (DEPTH, D_MIX, D_MODEL), f32) * D_MIX ** -0.5
    q_norm = 1.0 + 0.05 * jax.random.normal(ks[5], (DEPTH, HEAD_DIM), f32)
    k_norm = 1.0 + 0.05 * jax.random.normal(ks[6], (DEPTH, HEAD_DIM), f32)
    conv_dw = jax.random.normal(ks[7], (DEPTH, CONV_KERNEL, CONV_WIDTH), f32) * CONV_KERNEL ** -0.5
    conv_dw_b = 0.02 * jax.random.normal(ks[8], (DEPTH, CONV_WIDTH), f32)
    conv_ln_g = 1.0 + 0.05 * jax.random.normal(ks[9], (DEPTH, CONV_WIDTH), f32)
    conv_ln_b = 0.02 * jax.random.normal(ks[10], (DEPTH, CONV_WIDTH), f32)
    sg_ln_g = 1.0 + 0.05 * jax.random.normal(ks[11], (DEPTH, SG_WIDTH), f32)
    sg_ln_b = 0.02 * jax.random.normal(ks[12], (DEPTH, SG_WIDTH), f32)
    sg_w = jax.random.normal(ks[13], (DEPTH, SG_HEADS, SG_CHUNK, SG_CHUNK), f32) * SG_CHUNK ** -0.5
    sg_b = 1.0 + 0.1 * jax.random.normal(ks[14], (DEPTH, SG_HEADS, SG_CHUNK), f32)
    return {"x": x, "pre_norm": pre_norm, "post_norm": post_norm, "w_in": w_in, "w_out": w_out,
            "q_norm": q_norm, "k_norm": k_norm, "conv_dw": conv_dw, "conv_dw_b": conv_dw_b,
            "conv_ln_g": conv_ln_g, "conv_ln_b": conv_ln_b, "sg_ln_g": sg_ln_g, "sg_ln_b": sg_ln_b,
            "sg_w": sg_w, "sg_b": sg_b}


def reference(x, pre_norm, post_norm, w_in, w_out, q_norm, k_norm, conv_dw, conv_dw_b,
              conv_ln_g, conv_ln_b, sg_ln_g, sg_ln_b, sg_w, sg_b):
    B, S = x.shape[0], x.shape[1]
    rows = S // GRID_W
    row = jnp.repeat(jnp.arange(rows, dtype=jnp.int32), GRID_W).astype(jnp.float32)
    col = jnp.tile(jnp.arange(GRID_W, dtype=jnp.int32), rows).astype(jnp.float32)
    split_pts = _split_points()

    for l in range(DEPTH):
        h = rms_norm(x, pre_norm[l])
        proj = jnp.einsum('bsd,de->bse', h, w_in[l])
        q, k, v, g_att, a_conv, g_conv, u_sg, v_sg, g_sg = jnp.split(proj, split_pts, axis=-1)

        q = axial_rope(rms_norm(q.reshape(B, S, ATT_HEADS, HEAD_DIM), q_norm[l]), row, col)
        k = axial_rope(rms_norm(k.reshape(B, S, ATT_KV_HEADS, HEAD_DIM), k_norm[l]), row, col)
        v = v.reshape(B, S, ATT_KV_HEADS, HEAD_DIM)
        att = attention_group(q, k, v) * jax.nn.silu(g_att)

        cnv = conv_group(a_conv, conv_dw[l], conv_dw_b[l], conv_ln_g[l], conv_ln_b[l]) * jax.nn.silu(g_conv)

        sgu = spatial_gating_group(u_sg, v_sg, sg_ln_g[l], sg_ln_b[l], sg_w[l], sg_b[l]) * jax.nn.silu(g_sg)

        mix = jnp.einsum('bse,ed->bsd', jnp.concatenate([att, cnv, sgu], axis=-1), w_out[l])
        x = x + rms_norm(mix, post_norm[l])
    return x
```

```python
import functools

import jax
import jax.numpy as jnp
from jax import lax
from jax.experimental import pallas as pl
from jax.experimental.pallas import tpu as pltpu

D_MODEL = 1024
HEAD_DIM = 64
GRID_W = 64
EPS = 1e-6
ATT_HEADS = 8
ATT_KV_HEADS = 2
GQA = ATT_HEADS // ATT_KV_HEADS
ATT_WIDTH = ATT_HEADS * HEAD_DIM
KV_WIDTH = ATT_KV_HEADS * HEAD_DIM
ROPE_THETA = 10000.0
ROPE_AXIS_DIM = HEAD_DIM // 2
CONV_WIDTH = 256
CONV_KERNEL = 31
CONV_PAD = CONV_KERNEL // 2
SG_HEADS = 4
SG_WIDTH = SG_HEADS * HEAD_DIM
SG_CHUNK = 128
D_MIX = ATT_WIDTH + CONV_WIDTH + SG_WIDTH
QKV_WIDTH = ATT_WIDTH + 2 * KV_WIDTH
REST_WIDTH = ATT_WIDTH + 3 * CONV_WIDTH + 3 * SG_WIDTH
D_IN = QKV_WIDTH + REST_WIDTH

R_GATT = 0
R_ACONV = ATT_WIDTH
R_GCONV = R_ACONV + 2 * CONV_WIDTH
R_U = R_GCONV + CONV_WIDTH
R_V = R_U + SG_WIDTH
R_GSG = R_V + SG_WIDTH

LANES = 128
HALO = 16
TM = 512
TQ = 128
TK = 512
NQ = GQA * TQ
VMEM_LIMIT = 56 * 1024 * 1024


def _bf16(x):
    return x.astype(jnp.bfloat16)


def _group_mean_sq(x, bd):
    x2 = x * x
    hi = _bf16(x2)
    lo = _bf16(x2 - hi.astype(jnp.float32))
    return (jnp.dot(hi, bd, preferred_element_type=jnp.float32)
            + jnp.dot(lo, bd, preferred_element_type=jnp.float32))


def _rope_slab(xn, cos, sin_signed, first_half):
    fwd = pltpu.roll(xn, LANES - ROPE_AXIS_DIM // 2, 1)
    bwd = pltpu.roll(xn, ROPE_AXIS_DIM // 2, 1)
    partner = jnp.where(first_half, fwd, bwd)
    return xn * cos + partner * sin_signed


def _in_proj_kernel(x_ref, g_ref, w_ref, gq_ref, gk_ref, bd_ref, cos_ref, sin_ref,
                    qt_ref, k_ref, vt_ref, rest_ref):
    x = x_ref[...]
    ms = jnp.mean(x * x, axis=-1, keepdims=True)
    h = _bf16(x * lax.rsqrt(ms + EPS) * g_ref[...])

    cos = cos_ref[...]
    sin_signed = sin_ref[...]
    lane = lax.broadcasted_iota(jnp.int32, (1, LANES), 1)
    first_half = (lane % ROPE_AXIS_DIM) < (ROPE_AXIS_DIM // 2)

    q = jnp.dot(h, w_ref[:, 0:ATT_WIDTH], preferred_element_type=jnp.float32)
    qn = q * lax.rsqrt(_group_mean_sq(q, bd_ref[...]) + EPS) * gq_ref[...]
    for j in range(ATT_WIDTH // LANES):
        sl = slice(j * LANES, (j + 1) * LANES)
        r = _rope_slab(qn[:, sl], cos, sin_signed, first_half) * (HEAD_DIM ** -0.5)
        qt_ref[sl, :] = _bf16(r.T)

    kk = jnp.dot(h, w_ref[:, ATT_WIDTH:ATT_WIDTH + KV_WIDTH], preferred_element_type=jnp.float32)
    kn = kk * lax.rsqrt(_group_mean_sq(kk, bd_ref[0:KV_WIDTH, 0:KV_WIDTH]) + EPS) * gk_ref[...]
    k_ref[...] = _bf16(_rope_slab(kn, cos, sin_signed, first_half))

    v = jnp.dot(h, w_ref[:, ATT_WIDTH + KV_WIDTH:QKV_WIDTH], preferred_element_type=jnp.float32)
    vt_ref[...] = _bf16(v.T)

    for c in range(REST_WIDTH // 512):
        rest_ref[:, c * 512:(c + 1) * 512] = jnp.dot(
            h, w_ref[:, QKV_WIDTH + c * 512:QKV_WIDTH + (c + 1) * 512],
            preferred_element_type=jnp.float32)


def _in_proj(x, g, w, gq, gk, bd, cos_t, sin_t):
    B, S, _ = x.shape
    ns = S // TM
    const = lambda b, s: (0, 0)
    return pl.pallas_call(
        _in_proj_kernel,
        grid=(B, ns),
        in_specs=[
            pl.BlockSpec((None, TM, D_MODEL), lambda b, s: (b, s, 0)),
            pl.BlockSpec((1, D_MODEL), const),
            pl.BlockSpec((D_MODEL, D_IN), const),
            pl.BlockSpec((1, ATT_WIDTH), const),
            pl.BlockSpec((1, KV_WIDTH), const),
            pl.BlockSpec((ATT_WIDTH, ATT_WIDTH), const),
            pl.BlockSpec((TM, LANES), lambda b, s: (s, 0)),
            pl.BlockSpec((TM, LANES), lambda b, s: (s, 0)),
        ],
        out_specs=[
            pl.BlockSpec((None, ATT_WIDTH, TM), lambda b, s: (b, 0, s)),
            pl.BlockSpec((None, TM, KV_WIDTH), lambda b, s: (b, s, 0)),
            pl.BlockSpec((None, None, KV_WIDTH, TM), lambda b, s: (b, s, 0, 0)),
            pl.BlockSpec((None, TM, REST_WIDTH), lambda b, s: (b, s, 0)),
        ],
        out_shape=[
            jax.ShapeDtypeStruct((B, ATT_WIDTH, S), jnp.bfloat16),
            jax.ShapeDtypeStruct((B, S, KV_WIDTH), jnp.bfloat16),
            jax.ShapeDtypeStruct((B, ns, KV_WIDTH, TM), jnp.bfloat16),
            jax.ShapeDtypeStruct((B, S, REST_WIDTH), jnp.float32),
        ],
        compiler_params=pltpu.CompilerParams(
            dimension_semantics=("arbitrary", "arbitrary"), vmem_limit_bytes=VMEM_LIMIT),
        name="in_proj",
    )(x, g, w, gq, gk, bd, cos_t, sin_t)


def _attention_kernel(qt_ref, k_ref, vt_ref, o_ref, qext_ref):
    kv = pl.program_id(1)
    for g in range(GQA):
        qg = qt_ref[g * HEAD_DIM:(g + 1) * HEAD_DIM, :]
        zero = jnp.zeros_like(qg)
        qext_ref[0:HEAD_DIM, g * TQ:(g + 1) * TQ] = jnp.where(kv == 0, qg, zero)
        qext_ref[HEAD_DIM:2 * HEAD_DIM, g * TQ:(g + 1) * TQ] = jnp.where(kv == 1, qg, zero)
    qext = qext_ref[...]

    def step(j, carry):
        m, l, acc = carry
        kblk = k_ref[pl.ds(pl.multiple_of(j * TK, TK), TK), :]
        s = jnp.dot(kblk, qext, preferred_element_type=jnp.float32)
        m_new = jnp.maximum(m, jnp.max(s, axis=0, keepdims=True))
        alpha = jnp.exp(m - m_new)
        p = jnp.exp(s - m_new)
        l = alpha * l + jnp.sum(p, axis=0, keepdims=True)
        acc = alpha * acc + jnp.dot(vt_ref[j], _bf16(p), preferred_element_type=jnp.float32)
        return m_new, l, acc

    n_steps = k_ref.shape[0] // TK
    init = (jnp.full((1, NQ), -jnp.inf, jnp.float32),
            jnp.zeros((1, NQ), jnp.float32),
            jnp.zeros((HEAD_DIM, NQ), jnp.float32))
    _, l, acc = lax.fori_loop(0, n_steps, step, init)
    o = acc / l
    o4 = jnp.concatenate([o[:, g * TQ:(g + 1) * TQ] for g in range(GQA)], axis=0)
    o_ref[...] = o4.T


def _attention(qt, k, vt):
    B, _, S = qt.shape
    nk = vt.shape[1]
    return pl.pallas_call(
        _attention_kernel,
        grid=(B, ATT_KV_HEADS, S // TQ),
        in_specs=[
            pl.BlockSpec((None, GQA * HEAD_DIM, TQ), lambda b, kv, qi: (b, kv, qi)),
            pl.BlockSpec((None, S, KV_WIDTH), lambda b, kv, qi: (b, 0, 0)),
            pl.BlockSpec((None, nk, HEAD_DIM, TK), lambda b, kv, qi: (b, 0, kv, 0)),
        ],
        out_specs=pl.BlockSpec((None, TQ, GQA * HEAD_DIM), lambda b, kv, qi: (b, qi, kv)),
        out_shape=jax.ShapeDtypeStruct((B, S, ATT_WIDTH), jnp.float32),
        scratch_shapes=[pltpu.VMEM((KV_WIDTH, NQ), jnp.bfloat16)],
        compiler_params=pltpu.CompilerParams(
            dimension_semantics=("arbitrary", "arbitrary", "arbitrary"),
            vmem_limit_bytes=VMEM_LIMIT),
        name="attention",
    )(qt, k, vt)


def _silu(x):
    return x * jax.nn.sigmoid(x)


def _gelu(x):
    return 0.5 * x * (1.0 + lax.erf(x * (2.0 ** -0.5)))


def _layer_norm(x, g, b):
    mu = jnp.mean(x, axis=-1, keepdims=True)
    xc = x - mu
    var = jnp.mean(xc * xc, axis=-1, keepdims=True)
    return xc * lax.rsqrt(var + EPS) * g + b


def _glu(a):
    return a[:, :CONV_WIDTH] * jax.nn.sigmoid(a[:, CONV_WIDTH:])


CONV_ROWS = 64


def _post_kernel(x_ref, rest_ref, prev_ref, next_ref, att_ref, wout_ref, gpost_ref,
                 dw_ref, dwb_ref, cg_ref, cb_ref, sg_ref, sb_ref, sw_ref, sbias_ref,
                 o_ref, hext_ref, cat_ref):
    si = pl.program_id(1)
    last = pl.num_programs(1) - 1

    cat_ref[:, 0:ATT_WIDTH] = _bf16(att_ref[...] * _silu(rest_ref[:, R_GATT:R_GATT + ATT_WIDTH]))

    hext_ref[0:HALO, :] = jnp.where(si > 0, _glu(prev_ref[...]), 0.0)
    hext_ref[HALO:HALO + TM, :] = _glu(rest_ref[:, R_ACONV:R_ACONV + 2 * CONV_WIDTH])
    hext_ref[HALO + TM:2 * HALO + TM, :] = jnp.where(si < last, _glu(next_ref[...]), 0.0)
    for rb in range(TM // CONV_ROWS):
        base = rb * CONV_ROWS + HALO - CONV_PAD
        acc = jnp.zeros((CONV_ROWS, CONV_WIDTH), jnp.float32) + dwb_ref[...]
        for j in range(CONV_KERNEL):
            acc = acc + hext_ref[base + j:base + j + CONV_ROWS, :] * dw_ref[j:j + 1, :]
        rows = slice(rb * CONV_ROWS, (rb + 1) * CONV_ROWS)
        c = _silu(_layer_norm(acc, cg_ref[...], cb_ref[...]))
        c = c * _silu(rest_ref[rows, R_GCONV:R_GCONV + CONV_WIDTH])
        cat_ref[rows, ATT_WIDTH:ATT_WIDTH + CONV_WIDTH] = _bf16(c)

    lane = lax.broadcasted_iota(jnp.int32, (1, SG_WIDTH), 1)
    for c in range(TM // SG_CHUNK):
        rows = slice(c * SG_CHUNK, (c + 1) * SG_CHUNK)
        vln = _bf16(_layer_norm(_gelu(rest_ref[rows, R_V:R_V + SG_WIDTH]), sg_ref[...], sb_ref[...]))
        mixed = sbias_ref[...]
        for hd in range(SG_HEADS):
            mh = jnp.dot(sw_ref[hd], vln, preferred_element_type=jnp.float32)
            mixed = mixed + jnp.where(lane // HEAD_DIM == hd, mh, 0.0)
        sgu = _gelu(rest_ref[rows, R_U:R_U + SG_WIDTH]) * mixed
        sgu = sgu * _silu(rest_ref[rows, R_GSG:R_GSG + SG_WIDTH])
        cat_ref[rows, ATT_WIDTH + CONV_WIDTH:D_MIX] = _bf16(sgu)

    mix = jnp.dot(cat_ref[...], wout_ref[...], preferred_element_type=jnp.float32)
    ms = jnp.mean(mix * mix, axis=-1, keepdims=True)
    o_ref[...] = x_ref[...] + mix * lax.rsqrt(ms + EPS) * gpost_ref[...]


def _post(x, rest, att, wout, gpost, dw, dwb, cg, cb, sg, sb, sw, sbias):
    B, S, _ = x.shape
    ns = S // TM
    hb = TM // HALO
    const2 = lambda b, s: (0, 0)
    const3 = lambda b, s: (0, 0, 0)
    aconv_blk = R_ACONV // (2 * CONV_WIDTH)
    return pl.pallas_call(
        _post_kernel,
        grid=(B, ns),
        in_specs=[
            pl.BlockSpec((None, TM, D_MODEL), lambda b, s: (b, s, 0)),
            pl.BlockSpec((None, TM, REST_WIDTH), lambda b, s: (b, s, 0)),
            pl.BlockSpec((None, HALO, 2 * CONV_WIDTH),
                         lambda b, s: (b, jnp.maximum(s * hb - 1, 0), aconv_blk)),
            pl.BlockSpec((None, HALO, 2 * CONV_WIDTH),
                         lambda b, s: (b, jnp.minimum((s + 1) * hb, S // HALO - 1), aconv_blk)),
            pl.BlockSpec((None, TM, ATT_WIDTH), lambda b, s: (b, s, 0)),
            pl.BlockSpec((D_MIX, D_MODEL), const2),
            pl.BlockSpec((1, D_MODEL), const2),
            pl.BlockSpec((CONV_KERNEL, CONV_WIDTH), const2),
            pl.BlockSpec((1, CONV_WIDTH), const2),
            pl.BlockSpec((1, CONV_WIDTH), const2),
            pl.BlockSpec((1, CONV_WIDTH), const2),
            pl.BlockSpec((1, SG_WIDTH), const2),
            pl.BlockSpec((1, SG_WIDTH), const2),
            pl.BlockSpec((SG_HEADS, SG_CHUNK, SG_CHUNK), const3),
            pl.BlockSpec((SG_CHUNK, SG_WIDTH), const2),
        ],
        out_specs=pl.BlockSpec((None, TM, D_MODEL), lambda b, s: (b, s, 0)),
        out_shape=jax.ShapeDtypeStruct((B, S, D_MODEL), jnp.float32),
        scratch_shapes=[
            pltpu.VMEM((TM + 2 * HALO, CONV_WIDTH), jnp.float32),
            pltpu.VMEM((TM, D_MIX), jnp.bfloat16),
        ],
        compiler_params=pltpu.CompilerParams(
            dimension_semantics=("arbitrary", "arbitrary"), vmem_limit_bytes=VMEM_LIMIT),
        name="post",
    )(x, rest, rest, rest, att, wout, gpost, dw, dwb, cg, cb, sg, sb, sw, sbias)


def _rope_tables(S):
    half = ROPE_AXIS_DIM // 2
    t = jnp.arange(S, dtype=jnp.int32)
    row = (t // GRID_W).astype(jnp.float32)
    col = (t % GRID_W).astype(jnp.float32)
    inv_freq = ROPE_THETA ** (-jnp.arange(half, dtype=jnp.float32) / half)
    ang_r = row[:, None] * inv_freq[None, :]
    ang_c = col[:, None] * inv_freq[None, :]
    cos_h = jnp.concatenate([jnp.cos(ang_r)] * 2 + [jnp.cos(ang_c)] * 2, axis=-1)
    sin_h = jnp.concatenate([-jnp.sin(ang_r), jnp.sin(ang_r), -jnp.sin(ang_c), jnp.sin(ang_c)], axis=-1)
    reps = LANES // HEAD_DIM
    return jnp.tile(cos_h, (1, reps)), jnp.tile(sin_h, (1, reps))


def kernel(x, pre_norm, post_norm, w_in, w_out, q_norm, k_norm, conv_dw, conv_dw_b,
           conv_ln_g, conv_ln_b, sg_ln_g, sg_ln_b, sg_w, sg_b):
    B, S, _ = x.shape
    depth = w_in.shape[0]
    cos_t, sin_t = _rope_tables(S)
    head_id = jnp.arange(ATT_WIDTH, dtype=jnp.int32) // HEAD_DIM
    bd = _bf16(jnp.where(head_id[:, None] == head_id[None, :], 1.0 / HEAD_DIM, 0.0))
    row2 = lambda a: a.reshape(1, -1)
    for l in range(depth):
        qt, k, vt, rest = _in_proj(
            x, row2(pre_norm[l]), _bf16(w_in[l]),
            row2(jnp.tile(q_norm[l], ATT_HEADS)), row2(jnp.tile(k_norm[l], ATT_KV_HEADS)),
            bd, cos_t, sin_t)
        att = _attention(qt, k, vt)
        sbias = jnp.repeat(sg_b[l].T, HEAD_DIM, axis=1)
        x = _post(x, rest, att, _bf16(w_out[l]), row2(post_norm[l]),
                  conv_dw[l], row2(conv_dw_b[l]), row2(conv_ln_g[l]), row2(conv_ln_b[l]),
                  row2(sg_ln_g[l]), row2(sg_ln_b[l]), _bf16(sg_w[l]), sbias)
    return x
```

```python
import math

import jax
import jax.numpy as jnp
from jax import lax
from jax.experimental import pallas as pl
from jax.experimental.pallas import tpu as pltpu

D_MODEL = 1024
HEAD_DIM = 64
GRID_W = 64
EPS = 1e-6
ATT_HEADS = 8
ATT_KV_HEADS = 2
GQA = ATT_HEADS // ATT_KV_HEADS
ATT_WIDTH = ATT_HEADS * HEAD_DIM
KV_WIDTH = ATT_KV_HEADS * HEAD_DIM
ROPE_THETA = 10000.0
ROPE_AXIS_DIM = HEAD_DIM // 2
CONV_WIDTH = 256
CONV_KERNEL = 31
CONV_PAD = CONV_KERNEL // 2
SG_HEADS = 4
SG_WIDTH = SG_HEADS * HEAD_DIM
SG_CHUNK = 128
D_MIX = ATT_WIDTH + CONV_WIDTH + SG_WIDTH
QKV_WIDTH = ATT_WIDTH + 2 * KV_WIDTH
REST_WIDTH = ATT_WIDTH + 3 * CONV_WIDTH + 3 * SG_WIDTH
D_IN = QKV_WIDTH + REST_WIDTH

R_GATT = 0
R_ACONV = ATT_WIDTH
R_GCONV = R_ACONV + 2 * CONV_WIDTH
R_U = R_GCONV + CONV_WIDTH
R_V = R_U + SG_WIDTH
R_GSG = R_V + SG_WIDTH

LANES = 128
HALO = 16
TM = 512
TQ = 128
TK = 512
NQ = GQA * TQ
V_ROWS = HEAD_DIM + 16
SCORE_SCALE = HEAD_DIM ** -0.5 * math.log2(math.e)
VMEM_LIMIT = 56 * 1024 * 1024


def _bf16(x):
    return x.astype(jnp.bfloat16)


def _group_mean_sq(x, bd):
    x2 = x * x
    hi = _bf16(x2)
    lo = _bf16(x2 - hi.astype(jnp.float32))
    return (jnp.dot(hi, bd, preferred_element_type=jnp.float32)
            + jnp.dot(lo, bd, preferred_element_type=jnp.float32))


def _rope_slab(xn, cos, sin_signed, first_half):
    fwd = pltpu.roll(xn, LANES - ROPE_AXIS_DIM // 2, 1)
    bwd = pltpu.roll(xn, ROPE_AXIS_DIM // 2, 1)
    partner = jnp.where(first_half, fwd, bwd)
    return xn * cos + partner * sin_signed


def _in_proj_kernel(x_ref, g_ref, w_ref, gq_ref, gk_ref, bd_ref, cos_ref, sin_ref,
                    qt_ref, k_ref, vt_ref, rest_ref):
    x = x_ref[...]
    ms = jnp.mean(x * x, axis=-1, keepdims=True)
    h = _bf16(x * lax.rsqrt(ms + EPS) * g_ref[...])

    cos = cos_ref[...]
    sin_signed = sin_ref[...]
    lane = lax.broadcasted_iota(jnp.int32, (1, LANES), 1)
    first_half = (lane % ROPE_AXIS_DIM) < (ROPE_AXIS_DIM // 2)

    q = jnp.dot(h, w_ref[:, 0:ATT_WIDTH], preferred_element_type=jnp.float32)
    qn = q * lax.rsqrt(_group_mean_sq(q, bd_ref[...]) + EPS) * gq_ref[...]
    for j in range(ATT_WIDTH // LANES):
        sl = slice(j * LANES, (j + 1) * LANES)
        r = _rope_slab(qn[:, sl], cos, sin_signed, first_half) * SCORE_SCALE
        qt_ref[sl, :] = _bf16(r.T)

    kk = jnp.dot(h, w_ref[:, ATT_WIDTH:ATT_WIDTH + KV_WIDTH], preferred_element_type=jnp.float32)
    kn = kk * lax.rsqrt(_group_mean_sq(kk, bd_ref[0:KV_WIDTH, 0:KV_WIDTH]) + EPS) * gk_ref[...]
    k_ref[...] = _bf16(_rope_slab(kn, cos, sin_signed, first_half))

    v = jnp.dot(h, w_ref[:, ATT_WIDTH + KV_WIDTH:QKV_WIDTH], preferred_element_type=jnp.float32)
    vt = _bf16(v.T)
    for hd in range(ATT_KV_HEADS):
        vt_ref[hd, 0:HEAD_DIM, :] = vt[hd * HEAD_DIM:(hd + 1) * HEAD_DIM, :]
        vt_ref[hd, HEAD_DIM:V_ROWS, :] = jnp.ones((V_ROWS - HEAD_DIM, TM), jnp.bfloat16)

    for c in range(REST_WIDTH // 512):
        rest_ref[:, c * 512:(c + 1) * 512] = jnp.dot(
            h, w_ref[:, QKV_WIDTH + c * 512:QKV_WIDTH + (c + 1) * 512],
            preferred_element_type=jnp.float32)


def _in_proj(x, g, w, gq, gk, bd, cos_t, sin_t):
    B, S, _ = x.shape
    ns = S // TM
    const = lambda b, s: (0, 0)
    return pl.pallas_call(
        _in_proj_kernel,
        grid=(B, ns),
        in_specs=[
            pl.BlockSpec((None, TM, D_MODEL), lambda b, s: (b, s, 0)),
            pl.BlockSpec((1, D_MODEL), const),
            pl.BlockSpec((D_MODEL, D_IN), const),
            pl.BlockSpec((1, ATT_WIDTH), const),
            pl.BlockSpec((1, KV_WIDTH), const),
            pl.BlockSpec((ATT_WIDTH, ATT_WIDTH), const),
            pl.BlockSpec((TM, LANES), lambda b, s: (s, 0)),
            pl.BlockSpec((TM, LANES), lambda b, s: (s, 0)),
        ],
        out_specs=[
            pl.BlockSpec((None, ATT_WIDTH, TM), lambda b, s: (b, 0, s)),
            pl.BlockSpec((None, TM, KV_WIDTH), lambda b, s: (b, s, 0)),
            pl.BlockSpec((None, None, ATT_KV_HEADS, V_ROWS, TM), lambda b, s: (b, s, 0, 0, 0)),
            pl.BlockSpec((None, TM, REST_WIDTH), lambda b, s: (b, s, 0)),
        ],
        out_shape=[
            jax.ShapeDtypeStruct((B, ATT_WIDTH, S), jnp.bfloat16),
            jax.ShapeDtypeStruct((B, S, KV_WIDTH), jnp.bfloat16),
            jax.ShapeDtypeStruct((B, ns, ATT_KV_HEADS, V_ROWS, TM), jnp.bfloat16),
            jax.ShapeDtypeStruct((B, S, REST_WIDTH), jnp.float32),
        ],
        compiler_params=pltpu.CompilerParams(
            dimension_semantics=("arbitrary", "arbitrary"), vmem_limit_bytes=VMEM_LIMIT),
        name="in_proj",
    )(x, g, w, gq, gk, bd, cos_t, sin_t)


def _attention_kernel(qt_ref, k_ref, vt_ref, o_ref, qext_ref, s0_ref, s1_ref, p0_ref, p1_ref, acc_ref):
    kv = pl.program_id(1)
    n = k_ref.shape[0] // TK
    for g in range(GQA):
        qg = qt_ref[g * HEAD_DIM:(g + 1) * HEAD_DIM, :]
        zero = jnp.zeros_like(qg)
        qext_ref[0:HEAD_DIM, g * TQ:(g + 1) * TQ] = jnp.where(kv == 0, qg, zero)
        qext_ref[HEAD_DIM:2 * HEAD_DIM, g * TQ:(g + 1) * TQ] = jnp.where(kv == 1, qg, zero)

    def scores(j, s_ref):
        kblk = k_ref[pl.ds(pl.multiple_of(j * TK, TK), TK), :]
        s = jnp.dot(kblk, qext_ref[...], preferred_element_type=jnp.float32)
        s_ref[...] = s
        return jnp.max(s, axis=0, keepdims=True)

    def half(j, m, mx, s_cur, s_nxt, p_cur, p_prev):
        mx_nxt = scores(jnp.minimum(j + 1, n - 1), s_nxt)
        m_new = jnp.maximum(m, mx)
        alpha = jnp.exp2(m - m_new)
        p_cur[...] = _bf16(jnp.exp2(s_cur[...] - m_new))
        pv = jnp.dot(vt_ref[jnp.maximum(j - 1, 0)], p_prev[...], preferred_element_type=jnp.float32)
        acc_ref[...] = alpha * (acc_ref[...] + pv)
        return m_new, mx_nxt

    acc_ref[...] = jnp.zeros_like(acc_ref)
    p1_ref[...] = jnp.zeros_like(p1_ref)
    mx0 = scores(0, s0_ref)

    def pair(i, carry):
        m, mx = carry
        m, mx = half(2 * i, m, mx, s0_ref, s1_ref, p0_ref, p1_ref)
        m, mx = half(2 * i + 1, m, mx, s1_ref, s0_ref, p1_ref, p0_ref)
        return m, mx

    lax.fori_loop(0, n // 2, pair, (jnp.full((1, NQ), -jnp.inf, jnp.float32), mx0))
    acc = acc_ref[...] + jnp.dot(vt_ref[n - 1], p1_ref[...], preferred_element_type=jnp.float32)
    o = acc[0:HEAD_DIM, :] / acc[HEAD_DIM:HEAD_DIM + 1, :]
    o4 = jnp.concatenate([o[:, g * TQ:(g + 1) * TQ] for g in range(GQA)], axis=0)
    o_ref[...] = o4.T


def _attention(qt, k, vt):
    B, _, S = qt.shape
    nk = vt.shape[1]
    assert vt.shape[-1] == TK and nk % 2 == 0
    return pl.pallas_call(
        _attention_kernel,
        grid=(B, ATT_KV_HEADS, S // TQ),
        in_specs=[
            pl.BlockSpec((None, GQA * HEAD_DIM, TQ), lambda b, kv, qi: (b, kv, qi)),
            pl.BlockSpec((None, S, KV_WIDTH), lambda b, kv, qi: (b, 0, 0)),
            pl.BlockSpec((None, nk, None, V_ROWS, TK), lambda b, kv, qi: (b, 0, kv, 0, 0)),
        ],
        out_specs=pl.BlockSpec((None, TQ, GQA * HEAD_DIM), lambda b, kv, qi: (b, qi, kv)),
        out_shape=jax.ShapeDtypeStruct((B, S, ATT_WIDTH), jnp.float32),
        scratch_shapes=[
            pltpu.VMEM((KV_WIDTH, NQ), jnp.bfloat16),
            pltpu.VMEM((TK, NQ), jnp.float32),
            pltpu.VMEM((TK, NQ), jnp.float32),
            pltpu.VMEM((TK, NQ), jnp.bfloat16),
            pltpu.VMEM((TK, NQ), jnp.bfloat16),
            pltpu.VMEM((V_ROWS, NQ), jnp.float32),
        ],
        compiler_params=pltpu.CompilerParams(
            dimension_semantics=("arbitrary", "arbitrary", "arbitrary"),
            vmem_limit_bytes=VMEM_LIMIT),
        name="attention",
    )(qt, k, vt)


def _silu(x):
    return x * jax.nn.sigmoid(x)


def _gelu(x):
    return 0.5 * x * (1.0 + lax.erf(x * (2.0 ** -0.5)))


def _layer_norm(x, g, b):
    mu = jnp.mean(x, axis=-1, keepdims=True)
    xc = x - mu
    var = jnp.mean(xc * xc, axis=-1, keepdims=True)
    return xc * lax.rsqrt(var + EPS) * g + b


def _glu(a):
    return a[:, :CONV_WIDTH] * jax.nn.sigmoid(a[:, CONV_WIDTH:])


CONV_ROWS = 64


def _post_kernel(x_ref, rest_ref, prev_ref, next_ref, att_ref, wout_ref, gpost_ref,
                 dw_ref, dwb_ref, cg_ref, cb_ref, sg_ref, sb_ref, sw_ref, sbias_ref,
                 o_ref, hext_ref, cat_ref):
    si = pl.program_id(1)
    last = pl.num_programs(1) - 1

    cat_ref[:, 0:ATT_WIDTH] = _bf16(att_ref[...] * _silu(rest_ref[:, R_GATT:R_GATT + ATT_WIDTH]))

    hext_ref[0:HALO, :] = jnp.where(si > 0, _glu(prev_ref[...]), 0.0)
    hext_ref[HALO:HALO + TM, :] = _glu(rest_ref[:, R_ACONV:R_ACONV + 2 * CONV_WIDTH])
    hext_ref[HALO + TM:2 * HALO + TM, :] = jnp.where(si < last, _glu(next_ref[...]), 0.0)
    for rb in range(TM // CONV_ROWS):
        base = rb * CONV_ROWS + HALO - CONV_PAD
        acc = jnp.zeros((CONV_ROWS, CONV_WIDTH), jnp.float32) + dwb_ref[...]
        for j in range(CONV_KERNEL):
            acc = acc + hext_ref[base + j:base + j + CONV_ROWS, :] * dw_ref[j:j + 1, :]
        rows = slice(rb * CONV_ROWS, (rb + 1) * CONV_ROWS)
        c = _silu(_layer_norm(acc, cg_ref[...], cb_ref[...]))
        c = c * _silu(rest_ref[rows, R_GCONV:R_GCONV + CONV_WIDTH])
        cat_ref[rows, ATT_WIDTH:ATT_WIDTH + CONV_WIDTH] = _bf16(c)

    lane = lax.broadcasted_iota(jnp.int32, (1, SG_WIDTH), 1)
    for c in range(TM // SG_CHUNK):
        rows = slice(c * SG_CHUNK, (c + 1) * SG_CHUNK)
        vln = _bf16(_layer_norm(_gelu(rest_ref[rows, R_V:R_V + SG_WIDTH]), sg_ref[...], sb_ref[...]))
        mixed = sbias_ref[...]
        for hd in range(SG_HEADS):
            mh = jnp.dot(sw_ref[hd], vln, preferred_element_type=jnp.float32)
            mixed = mixed + jnp.where(lane // HEAD_DIM == hd, mh, 0.0)
        sgu = _gelu(rest_ref[rows, R_U:R_U + SG_WIDTH]) * mixed
        sgu = sgu * _silu(rest_ref[rows, R_GSG:R_GSG + SG_WIDTH])
        cat_ref[rows, ATT_WIDTH + CONV_WIDTH:D_MIX] = _bf16(sgu)

    mix = jnp.dot(cat_ref[...], wout_ref[...], preferred_element_type=jnp.float32)
    ms = jnp.mean(mix * mix, axis=-1, keepdims=True)
    o_ref[...] = x_ref[...] + mix * lax.rsqrt(ms + EPS) * gpost_ref[...]


def _post(x, rest, att, wout, gpost, dw, dwb, cg, cb, sg, sb, sw, sbias):
    B, S, _ = x.shape
    ns = S // TM
    hb = TM // HALO
    const2 = lambda b, s: (0, 0)
    const3 = lambda b, s: (0, 0, 0)
    aconv_blk = R_ACONV // (2 * CONV_WIDTH)
    return pl.pallas_call(
        _post_kernel,
        grid=(B, ns),
        in_specs=[
            pl.BlockSpec((None, TM, D_MODEL), lambda b, s: (b, s, 0)),
            pl.BlockSpec((None, TM, REST_WIDTH), lambda b, s: (b, s, 0)),
            pl.BlockSpec((None, HALO, 2 * CONV_WIDTH),
                         lambda b, s: (b, jnp.maximum(s * hb - 1, 0), aconv_blk)),
            pl.BlockSpec((None, HALO, 2 * CONV_WIDTH),
                         lambda b, s: (b, jnp.minimum((s + 1) * hb, S // HALO - 1), aconv_blk)),
            pl.BlockSpec((None, TM, ATT_WIDTH), lambda b, s: (b, s, 0)),
            pl.BlockSpec((D_MIX, D_MODEL), const2),
            pl.BlockSpec((1, D_MODEL), const2),
            pl.BlockSpec((CONV_KERNEL, CONV_WIDTH), const2),
            pl.BlockSpec((1, CONV_WIDTH), const2),
            pl.BlockSpec((1, CONV_WIDTH), const2),
            pl.BlockSpec((1, CONV_WIDTH), const2),
            pl.BlockSpec((1, SG_WIDTH), const2),
            pl.BlockSpec((1, SG_WIDTH), const2),
            pl.BlockSpec((SG_HEADS, SG_CHUNK, SG_CHUNK), const3),
            pl.BlockSpec((SG_CHUNK, SG_WIDTH), const2),
        ],
        out_specs=pl.BlockSpec((None, TM, D_MODEL), lambda b, s: (b, s, 0)),
        out_shape=jax.ShapeDtypeStruct((B, S, D_MODEL), jnp.float32),
        scratch_shapes=[
            pltpu.VMEM((TM + 2 * HALO, CONV_WIDTH), jnp.float32),
            pltpu.VMEM((TM, D_MIX), jnp.bfloat16),
        ],
        compiler_params=pltpu.CompilerParams(
            dimension_semantics=("arbitrary", "arbitrary"), vmem_limit_bytes=VMEM_LIMIT),
        name="post",
    )(x, rest, rest, rest, att, wout, gpost, dw, dwb, cg, cb, sg, sb, sw, sbias)


def _rope_tables(S):
    half = ROPE_AXIS_DIM // 2
    t = jnp.arange(S, dtype=jnp.int32)
    row = (t // GRID_W).astype(jnp.float32)
    col = (t % GRID_W).astype(jnp.float32)
    inv_freq = ROPE_THETA ** (-jnp.arange(half, dtype=jnp.float32) / half)
    ang_r = row[:, None] * inv_freq[None, :]
    ang_c = col[:, None] * inv_freq[None, :]
    cos_h = jnp.concatenate([jnp.cos(ang_r)] * 2 + [jnp.cos(ang_c)] * 2, axis=-1)
    sin_h = jnp.concatenate([-jnp.sin(ang_r), jnp.sin(ang_r), -jnp.sin(ang_c), jnp.sin(ang_c)], axis=-1)
    reps = LANES // HEAD_DIM
    return jnp.tile(cos_h, (1, reps)), jnp.tile(sin_h, (1, reps))


def kernel(x, pre_norm, post_norm, w_in, w_out, q_norm, k_norm, conv_dw, conv_dw_b,
           conv_ln_g, conv_ln_b, sg_ln_g, sg_ln_b, sg_w, sg_b):
    B, S, _ = x.shape
    depth = w_in.shape[0]
    cos_t, sin_t = _rope_tables(S)
    head_id = jnp.arange(ATT_WIDTH, dtype=jnp.int32) // HEAD_DIM
    bd = _bf16(jnp.where(head_id[:, None] == head_id[None, :], 1.0 / HEAD_DIM, 0.0))
    row2 = lambda a: a.reshape(1, -1)
    for l in range(depth):
        qt, k, vt, rest = _in_proj(
            x, row2(pre_norm[l]), _bf16(w_in[l]),
            row2(jnp.tile(q_norm[l], ATT_HEADS)), row2(jnp.tile(k_norm[l], ATT_KV_HEADS)),
            bd, cos_t, sin_t)
        att = _attention(qt, k, vt)
        sbias = jnp.repeat(sg_b[l].T, HEAD_DIM, axis=1)
        x = _post(x, rest, att, _bf16(w_out[l]), row2(post_norm[l]),
                  conv_dw[l], row2(conv_dw_b[l]), row2(conv_ln_g[l]), row2(conv_ln_b[l]),
                  row2(sg_ln_g[l]), row2(sg_ln_b[l]), _bf16(sg_w[l]), sbias)
    return x
```

```python
import math

import jax
import jax.numpy as jnp
from jax import lax
from jax.experimental import pallas as pl
from jax.experimental.pallas import tpu as pltpu

D_MODEL = 1024
HEAD_DIM = 64
GRID_W = 64
EPS = 1e-6
ATT_HEADS = 8
ATT_KV_HEADS = 2
GQA = ATT_HEADS // ATT_KV_HEADS
ATT_WIDTH = ATT_HEADS * HEAD_DIM
KV_WIDTH = ATT_KV_HEADS * HEAD_DIM
ROPE_THETA = 10000.0
ROPE_AXIS_DIM = HEAD_DIM // 2
CONV_WIDTH = 256
CONV_KERNEL = 31
CONV_PAD = CONV_KERNEL // 2
SG_HEADS = 4
SG_WIDTH = SG_HEADS * HEAD_DIM
SG_CHUNK = 128
D_MIX = ATT_WIDTH + CONV_WIDTH + SG_WIDTH
QKV_WIDTH = ATT_WIDTH + 2 * KV_WIDTH
REST_WIDTH = ATT_WIDTH + 3 * CONV_WIDTH + 3 * SG_WIDTH
D_IN = QKV_WIDTH + REST_WIDTH

R_GATT = 0
R_ACONV = ATT_WIDTH
R_GCONV = R_ACONV + 2 * CONV_WIDTH
R_U = R_GCONV + CONV_WIDTH
R_V = R_U + SG_WIDTH
R_GSG = R_V + SG_WIDTH

LANES = 128
HALO = 16
TM = 512
TQ = 512
TK = 512
NQ = GQA * TQ
LANE_GROUP = 256
V_ROWS = HEAD_DIM + 16
SCORE_SCALE = HEAD_DIM ** -0.5 * math.log2(math.e)
VMEM_LIMIT = 56 * 1024 * 1024


def _bf16(x):
    return x.astype(jnp.bfloat16)


def _group_mean_sq(x, bd):
    x2 = x * x
    hi = _bf16(x2)
    lo = _bf16(x2 - hi.astype(jnp.float32))
    return (jnp.dot(hi, bd, preferred_element_type=jnp.float32)
            + jnp.dot(lo, bd, preferred_element_type=jnp.float32))


def _rope_slab(xn, cos, sin_signed, first_half):
    fwd = pltpu.roll(xn, LANES - ROPE_AXIS_DIM // 2, 1)
    bwd = pltpu.roll(xn, ROPE_AXIS_DIM // 2, 1)
    partner = jnp.where(first_half, fwd, bwd)
    return xn * cos + partner * sin_signed


def _in_proj_kernel(x_ref, g_ref, w_ref, gq_ref, gk_ref, bd_ref, cos_ref, sin_ref,
                    qt_ref, k_ref, vt_ref, rest_ref):
    x = x_ref[...]
    ms = jnp.mean(x * x, axis=-1, keepdims=True)
    h = _bf16(x * lax.rsqrt(ms + EPS) * g_ref[...])

    cos = cos_ref[...]
    sin_signed = sin_ref[...]
    lane = lax.broadcasted_iota(jnp.int32, (1, LANES), 1)
    first_half = (lane % ROPE_AXIS_DIM) < (ROPE_AXIS_DIM // 2)

    q = jnp.dot(h, w_ref[:, 0:ATT_WIDTH], preferred_element_type=jnp.float32)
    qn = q * lax.rsqrt(_group_mean_sq(q, bd_ref[...]) + EPS) * gq_ref[...]
    for j in range(ATT_WIDTH // LANES):
        sl = slice(j * LANES, (j + 1) * LANES)
        r = _rope_slab(qn[:, sl], cos, sin_signed, first_half) * SCORE_SCALE
        qt_ref[sl, :] = _bf16(r.T)

    kk = jnp.dot(h, w_ref[:, ATT_WIDTH:ATT_WIDTH + KV_WIDTH], preferred_element_type=jnp.float32)
    kn = kk * lax.rsqrt(_group_mean_sq(kk, bd_ref[0:KV_WIDTH, 0:KV_WIDTH]) + EPS) * gk_ref[...]
    k_ref[...] = _bf16(_rope_slab(kn, cos, sin_signed, first_half))

    v = jnp.dot(h, w_ref[:, ATT_WIDTH + KV_WIDTH:QKV_WIDTH], preferred_element_type=jnp.float32)
    vt = _bf16(v.T)
    for hd in range(ATT_KV_HEADS):
        vt_ref[hd, 0:HEAD_DIM, :] = vt[hd * HEAD_DIM:(hd + 1) * HEAD_DIM, :]
        vt_ref[hd, HEAD_DIM:V_ROWS, :] = jnp.ones((V_ROWS - HEAD_DIM, TM), jnp.bfloat16)

    for c in range(REST_WIDTH // 512):
        rest_ref[:, c * 512:(c + 1) * 512] = jnp.dot(
            h, w_ref[:, QKV_WIDTH + c * 512:QKV_WIDTH + (c + 1) * 512],
            preferred_element_type=jnp.float32)


def _in_proj(x, g, w, gq, gk, bd, cos_t, sin_t):
    B, S, _ = x.shape
    ns = S // TM
    const = lambda b, s: (0, 0)
    return pl.pallas_call(
        _in_proj_kernel,
        grid=(B, ns),
        in_specs=[
            pl.BlockSpec((None, TM, D_MODEL), lambda b, s: (b, s, 0)),
            pl.BlockSpec((1, D_MODEL), const),
            pl.BlockSpec((D_MODEL, D_IN), const),
            pl.BlockSpec((1, ATT_WIDTH), const),
            pl.BlockSpec((1, KV_WIDTH), const),
            pl.BlockSpec((ATT_WIDTH, ATT_WIDTH), const),
            pl.BlockSpec((TM, LANES), lambda b, s: (s, 0)),
            pl.BlockSpec((TM, LANES), lambda b, s: (s, 0)),
        ],
        out_specs=[
            pl.BlockSpec((None, ATT_WIDTH, TM), lambda b, s: (b, 0, s)),
            pl.BlockSpec((None, TM, KV_WIDTH), lambda b, s: (b, s, 0)),
            pl.BlockSpec((None, None, ATT_KV_HEADS, V_ROWS, TM), lambda b, s: (b, s, 0, 0, 0)),
            pl.BlockSpec((None, TM, REST_WIDTH), lambda b, s: (b, s, 0)),
        ],
        out_shape=[
            jax.ShapeDtypeStruct((B, ATT_WIDTH, S), jnp.bfloat16),
            jax.ShapeDtypeStruct((B, S, KV_WIDTH), jnp.bfloat16),
            jax.ShapeDtypeStruct((B, ns, ATT_KV_HEADS, V_ROWS, TM), jnp.bfloat16),
            jax.ShapeDtypeStruct((B, S, REST_WIDTH), jnp.float32),
        ],
        compiler_params=pltpu.CompilerParams(
            dimension_semantics=("arbitrary", "arbitrary"), vmem_limit_bytes=VMEM_LIMIT),
        name="in_proj",
    )(x, g, w, gq, gk, bd, cos_t, sin_t)


def _attention_kernel(qt_ref, k_ref, vt_ref, o_ref, qext_ref, s_ref, p_ref, acc_ref, m_ref, mx_ref):
    kv = pl.program_id(1)
    n = k_ref.shape[0] // TK
    for g in range(GQA):
        qg = qt_ref[g * HEAD_DIM:(g + 1) * HEAD_DIM, :]
        zero = jnp.zeros_like(qg)
        qext_ref[0:HEAD_DIM, g * TQ:(g + 1) * TQ] = jnp.where(kv == 0, qg, zero)
        qext_ref[HEAD_DIM:2 * HEAD_DIM, g * TQ:(g + 1) * TQ] = jnp.where(kv == 1, qg, zero)

    def scores(j, cols):
        kblk = k_ref[pl.ds(pl.multiple_of(j * TK, TK), TK), :]
        s = jnp.dot(kblk, qext_ref[:, cols], preferred_element_type=jnp.float32)
        s_ref[:, cols] = s
        mx_ref[:, cols] = jnp.max(s, axis=0, keepdims=True)

    def values(j, cols):
        return jnp.dot(vt_ref[j], p_ref[:, cols], preferred_element_type=jnp.float32)

    groups = [slice(g * LANE_GROUP, (g + 1) * LANE_GROUP) for g in range(NQ // LANE_GROUP)]
    acc_ref[...] = jnp.zeros_like(acc_ref)
    p_ref[...] = jnp.zeros_like(p_ref)
    m_ref[...] = jnp.full_like(m_ref, -jnp.inf)
    for cols in groups:
        scores(0, cols)

    def step(j, carry):
        for cols in groups:
            pv = values(jnp.maximum(j - 1, 0), cols)
            m_old = m_ref[:, cols]
            m_new = jnp.maximum(m_old, mx_ref[:, cols])
            m_ref[:, cols] = m_new
            p_ref[:, cols] = _bf16(jnp.exp2(s_ref[:, cols] - m_new))
            acc_ref[:, cols] = jnp.exp2(m_old - m_new) * (acc_ref[:, cols] + pv)
            scores(jnp.minimum(j + 1, n - 1), cols)
        return carry

    lax.fori_loop(0, n, step, 0)
    for cols in groups:
        acc_ref[:, cols] = acc_ref[:, cols] + values(n - 1, cols)
    o = acc_ref[0:HEAD_DIM, :] / acc_ref[HEAD_DIM:HEAD_DIM + 1, :]
    o4 = jnp.concatenate([o[:, g * TQ:(g + 1) * TQ] for g in range(GQA)], axis=0)
    o_ref[...] = o4.T


def _attention(qt, k, vt):
    B, _, S = qt.shape
    nk = vt.shape[1]
    assert vt.shape[-1] == TK
    return pl.pallas_call(
        _attention_kernel,
        grid=(B, ATT_KV_HEADS, S // TQ),
        in_specs=[
            pl.BlockSpec((None, GQA * HEAD_DIM, TQ), lambda b, kv, qi: (b, kv, qi)),
            pl.BlockSpec((None, S, KV_WIDTH), lambda b, kv, qi: (b, 0, 0)),
            pl.BlockSpec((None, nk, None, V_ROWS, TK), lambda b, kv, qi: (b, 0, kv, 0, 0)),
        ],
        out_specs=pl.BlockSpec((None, TQ, GQA * HEAD_DIM), lambda b, kv, qi: (b, qi, kv)),
        out_shape=jax.ShapeDtypeStruct((B, S, ATT_WIDTH), jnp.float32),
        scratch_shapes=[
            pltpu.VMEM((KV_WIDTH, NQ), jnp.bfloat16),
            pltpu.VMEM((TK, NQ), jnp.float32),
            pltpu.VMEM((TK, NQ), jnp.bfloat16),
            pltpu.VMEM((V_ROWS, NQ), jnp.float32),
            pltpu.VMEM((1, NQ), jnp.float32),
            pltpu.VMEM((1, NQ), jnp.float32),
        ],
        compiler_params=pltpu.CompilerParams(
            dimension_semantics=("arbitrary", "arbitrary", "arbitrary"),
            vmem_limit_bytes=VMEM_LIMIT),
        name="attention",
    )(qt, k, vt)


def _silu(x):
    return x * jax.nn.sigmoid(x)


def _gelu(x):
    return 0.5 * x * (1.0 + lax.erf(x * (2.0 ** -0.5)))


def _layer_norm(x, g, b):
    mu = jnp.mean(x, axis=-1, keepdims=True)
    xc = x - mu
    var = jnp.mean(xc * xc, axis=-1, keepdims=True)
    return xc * lax.rsqrt(var + EPS) * g + b


def _glu(a):
    return a[:, :CONV_WIDTH] * jax.nn.sigmoid(a[:, CONV_WIDTH:])


CONV_ROWS = 64


def _post_kernel(x_ref, rest_ref, prev_ref, next_ref, att_ref, wout_ref, gpost_ref,
                 dw_ref, dwb_ref, cg_ref, cb_ref, sg_ref, sb_ref, sw_ref, sbias_ref,
                 o_ref, hext_ref, cat_ref):
    si = pl.program_id(1)
    last = pl.num_programs(1) - 1

    cat_ref[:, 0:ATT_WIDTH] = _bf16(att_ref[...] * _silu(rest_ref[:, R_GATT:R_GATT + ATT_WIDTH]))

    hext_ref[0:HALO, :] = jnp.where(si > 0, _glu(prev_ref[...]), 0.0)
    hext_ref[HALO:HALO + TM, :] = _glu(rest_ref[:, R_ACONV:R_ACONV + 2 * CONV_WIDTH])
    hext_ref[HALO + TM:2 * HALO + TM, :] = jnp.where(si < last, _glu(next_ref[...]), 0.0)
    for rb in range(TM // CONV_ROWS):
        base = rb * CONV_ROWS + HALO - CONV_PAD
        acc = jnp.zeros((CONV_ROWS, CONV_WIDTH), jnp.float32) + dwb_ref[...]
        for j in range(CONV_KERNEL):
            acc = acc + hext_ref[base + j:base + j + CONV_ROWS, :] * dw_ref[j:j + 1, :]
        rows = slice(rb * CONV_ROWS, (rb + 1) * CONV_ROWS)
        c = _silu(_layer_norm(acc, cg_ref[...], cb_ref[...]))
        c = c * _silu(rest_ref[rows, R_GCONV:R_GCONV + CONV_WIDTH])
        cat_ref[rows, ATT_WIDTH:ATT_WIDTH + CONV_WIDTH] = _bf16(c)

    lane = lax.broadcasted_iota(jnp.int32, (1, SG_WIDTH), 1)
    for c in range(TM // SG_CHUNK):
        rows = slice(c * SG_CHUNK, (c + 1) * SG_CHUNK)
        vln = _bf16(_layer_norm(_gelu(rest_ref[rows, R_V:R_V + SG_WIDTH]), sg_ref[...], sb_ref[...]))
        mixed = sbias_ref[...]
        for hd in range(SG_HEADS):
            mh = jnp.dot(sw_ref[hd], vln, preferred_element_type=jnp.float32)
            mixed = mixed + jnp.where(lane // HEAD_DIM == hd, mh, 0.0)
        sgu = _gelu(rest_ref[rows, R_U:R_U + SG_WIDTH]) * mixed
        sgu = sgu * _silu(rest_ref[rows, R_GSG:R_GSG + SG_WIDTH])
        cat_ref[rows, ATT_WIDTH + CONV_WIDTH:D_MIX] = _bf16(sgu)

    mix = jnp.dot(cat_ref[...], wout_ref[...], preferred_element_type=jnp.float32)
    ms = jnp.mean(mix * mix, axis=-1, keepdims=True)
    o_ref[...] = x_ref[...] + mix * lax.rsqrt(ms + EPS) * gpost_ref[...]


def _post(x, rest, att, wout, gpost, dw, dwb, cg, cb, sg, sb, sw, sbias):
    B, S, _ = x.shape
    ns = S // TM
    hb = TM // HALO
    const2 = lambda b, s: (0, 0)
    const3 = lambda b, s: (0, 0, 0)
    aconv_blk = R_ACONV // (2 * CONV_WIDTH)
    return pl.pallas_call(
        _post_kernel,
        grid=(B, ns),
        in_specs=[
            pl.BlockSpec((None, TM, D_MODEL), lambda b, s: (b, s, 0)),
            pl.BlockSpec((None, TM, REST_WIDTH), lambda b, s: (b, s, 0)),
            pl.BlockSpec((None, HALO, 2 * CONV_WIDTH),
                         lambda b, s: (b, jnp.maximum(s * hb - 1, 0), aconv_blk)),
            pl.BlockSpec((None, HALO, 2 * CONV_WIDTH),
                         lambda b, s: (b, jnp.minimum((s + 1) * hb, S // HALO - 1), aconv_blk)),
            pl.BlockSpec((None, TM, ATT_WIDTH), lambda b, s: (b, s, 0)),
            pl.BlockSpec((D_MIX, D_MODEL), const2),
            pl.BlockSpec((1, D_MODEL), const2),
            pl.BlockSpec((CONV_KERNEL, CONV_WIDTH), const2),
            pl.BlockSpec((1, CONV_WIDTH), const2),
            pl.BlockSpec((1, CONV_WIDTH), const2),
            pl.BlockSpec((1, CONV_WIDTH), const2),
            pl.BlockSpec((1, SG_WIDTH), const2),
            pl.BlockSpec((1, SG_WIDTH), const2),
            pl.BlockSpec((SG_HEADS, SG_CHUNK, SG_CHUNK), const3),
            pl.BlockSpec((SG_CHUNK, SG_WIDTH), const2),
        ],
        out_specs=pl.BlockSpec((None, TM, D_MODEL), lambda b, s: (b, s, 0)),
        out_shape=jax.ShapeDtypeStruct((B, S, D_MODEL), jnp.float32),
        scratch_shapes=[
            pltpu.VMEM((TM + 2 * HALO, CONV_WIDTH), jnp.float32),
            pltpu.VMEM((TM, D_MIX), jnp.bfloat16),
        ],
        compiler_params=pltpu.CompilerParams(
            dimension_semantics=("arbitrary", "arbitrary"), vmem_limit_bytes=VMEM_LIMIT),
        name="post",
    )(x, rest, rest, rest, att, wout, gpost, dw, dwb, cg, cb, sg, sb, sw, sbias)


def _rope_tables(S):
    half = ROPE_AXIS_DIM // 2
    t = jnp.arange(S, dtype=jnp.int32)
    row = (t // GRID_W).astype(jnp.float32)
    col = (t % GRID_W).astype(jnp.float32)
    inv_freq = ROPE_THETA ** (-jnp.arange(half, dtype=jnp.float32) / half)
    ang_r = row[:, None] * inv_freq[None, :]
    ang_c = col[:, None] * inv_freq[None, :]
    cos_h = jnp.concatenate([jnp.cos(ang_r)] * 2 + [jnp.cos(ang_c)] * 2, axis=-1)
    sin_h = jnp.concatenate([-jnp.sin(ang_r), jnp.sin(ang_r), -jnp.sin(ang_c), jnp.sin(ang_c)], axis=-1)
    reps = LANES // HEAD_DIM
    return jnp.tile(cos_h, (1, reps)), jnp.tile(sin_h, (1, reps))


def kernel(x, pre_norm, post_norm, w_in, w_out, q_norm, k_norm, conv_dw, conv_dw_b,
           conv_ln_g, conv_ln_b, sg_ln_g, sg_ln_b, sg_w, sg_b):
    B, S, _ = x.shape
    depth = w_in.shape[0]
    cos_t, sin_t = _rope_tables(S)
    head_id = jnp.arange(ATT_WIDTH, dtype=jnp.int32) // HEAD_DIM
    bd = _bf16(jnp.where(head_id[:, None] == head_id[None, :], 1.0 / HEAD_DIM, 0.0))
    row2 = lambda a: a.reshape(1, -1)
    for l in range(depth):
        qt, k, vt, rest = _in_proj(
            x, row2(pre_norm[l]), _bf16(w_in[l]),
            row2(jnp.tile(q_norm[l], ATT_HEADS)), row2(jnp.tile(k_norm[l], ATT_KV_HEADS)),
            bd, cos_t, sin_t)
        att = _attention(qt, k, vt)
        sbias = jnp.repeat(sg_b[l].T, HEAD_DIM, axis=1)
        x = _post(x, rest, att, _bf16(w_out[l]), row2(post_norm[l]),
                  conv_dw[l], row2(conv_dw_b[l]), row2(conv_ln_g[l]), row2(conv_ln_b[l]),
                  row2(sg_ln_g[l]), row2(sg_ln_b[l]), _bf16(sg_w[l]), sbias)
    return x
```

```python
import math

import jax
import jax.numpy as jnp
from jax import lax
from jax.experimental import pallas as pl
from jax.experimental.pallas import tpu as pltpu

D_MODEL = 1024
HEAD_DIM = 64
GRID_W = 64
EPS = 1e-6
ATT_HEADS = 8
ATT_KV_HEADS = 2
GQA = ATT_HEADS // ATT_KV_HEADS
ATT_WIDTH = ATT_HEADS * HEAD_DIM
KV_WIDTH = ATT_KV_HEADS * HEAD_DIM
ROPE_THETA = 10000.0
ROPE_AXIS_DIM = HEAD_DIM // 2
CONV_WIDTH = 256
CONV_KERNEL = 31
CONV_PAD = CONV_KERNEL // 2
SG_HEADS = 4
SG_WIDTH = SG_HEADS * HEAD_DIM
SG_CHUNK = 128
D_MIX = ATT_WIDTH + CONV_WIDTH + SG_WIDTH
QKV_WIDTH = ATT_WIDTH + 2 * KV_WIDTH
REST_WIDTH = ATT_WIDTH + 3 * CONV_WIDTH + 3 * SG_WIDTH
D_IN = QKV_WIDTH + REST_WIDTH

R_GATT = 0
R_ACONV = ATT_WIDTH
R_GCONV = R_ACONV + 2 * CONV_WIDTH
R_U = R_GCONV + CONV_WIDTH
R_V = R_U + SG_WIDTH
R_GSG = R_V + SG_WIDTH

LANES = 128
HALO = 16
TM = 512
TQ = 512
TK = 512
NQ = GQA * TQ
LANE_GROUP = 256
V_ROWS = HEAD_DIM + 16
SCORE_SCALE = HEAD_DIM ** -0.5 * math.log2(math.e)
VMEM_LIMIT = 56 * 1024 * 1024


def _bf16(x):
    return x.astype(jnp.bfloat16)


def _group_mean_sq(x, bd):
    x2 = x * x
    hi = _bf16(x2)
    lo = _bf16(x2 - hi.astype(jnp.float32))
    return (jnp.dot(hi, bd, preferred_element_type=jnp.float32)
            + jnp.dot(lo, bd, preferred_element_type=jnp.float32))


def _rope_slab(xn, cos, sin_signed, first_half):
    fwd = pltpu.roll(xn, LANES - ROPE_AXIS_DIM // 2, 1)
    bwd = pltpu.roll(xn, ROPE_AXIS_DIM // 2, 1)
    partner = jnp.where(first_half, fwd, bwd)
    return xn * cos + partner * sin_signed


def _in_proj_kernel(x_ref, g_ref, w_ref, gq_ref, gk_ref, bd_ref, cos_ref, sin_ref,
                    qt_ref, k_ref, vt_ref, rest_ref):
    x = x_ref[...]
    ms = jnp.mean(x * x, axis=-1, keepdims=True)
    h = _bf16(x * lax.rsqrt(ms + EPS) * g_ref[...])

    cos = cos_ref[...]
    sin_signed = sin_ref[...]
    lane = lax.broadcasted_iota(jnp.int32, (1, LANES), 1)
    first_half = (lane % ROPE_AXIS_DIM) < (ROPE_AXIS_DIM // 2)

    q = jnp.dot(h, w_ref[:, 0:ATT_WIDTH], preferred_element_type=jnp.float32)
    qn = q * lax.rsqrt(_group_mean_sq(q, bd_ref[...]) + EPS) * gq_ref[...]
    for j in range(ATT_WIDTH // LANES):
        sl = slice(j * LANES, (j + 1) * LANES)
        r = _rope_slab(qn[:, sl], cos, sin_signed, first_half) * SCORE_SCALE
        qt_ref[sl, :] = _bf16(r.T)

    kk = jnp.dot(h, w_ref[:, ATT_WIDTH:ATT_WIDTH + KV_WIDTH], preferred_element_type=jnp.float32)
    kn = kk * lax.rsqrt(_group_mean_sq(kk, bd_ref[0:KV_WIDTH, 0:KV_WIDTH]) + EPS) * gk_ref[...]
    k_ref[...] = _bf16(_rope_slab(kn, cos, sin_signed, first_half))

    v = jnp.dot(h, w_ref[:, ATT_WIDTH + KV_WIDTH:QKV_WIDTH], preferred_element_type=jnp.float32)
    vt = _bf16(v.T)
    for hd in range(ATT_KV_HEADS):
        vt_ref[hd, 0:HEAD_DIM, :] = vt[hd * HEAD_DIM:(hd + 1) * HEAD_DIM, :]
        vt_ref[hd, HEAD_DIM:V_ROWS, :] = jnp.ones((V_ROWS - HEAD_DIM, TM), jnp.bfloat16)

    for c in range(REST_WIDTH // 512):
        rest_ref[:, c * 512:(c + 1) * 512] = jnp.dot(
            h, w_ref[:, QKV_WIDTH + c * 512:QKV_WIDTH + (c + 1) * 512],
            preferred_element_type=jnp.float32)


def _in_proj(x, g, w, gq, gk, bd, cos_t, sin_t):
    B, S, _ = x.shape
    ns = S // TM
    const = lambda b, s: (0, 0)
    return pl.pallas_call(
        _in_proj_kernel,
        grid=(B, ns),
        in_specs=[
            pl.BlockSpec((None, TM, D_MODEL), lambda b, s: (b, s, 0)),
            pl.BlockSpec((1, D_MODEL), const),
            pl.BlockSpec((D_MODEL, D_IN), const),
            pl.BlockSpec((1, ATT_WIDTH), const),
            pl.BlockSpec((1, KV_WIDTH), const),
            pl.BlockSpec((ATT_WIDTH, ATT_WIDTH), const),
            pl.BlockSpec((TM, LANES), lambda b, s: (s, 0)),
            pl.BlockSpec((TM, LANES), lambda b, s: (s, 0)),
        ],
        out_specs=[
            pl.BlockSpec((None, ATT_WIDTH, TM), lambda b, s: (b, 0, s)),
            pl.BlockSpec((None, TM, KV_WIDTH), lambda b, s: (b, s, 0)),
            pl.BlockSpec((None, None, ATT_KV_HEADS, V_ROWS, TM), lambda b, s: (b, s, 0, 0, 0)),
            pl.BlockSpec((None, TM, REST_WIDTH), lambda b, s: (b, s, 0)),
        ],
        out_shape=[
            jax.ShapeDtypeStruct((B, ATT_WIDTH, S), jnp.bfloat16),
            jax.ShapeDtypeStruct((B, S, KV_WIDTH), jnp.bfloat16),
            jax.ShapeDtypeStruct((B, ns, ATT_KV_HEADS, V_ROWS, TM), jnp.bfloat16),
            jax.ShapeDtypeStruct((B, S, REST_WIDTH), jnp.float32),
        ],
        compiler_params=pltpu.CompilerParams(
            dimension_semantics=("arbitrary", "arbitrary"), vmem_limit_bytes=VMEM_LIMIT),
        name="in_proj",
    )(x, g, w, gq, gk, bd, cos_t, sin_t)


def _attention_kernel(qt_ref, k_ref, vt_ref, o_ref, qext_ref, s_ref, p_ref, acc_ref, m_ref, mx_ref):
    kv = pl.program_id(1)
    n = k_ref.shape[0] // TK
    for g in range(GQA):
        qg = qt_ref[g * HEAD_DIM:(g + 1) * HEAD_DIM, :]
        zero = jnp.zeros_like(qg)
        qext_ref[0:HEAD_DIM, g * TQ:(g + 1) * TQ] = jnp.where(kv == 0, qg, zero)
        qext_ref[HEAD_DIM:2 * HEAD_DIM, g * TQ:(g + 1) * TQ] = jnp.where(kv == 1, qg, zero)

    def scores(j, g, cols):
        kblk = k_ref[pl.ds(pl.multiple_of(j * TK, TK), TK), :]
        s = jnp.dot(kblk, qext_ref[:, cols], preferred_element_type=jnp.float32)
        s_ref[g] = s
        mx_ref[:, cols] = jnp.max(s, axis=0, keepdims=True)

    def values(j, g):
        return jnp.dot(vt_ref[j], p_ref[g], preferred_element_type=jnp.float32)

    groups = [(g, slice(g * LANE_GROUP, (g + 1) * LANE_GROUP)) for g in range(NQ // LANE_GROUP)]
    acc_ref[...] = jnp.zeros_like(acc_ref)
    p_ref[...] = jnp.zeros_like(p_ref)
    m_ref[...] = jnp.full_like(m_ref, -jnp.inf)
    for g, cols in groups:
        scores(0, g, cols)

    def step(j, carry):
        for g, cols in groups:
            pv = values(jnp.maximum(j - 1, 0), g)
            m_old = m_ref[:, cols]
            m_new = jnp.maximum(m_old, mx_ref[:, cols])
            m_ref[:, cols] = m_new
            p_ref[g] = _bf16(jnp.exp2(s_ref[g] - m_new))
            acc_ref[:, cols] = jnp.exp2(m_old - m_new) * (acc_ref[:, cols] + pv)
            scores(jnp.minimum(j + 1, n - 1), g, cols)
        return carry

    lax.fori_loop(0, n, step, 0, unroll=2)
    for g, cols in groups:
        acc_ref[:, cols] = acc_ref[:, cols] + values(n - 1, g)
    o = acc_ref[0:HEAD_DIM, :] / acc_ref[HEAD_DIM:HEAD_DIM + 1, :]
    o4 = jnp.concatenate([o[:, g * TQ:(g + 1) * TQ] for g in range(GQA)], axis=0)
    o_ref[...] = o4.T


def _attention(qt, k, vt):
    B, _, S = qt.shape
    nk = vt.shape[1]
    assert vt.shape[-1] == TK
    return pl.pallas_call(
        _attention_kernel,
        grid=(B, ATT_KV_HEADS, S // TQ),
        in_specs=[
            pl.BlockSpec((None, GQA * HEAD_DIM, TQ), lambda b, kv, qi: (b, kv, qi)),
            pl.BlockSpec((None, S, KV_WIDTH), lambda b, kv, qi: (b, 0, 0)),
            pl.BlockSpec((None, nk, None, V_ROWS, TK), lambda b, kv, qi: (b, 0, kv, 0, 0)),
        ],
        out_specs=pl.BlockSpec((None, TQ, GQA * HEAD_DIM), lambda b, kv, qi: (b, qi, kv)),
        out_shape=jax.ShapeDtypeStruct((B, S, ATT_WIDTH), jnp.float32),
        scratch_shapes=[
            pltpu.VMEM((KV_WIDTH, NQ), jnp.bfloat16),
            pltpu.VMEM((NQ // LANE_GROUP, TK, LANE_GROUP), jnp.float32),
            pltpu.VMEM((NQ // LANE_GROUP, TK, LANE_GROUP), jnp.bfloat16),
            pltpu.VMEM((V_ROWS, NQ), jnp.float32),
            pltpu.VMEM((1, NQ), jnp.float32),
            pltpu.VMEM((1, NQ), jnp.float32),
        ],
        compiler_params=pltpu.CompilerParams(
            dimension_semantics=("arbitrary", "arbitrary", "arbitrary"),
            vmem_limit_bytes=VMEM_LIMIT),
        name="attention",
    )(qt, k, vt)


def _silu(x):
    return x * jax.nn.sigmoid(x)


def _gelu(x):
    return 0.5 * x * (1.0 + lax.erf(x * (2.0 ** -0.5)))


def _layer_norm(x, g, b):
    mu = jnp.mean(x, axis=-1, keepdims=True)
    xc = x - mu
    var = jnp.mean(xc * xc, axis=-1, keepdims=True)
    return xc * lax.rsqrt(var + EPS) * g + b


def _glu(a):
    return a[:, :CONV_WIDTH] * jax.nn.sigmoid(a[:, CONV_WIDTH:])


CONV_ROWS = 64


def _post_kernel(x_ref, rest_ref, prev_ref, next_ref, att_ref, wout_ref, gpost_ref,
                 dw_ref, dwb_ref, cg_ref, cb_ref, sg_ref, sb_ref, sw_ref, sbias_ref,
                 o_ref, hext_ref, cat_ref):
    si = pl.program_id(1)
    last = pl.num_programs(1) - 1

    cat_ref[:, 0:ATT_WIDTH] = _bf16(att_ref[...] * _silu(rest_ref[:, R_GATT:R_GATT + ATT_WIDTH]))

    hext_ref[0:HALO, :] = jnp.where(si > 0, _glu(prev_ref[...]), 0.0)
    hext_ref[HALO:HALO + TM, :] = _glu(rest_ref[:, R_ACONV:R_ACONV + 2 * CONV_WIDTH])
    hext_ref[HALO + TM:2 * HALO + TM, :] = jnp.where(si < last, _glu(next_ref[...]), 0.0)
    for rb in range(TM // CONV_ROWS):
        base = rb * CONV_ROWS + HALO - CONV_PAD
        acc = jnp.zeros((CONV_ROWS, CONV_WIDTH), jnp.float32) + dwb_ref[...]
        for j in range(CONV_KERNEL):
            acc = acc + hext_ref[base + j:base + j + CONV_ROWS, :] * dw_ref[j:j + 1, :]
        rows = slice(rb * CONV_ROWS, (rb + 1) * CONV_ROWS)
        c = _silu(_layer_norm(acc, cg_ref[...], cb_ref[...]))
        c = c * _silu(rest_ref[rows, R_GCONV:R_GCONV + CONV_WIDTH])
        cat_ref[rows, ATT_WIDTH:ATT_WIDTH + CONV_WIDTH] = _bf16(c)

    lane = lax.broadcasted_iota(jnp.int32, (1, SG_WIDTH), 1)
    for c in range(TM // SG_CHUNK):
        rows = slice(c * SG_CHUNK, (c + 1) * SG_CHUNK)
        vln = _bf16(_layer_norm(_gelu(rest_ref[rows, R_V:R_V + SG_WIDTH]), sg_ref[...], sb_ref[...]))
        mixed = sbias_ref[...]
        for hd in range(SG_HEADS):
            mh = jnp.dot(sw_ref[hd], vln, preferred_element_type=jnp.float32)
            mixed = mixed + jnp.where(lane // HEAD_DIM == hd, mh, 0.0)
        sgu = _gelu(rest_ref[rows, R_U:R_U + SG_WIDTH]) * mixed
        sgu = sgu * _silu(rest_ref[rows, R_GSG:R_GSG + SG_WIDTH])
        cat_ref[rows, ATT_WIDTH + CONV_WIDTH:D_MIX] = _bf16(sgu)

    mix = jnp.dot(cat_ref[...], wout_ref[...], preferred_element_type=jnp.float32)
    ms = jnp.mean(mix * mix, axis=-1, keepdims=True)
    o_ref[...] = x_ref[...] + mix * lax.rsqrt(ms + EPS) * gpost_ref[...]


def _post(x, rest, att, wout, gpost, dw, dwb, cg, cb, sg, sb, sw, sbias):
    B, S, _ = x.shape
    ns = S // TM
    hb = TM // HALO
    const2 = lambda b, s: (0, 0)
    const3 = lambda b, s: (0, 0, 0)
    aconv_blk = R_ACONV // (2 * CONV_WIDTH)
    return pl.pallas_call(
        _post_kernel,
        grid=(B, ns),
        in_specs=[
            pl.BlockSpec((None, TM, D_MODEL), lambda b, s: (b, s, 0)),
            pl.BlockSpec((None, TM, REST_WIDTH), lambda b, s: (b, s, 0)),
            pl.BlockSpec((None, HALO, 2 * CONV_WIDTH),
                         lambda b, s: (b, jnp.maximum(s * hb - 1, 0), aconv_blk)),
            pl.BlockSpec((None, HALO, 2 * CONV_WIDTH),
                         lambda b, s: (b, jnp.minimum((s + 1) * hb, S // HALO - 1), aconv_blk)),
            pl.BlockSpec((None, TM, ATT_WIDTH), lambda b, s: (b, s, 0)),
            pl.BlockSpec((D_MIX, D_MODEL), const2),
            pl.BlockSpec((1, D_MODEL), const2),
            pl.BlockSpec((CONV_KERNEL, CONV_WIDTH), const2),
            pl.BlockSpec((1, CONV_WIDTH), const2),
            pl.BlockSpec((1, CONV_WIDTH), const2),
            pl.BlockSpec((1, CONV_WIDTH), const2),
            pl.BlockSpec((1, SG_WIDTH), const2),
            pl.BlockSpec((1, SG_WIDTH), const2),
            pl.BlockSpec((SG_HEADS, SG_CHUNK, SG_CHUNK), const3),
            pl.BlockSpec((SG_CHUNK, SG_WIDTH), const2),
        ],
        out_specs=pl.BlockSpec((None, TM, D_MODEL), lambda b, s: (b, s, 0)),
        out_shape=jax.ShapeDtypeStruct((B, S, D_MODEL), jnp.float32),
        scratch_shapes=[
            pltpu.VMEM((TM + 2 * HALO, CONV_WIDTH), jnp.float32),
            pltpu.VMEM((TM, D_MIX), jnp.bfloat16),
        ],
        compiler_params=pltpu.CompilerParams(
            dimension_semantics=("arbitrary", "arbitrary"), vmem_limit_bytes=VMEM_LIMIT),
        name="post",
    )(x, rest, rest, rest, att, wout, gpost, dw, dwb, cg, cb, sg, sb, sw, sbias)


def _rope_tables(S):
    half = ROPE_AXIS_DIM // 2
    t = jnp.arange(S, dtype=jnp.int32)
    row = (t // GRID_W).astype(jnp.float32)
    col = (t % GRID_W).astype(jnp.float32)
    inv_freq = ROPE_THETA ** (-jnp.arange(half, dtype=jnp.float32) / half)
    ang_r = row[:, None] * inv_freq[None, :]
    ang_c = col[:, None] * inv_freq[None, :]
    cos_h = jnp.concatenate([jnp.cos(ang_r)] * 2 + [jnp.cos(ang_c)] * 2, axis=-1)
    sin_h = jnp.concatenate([-jnp.sin(ang_r), jnp.sin(ang_r), -jnp.sin(ang_c), jnp.sin(ang_c)], axis=-1)
    reps = LANES // HEAD_DIM
    return jnp.tile(cos_h, (1, reps)), jnp.tile(sin_h, (1, reps))


def kernel(x, pre_norm, post_norm, w_in, w_out, q_norm, k_norm, conv_dw, conv_dw_b,
           conv_ln_g, conv_ln_b, sg_ln_g, sg_ln_b, sg_w, sg_b):
    B, S, _ = x.shape
    depth = w_in.shape[0]
    cos_t, sin_t = _rope_tables(S)
    head_id = jnp.arange(ATT_WIDTH, dtype=jnp.int32) // HEAD_DIM
    bd = _bf16(jnp.where(head_id[:, None] == head_id[None, :], 1.0 / HEAD_DIM, 0.0))
    row2 = lambda a: a.reshape(1, -1)
    for l in range(depth):
        qt, k, vt, rest = _in_proj(
            x, row2(pre_norm[l]), _bf16(w_in[l]),
            row2(jnp.tile(q_norm[l], ATT_HEADS)), row2(jnp.tile(k_norm[l], ATT_KV_HEADS)),
            bd, cos_t, sin_t)
        att = _attention(qt, k, vt)
        sbias = jnp.repeat(sg_b[l].T, HEAD_DIM, axis=1)
        x = _post(x, rest, att, _bf16(w_out[l]), row2(post_norm[l]),
                  conv_dw[l], row2(conv_dw_b[l]), row2(conv_ln_g[l]), row2(conv_ln_b[l]),
                  row2(sg_ln_g[l]), row2(sg_ln_b[l]), _bf16(sg_w[l]), sbias)
    return x
```

```python
import math

import jax
import jax.numpy as jnp
from jax import lax
from jax.experimental import pallas as pl
from jax.experimental.pallas import tpu as pltpu

D_MODEL = 1024
HEAD_DIM = 64
GRID_W = 64
EPS = 1e-6
ATT_HEADS = 8
ATT_KV_HEADS = 2
GQA = ATT_HEADS // ATT_KV_HEADS
ATT_WIDTH = ATT_HEADS * HEAD_DIM
KV_WIDTH = ATT_KV_HEADS * HEAD_DIM
ROPE_THETA = 10000.0
ROPE_AXIS_DIM = HEAD_DIM // 2
CONV_WIDTH = 256
CONV_KERNEL = 31
CONV_PAD = CONV_KERNEL // 2
SG_HEADS = 4
SG_WIDTH = SG_HEADS * HEAD_DIM
SG_CHUNK = 128
D_MIX = ATT_WIDTH + CONV_WIDTH + SG_WIDTH
QKV_WIDTH = ATT_WIDTH + 2 * KV_WIDTH
REST_WIDTH = ATT_WIDTH + 3 * CONV_WIDTH + 3 * SG_WIDTH
D_IN = QKV_WIDTH + REST_WIDTH

R_GATT = 0
R_ACONV = ATT_WIDTH
R_GCONV = R_ACONV + 2 * CONV_WIDTH
R_U = R_GCONV + CONV_WIDTH
R_V = R_U + SG_WIDTH
R_GSG = R_V + SG_WIDTH

LANES = 128
HALO = 16
TM = 512
TQ = 512
TK = 256
ATT_UNROLL = 4
NQ = GQA * TQ
LANE_GROUP = 256
V_ROWS = HEAD_DIM + 16
SCORE_SCALE = HEAD_DIM ** -0.5 * math.log2(math.e)
VMEM_LIMIT = 56 * 1024 * 1024


def _bf16(x):
    return x.astype(jnp.bfloat16)


def _group_mean_sq(x, bd):
    x2 = x * x
    hi = _bf16(x2)
    lo = _bf16(x2 - hi.astype(jnp.float32))
    return (jnp.dot(hi, bd, preferred_element_type=jnp.float32)
            + jnp.dot(lo, bd, preferred_element_type=jnp.float32))


def _rope_slab(xn, cos, sin_signed, first_half):
    fwd = pltpu.roll(xn, LANES - ROPE_AXIS_DIM // 2, 1)
    bwd = pltpu.roll(xn, ROPE_AXIS_DIM // 2, 1)
    partner = jnp.where(first_half, fwd, bwd)
    return xn * cos + partner * sin_signed


def _in_proj_kernel(x_ref, g_ref, w_ref, gq_ref, gk_ref, bd_ref, cos_ref, sin_ref,
                    qt_ref, k_ref, vt_ref, rest_ref):
    x = x_ref[...]
    ms = jnp.mean(x * x, axis=-1, keepdims=True)
    h = _bf16(x * lax.rsqrt(ms + EPS) * g_ref[...])

    cos = cos_ref[...]
    sin_signed = sin_ref[...]
    lane = lax.broadcasted_iota(jnp.int32, (1, LANES), 1)
    first_half = (lane % ROPE_AXIS_DIM) < (ROPE_AXIS_DIM // 2)

    q = jnp.dot(h, w_ref[:, 0:ATT_WIDTH], preferred_element_type=jnp.float32)
    qn = q * lax.rsqrt(_group_mean_sq(q, bd_ref[...]) + EPS) * gq_ref[...]
    for j in range(ATT_WIDTH // LANES):
        sl = slice(j * LANES, (j + 1) * LANES)
        r = _rope_slab(qn[:, sl], cos, sin_signed, first_half) * SCORE_SCALE
        qt_ref[sl, :] = _bf16(r.T)

    kk = jnp.dot(h, w_ref[:, ATT_WIDTH:ATT_WIDTH + KV_WIDTH], preferred_element_type=jnp.float32)
    kn = kk * lax.rsqrt(_group_mean_sq(kk, bd_ref[0:KV_WIDTH, 0:KV_WIDTH]) + EPS) * gk_ref[...]
    k_ref[...] = _bf16(_rope_slab(kn, cos, sin_signed, first_half))

    v = jnp.dot(h, w_ref[:, ATT_WIDTH + KV_WIDTH:QKV_WIDTH], preferred_element_type=jnp.float32)
    vt = _bf16(v.T)
    for c in range(TM // TK):
        for hd in range(ATT_KV_HEADS):
            vt_ref[c, hd, 0:HEAD_DIM, :] = vt[hd * HEAD_DIM:(hd + 1) * HEAD_DIM, c * TK:(c + 1) * TK]
            vt_ref[c, hd, HEAD_DIM:V_ROWS, :] = jnp.ones((V_ROWS - HEAD_DIM, TK), jnp.bfloat16)

    for c in range(REST_WIDTH // 512):
        rest_ref[:, c * 512:(c + 1) * 512] = jnp.dot(
            h, w_ref[:, QKV_WIDTH + c * 512:QKV_WIDTH + (c + 1) * 512],
            preferred_element_type=jnp.float32)


def _in_proj(x, g, w, gq, gk, bd, cos_t, sin_t):
    B, S, _ = x.shape
    ns = S // TM
    const = lambda b, s: (0, 0)
    return pl.pallas_call(
        _in_proj_kernel,
        grid=(B, ns),
        in_specs=[
            pl.BlockSpec((None, TM, D_MODEL), lambda b, s: (b, s, 0)),
            pl.BlockSpec((1, D_MODEL), const),
            pl.BlockSpec((D_MODEL, D_IN), const),
            pl.BlockSpec((1, ATT_WIDTH), const),
            pl.BlockSpec((1, KV_WIDTH), const),
            pl.BlockSpec((ATT_WIDTH, ATT_WIDTH), const),
            pl.BlockSpec((TM, LANES), lambda b, s: (s, 0)),
            pl.BlockSpec((TM, LANES), lambda b, s: (s, 0)),
        ],
        out_specs=[
            pl.BlockSpec((None, ATT_WIDTH, TM), lambda b, s: (b, 0, s)),
            pl.BlockSpec((None, TM, KV_WIDTH), lambda b, s: (b, s, 0)),
            pl.BlockSpec((None, TM // TK, ATT_KV_HEADS, V_ROWS, TK), lambda b, s: (b, s, 0, 0, 0)),
            pl.BlockSpec((None, TM, REST_WIDTH), lambda b, s: (b, s, 0)),
        ],
        out_shape=[
            jax.ShapeDtypeStruct((B, ATT_WIDTH, S), jnp.bfloat16),
            jax.ShapeDtypeStruct((B, S, KV_WIDTH), jnp.bfloat16),
            jax.ShapeDtypeStruct((B, S // TK, ATT_KV_HEADS, V_ROWS, TK), jnp.bfloat16),
            jax.ShapeDtypeStruct((B, S, REST_WIDTH), jnp.float32),
        ],
        compiler_params=pltpu.CompilerParams(
            dimension_semantics=("arbitrary", "arbitrary"), vmem_limit_bytes=VMEM_LIMIT),
        name="in_proj",
    )(x, g, w, gq, gk, bd, cos_t, sin_t)


def _attention_kernel(qt_ref, k_ref, vt_ref, o_ref, qext_ref, s_ref, p_ref, acc_ref, m_ref, mx_ref):
    kv = pl.program_id(1)
    n = k_ref.shape[0] // TK
    for g in range(GQA):
        qg = qt_ref[g * HEAD_DIM:(g + 1) * HEAD_DIM, :]
        zero = jnp.zeros_like(qg)
        qext_ref[0:HEAD_DIM, g * TQ:(g + 1) * TQ] = jnp.where(kv == 0, qg, zero)
        qext_ref[HEAD_DIM:2 * HEAD_DIM, g * TQ:(g + 1) * TQ] = jnp.where(kv == 1, qg, zero)

    def scores(j, g, cols):
        kblk = k_ref[pl.ds(pl.multiple_of(j * TK, TK), TK), :]
        s = jnp.dot(kblk, qext_ref[:, cols], preferred_element_type=jnp.float32)
        s_ref[g] = s
        mx_ref[:, cols] = jnp.max(s, axis=0, keepdims=True)

    def values(j, g):
        return jnp.dot(vt_ref[j], p_ref[g], preferred_element_type=jnp.float32)

    groups = [(g, slice(g * LANE_GROUP, (g + 1) * LANE_GROUP)) for g in range(NQ // LANE_GROUP)]
    acc_ref[...] = jnp.zeros_like(acc_ref)
    p_ref[...] = jnp.zeros_like(p_ref)
    m_ref[...] = jnp.full_like(m_ref, -jnp.inf)
    for g, cols in groups:
        scores(0, g, cols)

    def step(j, carry):
        for g, cols in groups:
            pv = values(jnp.maximum(j - 1, 0), g)
            m_old = m_ref[:, cols]
            m_new = jnp.maximum(m_old, mx_ref[:, cols])
            m_ref[:, cols] = m_new
            p_ref[g] = _bf16(jnp.exp2(s_ref[g] - m_new))
            acc_ref[:, cols] = jnp.exp2(m_old - m_new) * (acc_ref[:, cols] + pv)
            scores(jnp.minimum(j + 1, n - 1), g, cols)
        return carry

    lax.fori_loop(0, n, step, 0, unroll=ATT_UNROLL)
    for g, cols in groups:
        acc_ref[:, cols] = acc_ref[:, cols] + values(n - 1, g)
    o = acc_ref[0:HEAD_DIM, :] / acc_ref[HEAD_DIM:HEAD_DIM + 1, :]
    o4 = jnp.concatenate([o[:, g * TQ:(g + 1) * TQ] for g in range(GQA)], axis=0)
    o_ref[...] = o4.T


def _attention(qt, k, vt):
    B, _, S = qt.shape
    nk = vt.shape[1]
    assert vt.shape[-1] == TK
    return pl.pallas_call(
        _attention_kernel,
        grid=(B, ATT_KV_HEADS, S // TQ),
        in_specs=[
            pl.BlockSpec((None, GQA * HEAD_DIM, TQ), lambda b, kv, qi: (b, kv, qi)),
            pl.BlockSpec((None, S, KV_WIDTH), lambda b, kv, qi: (b, 0, 0)),
            pl.BlockSpec((None, nk, None, V_ROWS, TK), lambda b, kv, qi: (b, 0, kv, 0, 0)),
        ],
        out_specs=pl.BlockSpec((None, TQ, GQA * HEAD_DIM), lambda b, kv, qi: (b, qi, kv)),
        out_shape=jax.ShapeDtypeStruct((B, S, ATT_WIDTH), jnp.float32),
        scratch_shapes=[
            pltpu.VMEM((KV_WIDTH, NQ), jnp.bfloat16),
            pltpu.VMEM((NQ // LANE_GROUP, TK, LANE_GROUP), jnp.float32),
            pltpu.VMEM((NQ // LANE_GROUP, TK, LANE_GROUP), jnp.bfloat16),
            pltpu.VMEM((V_ROWS, NQ), jnp.float32),
            pltpu.VMEM((1, NQ), jnp.float32),
            pltpu.VMEM((1, NQ), jnp.float32),
        ],
        compiler_params=pltpu.CompilerParams(
            dimension_semantics=("arbitrary", "arbitrary", "arbitrary"),
            vmem_limit_bytes=VMEM_LIMIT),
        name="attention",
    )(qt, k, vt)


def _silu(x):
    return x * jax.nn.sigmoid(x)


def _gelu(x):
    return 0.5 * x * (1.0 + lax.erf(x * (2.0 ** -0.5)))


def _layer_norm(x, g, b):
    mu = jnp.mean(x, axis=-1, keepdims=True)
    xc = x - mu
    var = jnp.mean(xc * xc, axis=-1, keepdims=True)
    return xc * lax.rsqrt(var + EPS) * g + b


def _glu(a):
    return a[:, :CONV_WIDTH] * jax.nn.sigmoid(a[:, CONV_WIDTH:])


CONV_ROWS = 64


def _post_kernel(x_ref, rest_ref, prev_ref, next_ref, att_ref, wout_ref, gpost_ref,
                 dw_ref, dwb_ref, cg_ref, cb_ref, sg_ref, sb_ref, sw_ref, sbias_ref,
                 o_ref, hext_ref, cat_ref):
    si = pl.program_id(1)
    last = pl.num_programs(1) - 1

    cat_ref[:, 0:ATT_WIDTH] = _bf16(att_ref[...] * _silu(rest_ref[:, R_GATT:R_GATT + ATT_WIDTH]))

    hext_ref[0:HALO, :] = jnp.where(si > 0, _glu(prev_ref[...]), 0.0)
    hext_ref[HALO:HALO + TM, :] = _glu(rest_ref[:, R_ACONV:R_ACONV + 2 * CONV_WIDTH])
    hext_ref[HALO + TM:2 * HALO + TM, :] = jnp.where(si < last, _glu(next_ref[...]), 0.0)
    for rb in range(TM // CONV_ROWS):
        base = rb * CONV_ROWS + HALO - CONV_PAD
        acc = jnp.zeros((CONV_ROWS, CONV_WIDTH), jnp.float32) + dwb_ref[...]
        for j in range(CONV_KERNEL):
            acc = acc + hext_ref[base + j:base + j + CONV_ROWS, :] * dw_ref[j:j + 1, :]
        rows = slice(rb * CONV_ROWS, (rb + 1) * CONV_ROWS)
        c = _silu(_layer_norm(acc, cg_ref[...], cb_ref[...]))
        c = c * _silu(rest_ref[rows, R_GCONV:R_GCONV + CONV_WIDTH])
        cat_ref[rows, ATT_WIDTH:ATT_WIDTH + CONV_WIDTH] = _bf16(c)

    lane = lax.broadcasted_iota(jnp.int32, (1, SG_WIDTH), 1)
    for c in range(TM // SG_CHUNK):
        rows = slice(c * SG_CHUNK, (c + 1) * SG_CHUNK)
        vln = _bf16(_layer_norm(_gelu(rest_ref[rows, R_V:R_V + SG_WIDTH]), sg_ref[...], sb_ref[...]))
        mixed = sbias_ref[...]
        for hd in range(SG_HEADS):
            mh = jnp.dot(sw_ref[hd], vln, preferred_element_type=jnp.float32)
            mixed = mixed + jnp.where(lane // HEAD_DIM == hd, mh, 0.0)
        sgu = _gelu(rest_ref[rows, R_U:R_U + SG_WIDTH]) * mixed
        sgu = sgu * _silu(rest_ref[rows, R_GSG:R_GSG + SG_WIDTH])
        cat_ref[rows, ATT_WIDTH + CONV_WIDTH:D_MIX] = _bf16(sgu)

    mix = jnp.dot(cat_ref[...], wout_ref[...], preferred_element_type=jnp.float32)
    ms = jnp.mean(mix * mix, axis=-1, keepdims=True)
    o_ref[...] = x_ref[...] + mix * lax.rsqrt(ms + EPS) * gpost_ref[...]


def _post(x, rest, att, wout, gpost, dw, dwb, cg, cb, sg, sb, sw, sbias):
    B, S, _ = x.shape
    ns = S // TM
    hb = TM // HALO
    const2 = lambda b, s: (0, 0)
    const3 = lambda b, s: (0, 0, 0)
    aconv_blk = R_ACONV // (2 * CONV_WIDTH)
    return pl.pallas_call(
        _post_kernel,
        grid=(B, ns),
        in_specs=[
            pl.BlockSpec((None, TM, D_MODEL), lambda b, s: (b, s, 0)),
            pl.BlockSpec((None, TM, REST_WIDTH), lambda b, s: (b, s, 0)),
            pl.BlockSpec((None, HALO, 2 * CONV_WIDTH),
                         lambda b, s: (b, jnp.maximum(s * hb - 1, 0), aconv_blk)),
            pl.BlockSpec((None, HALO, 2 * CONV_WIDTH),
                         lambda b, s: (b, jnp.minimum((s + 1) * hb, S // HALO - 1), aconv_blk)),
            pl.BlockSpec((None, TM, ATT_WIDTH), lambda b, s: (b, s, 0)),
            pl.BlockSpec((D_MIX, D_MODEL), const2),
            pl.BlockSpec((1, D_MODEL), const2),
            pl.BlockSpec((CONV_KERNEL, CONV_WIDTH), const2),
            pl.BlockSpec((1, CONV_WIDTH), const2),
            pl.BlockSpec((1, CONV_WIDTH), const2),
            pl.BlockSpec((1, CONV_WIDTH), const2),
            pl.BlockSpec((1, SG_WIDTH), const2),
            pl.BlockSpec((1, SG_WIDTH), const2),
            pl.BlockSpec((SG_HEADS, SG_CHUNK, SG_CHUNK), const3),
            pl.BlockSpec((SG_CHUNK, SG_WIDTH), const2),
        ],
        out_specs=pl.BlockSpec((None, TM, D_MODEL), lambda b, s: (b, s, 0)),
        out_shape=jax.ShapeDtypeStruct((B, S, D_MODEL), jnp.float32),
        scratch_shapes=[
            pltpu.VMEM((TM + 2 * HALO, CONV_WIDTH), jnp.float32),
            pltpu.VMEM((TM, D_MIX), jnp.bfloat16),
        ],
        compiler_params=pltpu.CompilerParams(
            dimension_semantics=("arbitrary", "arbitrary"), vmem_limit_bytes=VMEM_LIMIT),
        name="post",
    )(x, rest, rest, rest, att, wout, gpost, dw, dwb, cg, cb, sg, sb, sw, sbias)


def _rope_tables(S):
    half = ROPE_AXIS_DIM // 2
    t = jnp.arange(S, dtype=jnp.int32)
    row = (t // GRID_W).astype(jnp.float32)
    col = (t % GRID_W).astype(jnp.float32)
    inv_freq = ROPE_THETA ** (-jnp.arange(half, dtype=jnp.float32) / half)
    ang_r = row[:, None] * inv_freq[None, :]
    ang_c = col[:, None] * inv_freq[None, :]
    cos_h = jnp.concatenate([jnp.cos(ang_r)] * 2 + [jnp.cos(ang_c)] * 2, axis=-1)
    sin_h = jnp.concatenate([-jnp.sin(ang_r), jnp.sin(ang_r), -jnp.sin(ang_c), jnp.sin(ang_c)], axis=-1)
    reps = LANES // HEAD_DIM
    return jnp.tile(cos_h, (1, reps)), jnp.tile(sin_h, (1, reps))


def kernel(x, pre_norm, post_norm, w_in, w_out, q_norm, k_norm, conv_dw, conv_dw_b,
           conv_ln_g, conv_ln_b, sg_ln_g, sg_ln_b, sg_w, sg_b):
    B, S, _ = x.shape
    depth = w_in.shape[0]
    cos_t, sin_t = _rope_tables(S)
    head_id = jnp.arange(ATT_WIDTH, dtype=jnp.int32) // HEAD_DIM
    bd = _bf16(jnp.where(head_id[:, None] == head_id[None, :], 1.0 / HEAD_DIM, 0.0))
    row2 = lambda a: a.reshape(1, -1)
    for l in range(depth):
        qt, k, vt, rest = _in_proj(
            x, row2(pre_norm[l]), _bf16(w_in[l]),
            row2(jnp.tile(q_norm[l], ATT_HEADS)), row2(jnp.tile(k_norm[l], ATT_KV_HEADS)),
            bd, cos_t, sin_t)
        att = _attention(qt, k, vt)
        sbias = jnp.repeat(sg_b[l].T, HEAD_DIM, axis=1)
        x = _post(x, rest, att, _bf16(w_out[l]), row2(post_norm[l]),
                  conv_dw[l], row2(conv_dw_b[l]), row2(conv_ln_g[l]), row2(conv_ln_b[l]),
                  row2(sg_ln_g[l]), row2(sg_ln_b[l]), _bf16(sg_w[l]), sbias)
    return x
```

```python
import math

import jax
import jax.numpy as jnp
from jax import lax
from jax.experimental import pallas as pl
from jax.experimental.pallas import tpu as pltpu

D_MODEL = 1024
HEAD_DIM = 64
GRID_W = 64
EPS = 1e-6
ATT_HEADS = 8
ATT_KV_HEADS = 2
GQA = ATT_HEADS // ATT_KV_HEADS
ATT_WIDTH = ATT_HEADS * HEAD_DIM
KV_WIDTH = ATT_KV_HEADS * HEAD_DIM
ROPE_THETA = 10000.0
ROPE_AXIS_DIM = HEAD_DIM // 2
CONV_WIDTH = 256
CONV_KERNEL = 31
CONV_PAD = CONV_KERNEL // 2
SG_HEADS = 4
SG_WIDTH = SG_HEADS * HEAD_DIM
SG_CHUNK = 128
D_MIX = ATT_WIDTH + CONV_WIDTH + SG_WIDTH
QKV_WIDTH = ATT_WIDTH + 2 * KV_WIDTH
REST_WIDTH = ATT_WIDTH + 3 * CONV_WIDTH + 3 * SG_WIDTH
D_IN = QKV_WIDTH + REST_WIDTH

R_GATT = 0
R_ACONV = ATT_WIDTH
R_GCONV = R_ACONV + 2 * CONV_WIDTH
R_U = R_GCONV + CONV_WIDTH
R_V = R_U + SG_WIDTH
R_GSG = R_V + SG_WIDTH

LANES = 128
HALO = 16
TM = 512
TQ = 512
TK = 512
ATT_UNROLL = 4
NQ = GQA * TQ
LANE_GROUP = 256
V_ROWS = HEAD_DIM + 16
SCORE_SCALE = HEAD_DIM ** -0.5 * math.log2(math.e)
VMEM_LIMIT = 56 * 1024 * 1024


def _bf16(x):
    return x.astype(jnp.bfloat16)


def _group_mean_sq(x, bd):
    x2 = x * x
    hi = _bf16(x2)
    lo = _bf16(x2 - hi.astype(jnp.float32))
    return (jnp.dot(hi, bd, preferred_element_type=jnp.float32)
            + jnp.dot(lo, bd, preferred_element_type=jnp.float32))


def _rope_slab(xn, cos, sin_signed, first_half):
    fwd = pltpu.roll(xn, LANES - ROPE_AXIS_DIM // 2, 1)
    bwd = pltpu.roll(xn, ROPE_AXIS_DIM // 2, 1)
    partner = jnp.where(first_half, fwd, bwd)
    return xn * cos + partner * sin_signed


def _in_proj_kernel(x_ref, g_ref, w_ref, gq_ref, gk_ref, bd_ref, cos_ref, sin_ref,
                    qt_ref, k_ref, vt_ref, rest_ref):
    x = x_ref[...]
    ms = jnp.mean(x * x, axis=-1, keepdims=True)
    h = _bf16(x * lax.rsqrt(ms + EPS) * g_ref[...])

    cos = cos_ref[...]
    sin_signed = sin_ref[...]
    lane = lax.broadcasted_iota(jnp.int32, (1, LANES), 1)
    first_half = (lane % ROPE_AXIS_DIM) < (ROPE_AXIS_DIM // 2)

    q = jnp.dot(h, w_ref[:, 0:ATT_WIDTH], preferred_element_type=jnp.float32)
    qn = q * lax.rsqrt(_group_mean_sq(q, bd_ref[...]) + EPS) * gq_ref[...]
    for j in range(ATT_WIDTH // LANES):
        sl = slice(j * LANES, (j + 1) * LANES)
        r = _rope_slab(qn[:, sl], cos, sin_signed, first_half) * SCORE_SCALE
        qt_ref[sl, :] = _bf16(r.T)

    kk = jnp.dot(h, w_ref[:, ATT_WIDTH:ATT_WIDTH + KV_WIDTH], preferred_element_type=jnp.float32)
    kn = kk * lax.rsqrt(_group_mean_sq(kk, bd_ref[0:KV_WIDTH, 0:KV_WIDTH]) + EPS) * gk_ref[...]
    k_ref[...] = _bf16(_rope_slab(kn, cos, sin_signed, first_half))

    v = jnp.dot(h, w_ref[:, ATT_WIDTH + KV_WIDTH:QKV_WIDTH], preferred_element_type=jnp.float32)
    vt = _bf16(v.T)
    for c in range(TM // TK):
        for hd in range(ATT_KV_HEADS):
            vt_ref[c, hd, 0:HEAD_DIM, :] = vt[hd * HEAD_DIM:(hd + 1) * HEAD_DIM, c * TK:(c + 1) * TK]
            vt_ref[c, hd, HEAD_DIM:V_ROWS, :] = jnp.ones((V_ROWS - HEAD_DIM, TK), jnp.bfloat16)

    for c in range(REST_WIDTH // 512):
        rest_ref[:, c * 512:(c + 1) * 512] = jnp.dot(
            h, w_ref[:, QKV_WIDTH + c * 512:QKV_WIDTH + (c + 1) * 512],
            preferred_element_type=jnp.float32)


def _in_proj(x, g, w, gq, gk, bd, cos_t, sin_t):
    B, S, _ = x.shape
    ns = S // TM
    const = lambda b, s: (0, 0)
    return pl.pallas_call(
        _in_proj_kernel,
        grid=(B, ns),
        in_specs=[
            pl.BlockSpec((None, TM, D_MODEL), lambda b, s: (b, s, 0)),
            pl.BlockSpec((1, D_MODEL), const),
            pl.BlockSpec((D_MODEL, D_IN), const),
            pl.BlockSpec((1, ATT_WIDTH), const),
            pl.BlockSpec((1, KV_WIDTH), const),
            pl.BlockSpec((ATT_WIDTH, ATT_WIDTH), const),
            pl.BlockSpec((TM, LANES), lambda b, s: (s, 0)),
            pl.BlockSpec((TM, LANES), lambda b, s: (s, 0)),
        ],
        out_specs=[
            pl.BlockSpec((None, ATT_WIDTH, TM), lambda b, s: (b, 0, s)),
            pl.BlockSpec((None, TM, KV_WIDTH), lambda b, s: (b, s, 0)),
            pl.BlockSpec((None, TM // TK, ATT_KV_HEADS, V_ROWS, TK), lambda b, s: (b, s, 0, 0, 0)),
            pl.BlockSpec((None, TM, REST_WIDTH), lambda b, s: (b, s, 0)),
        ],
        out_shape=[
            jax.ShapeDtypeStruct((B, ATT_WIDTH, S), jnp.bfloat16),
            jax.ShapeDtypeStruct((B, S, KV_WIDTH), jnp.bfloat16),
            jax.ShapeDtypeStruct((B, S // TK, ATT_KV_HEADS, V_ROWS, TK), jnp.bfloat16),
            jax.ShapeDtypeStruct((B, S, REST_WIDTH), jnp.float32),
        ],
        compiler_params=pltpu.CompilerParams(
            dimension_semantics=("arbitrary", "arbitrary"), vmem_limit_bytes=VMEM_LIMIT),
        name="in_proj",
    )(x, g, w, gq, gk, bd, cos_t, sin_t)


def _attention_kernel(qt_ref, k_ref, vt_ref, o_ref, qext_ref, s_ref, p_ref, acc_ref, m_ref, mx_ref):
    kv = pl.program_id(1)
    n = k_ref.shape[0] // TK
    for g in range(GQA):
        qg = qt_ref[g * HEAD_DIM:(g + 1) * HEAD_DIM, :]
        zero = jnp.zeros_like(qg)
        qext_ref[0:HEAD_DIM, g * TQ:(g + 1) * TQ] = jnp.where(kv == 0, qg, zero)
        qext_ref[HEAD_DIM:2 * HEAD_DIM, g * TQ:(g + 1) * TQ] = jnp.where(kv == 1, qg, zero)

    def scores(j, g, cols):
        kblk = k_ref[pl.ds(pl.multiple_of(j * TK, TK), TK), :]
        s = jnp.dot(kblk, qext_ref[:, cols], preferred_element_type=jnp.float32)
        s_ref[g] = s
        mx_ref[:, cols] = jnp.max(s, axis=0, keepdims=True)

    def values(j, g):
        return jnp.dot(vt_ref[j], p_ref[g], preferred_element_type=jnp.float32)

    groups = [(g, slice(g * LANE_GROUP, (g + 1) * LANE_GROUP)) for g in range(NQ // LANE_GROUP)]
    acc_ref[...] = jnp.zeros_like(acc_ref)
    p_ref[...] = jnp.zeros_like(p_ref)
    m_ref[...] = jnp.full_like(m_ref, -jnp.inf)
    for g, cols in groups:
        scores(0, g, cols)

    def step(j, carry):
        for g, cols in groups:
            pv = values(jnp.maximum(j - 1, 0), g)
            m_old = m_ref[:, cols]
            m_new = jnp.maximum(m_old, mx_ref[:, cols])
            m_ref[:, cols] = m_new
            p_ref[g] = _bf16(jnp.exp2(s_ref[g] - m_new))
            acc_ref[:, cols] = jnp.exp2(m_old - m_new) * (acc_ref[:, cols] + pv)
            scores(jnp.minimum(j + 1, n - 1), g, cols)
        return carry

    lax.fori_loop(0, n, step, 0, unroll=ATT_UNROLL)
    for g, cols in groups:
        acc_ref[:, cols] = acc_ref[:, cols] + values(n - 1, g)
    o = acc_ref[0:HEAD_DIM, :] / acc_ref[HEAD_DIM:HEAD_DIM + 1, :]
    o4 = jnp.concatenate([o[:, g * TQ:(g + 1) * TQ] for g in range(GQA)], axis=0)
    o_ref[...] = o4.T


def _attention(qt, k, vt):
    B, _, S = qt.shape
    nk = vt.shape[1]
    assert vt.shape[-1] == TK
    return pl.pallas_call(
        _attention_kernel,
        grid=(B, ATT_KV_HEADS, S // TQ),
        in_specs=[
            pl.BlockSpec((None, GQA * HEAD_DIM, TQ), lambda b, kv, qi: (b, kv, qi)),
            pl.BlockSpec((None, S, KV_WIDTH), lambda b, kv, qi: (b, 0, 0)),
            pl.BlockSpec((None, nk, None, V_ROWS, TK), lambda b, kv, qi: (b, 0, kv, 0, 0)),
        ],
        out_specs=pl.BlockSpec((None, TQ, GQA * HEAD_DIM), lambda b, kv, qi: (b, qi, kv)),
        out_shape=jax.ShapeDtypeStruct((B, S, ATT_WIDTH), jnp.float32),
        scratch_shapes=[
            pltpu.VMEM((KV_WIDTH, NQ), jnp.bfloat16),
            pltpu.VMEM((NQ // LANE_GROUP, TK, LANE_GROUP), jnp.float32),
            pltpu.VMEM((NQ // LANE_GROUP, TK, LANE_GROUP), jnp.bfloat16),
            pltpu.VMEM((V_ROWS, NQ), jnp.float32),
            pltpu.VMEM((1, NQ), jnp.float32),
            pltpu.VMEM((1, NQ), jnp.float32),
        ],
        compiler_params=pltpu.CompilerParams(
            dimension_semantics=("arbitrary", "arbitrary", "arbitrary"),
            vmem_limit_bytes=VMEM_LIMIT),
        name="attention",
    )(qt, k, vt)


def _silu(x):
    return x * jax.nn.sigmoid(x)


def _gelu(x):
    return 0.5 * x * (1.0 + lax.erf(x * (2.0 ** -0.5)))


def _layer_norm(x, g, b):
    mu = jnp.mean(x, axis=-1, keepdims=True)
    xc = x - mu
    var = jnp.mean(xc * xc, axis=-1, keepdims=True)
    return xc * lax.rsqrt(var + EPS) * g + b


def _glu(a):
    return a[:, :CONV_WIDTH] * jax.nn.sigmoid(a[:, CONV_WIDTH:])


CONV_ROWS = 64


def _post_kernel(x_ref, rest_ref, prev_ref, next_ref, att_ref, wout_ref, gpost_ref,
                 dw_ref, dwb_ref, cg_ref, cb_ref, sg_ref, sb_ref, sw_ref, sbias_ref,
                 o_ref, hext_ref, cat_ref):
    si = pl.program_id(1)
    last = pl.num_programs(1) - 1

    cat_ref[:, 0:ATT_WIDTH] = _bf16(att_ref[...] * _silu(rest_ref[:, R_GATT:R_GATT + ATT_WIDTH]))

    hext_ref[0:HALO, :] = jnp.where(si > 0, _glu(prev_ref[...]), 0.0)
    hext_ref[HALO:HALO + TM, :] = _glu(rest_ref[:, R_ACONV:R_ACONV + 2 * CONV_WIDTH])
    hext_ref[HALO + TM:2 * HALO + TM, :] = jnp.where(si < last, _glu(next_ref[...]), 0.0)
    for rb in range(TM // CONV_ROWS):
        base = rb * CONV_ROWS + HALO - CONV_PAD
        acc = jnp.zeros((CONV_ROWS, CONV_WIDTH), jnp.float32) + dwb_ref[...]
        for j in range(CONV_KERNEL):
            acc = acc + hext_ref[base + j:base + j + CONV_ROWS, :] * dw_ref[j:j + 1, :]
        rows = slice(rb * CONV_ROWS, (rb + 1) * CONV_ROWS)
        c = _silu(_layer_norm(acc, cg_ref[...], cb_ref[...]))
        c = c * _silu(rest_ref[rows, R_GCONV:R_GCONV + CONV_WIDTH])
        cat_ref[rows, ATT_WIDTH:ATT_WIDTH + CONV_WIDTH] = _bf16(c)

    lane = lax.broadcasted_iota(jnp.int32, (1, SG_WIDTH), 1)
    for c in range(TM // SG_CHUNK):
        rows = slice(c * SG_CHUNK, (c + 1) * SG_CHUNK)
        vln = _bf16(_layer_norm(_gelu(rest_ref[rows, R_V:R_V + SG_WIDTH]), sg_ref[...], sb_ref[...]))
        mixed = sbias_ref[...]
        for hd in range(SG_HEADS):
            mh = jnp.dot(sw_ref[hd], vln, preferred_element_type=jnp.float32)
            mixed = mixed + jnp.where(lane // HEAD_DIM == hd, mh, 0.0)
        sgu = _gelu(rest_ref[rows, R_U:R_U + SG_WIDTH]) * mixed
        sgu = sgu * _silu(rest_ref[rows, R_GSG:R_GSG + SG_WIDTH])
        cat_ref[rows, ATT_WIDTH + CONV_WIDTH:D_MIX] = _bf16(sgu)

    mix = jnp.dot(cat_ref[...], wout_ref[...], preferred_element_type=jnp.float32)
    ms = jnp.mean(mix * mix, axis=-1, keepdims=True)
    o_ref[...] = x_ref[...] + mix * lax.rsqrt(ms + EPS) * gpost_ref[...]


def _post(x, rest, att, wout, gpost, dw, dwb, cg, cb, sg, sb, sw, sbias):
    B, S, _ = x.shape
    ns = S // TM
    hb = TM // HALO
    const2 = lambda b, s: (0, 0)
    const3 = lambda b, s: (0, 0, 0)
    aconv_blk = R_ACONV // (2 * CONV_WIDTH)
    return pl.pallas_call(
        _post_kernel,
        grid=(B, ns),
        in_specs=[
            pl.BlockSpec((None, TM, D_MODEL), lambda b, s: (b, s, 0)),
            pl.BlockSpec((None, TM, REST_WIDTH), lambda b, s: (b, s, 0)),
            pl.BlockSpec((None, HALO, 2 * CONV_WIDTH),
                         lambda b, s: (b, jnp.maximum(s * hb - 1, 0), aconv_blk)),
            pl.BlockSpec((None, HALO, 2 * CONV_WIDTH),
                         lambda b, s: (b, jnp.minimum((s + 1) * hb, S // HALO - 1), aconv_blk)),
            pl.BlockSpec((None, TM, ATT_WIDTH), lambda b, s: (b, s, 0)),
            pl.BlockSpec((D_MIX, D_MODEL), const2),
            pl.BlockSpec((1, D_MODEL), const2),
            pl.BlockSpec((CONV_KERNEL, CONV_WIDTH), const2),
            pl.BlockSpec((1, CONV_WIDTH), const2),
            pl.BlockSpec((1, CONV_WIDTH), const2),
            pl.BlockSpec((1, CONV_WIDTH), const2),
            pl.BlockSpec((1, SG_WIDTH), const2),
            pl.BlockSpec((1, SG_WIDTH), const2),
            pl.BlockSpec((SG_HEADS, SG_CHUNK, SG_CHUNK), const3),
            pl.BlockSpec((SG_CHUNK, SG_WIDTH), const2),
        ],
        out_specs=pl.BlockSpec((None, TM, D_MODEL), lambda b, s: (b, s, 0)),
        out_shape=jax.ShapeDtypeStruct((B, S, D_MODEL), jnp.float32),
        scratch_shapes=[
            pltpu.VMEM((TM + 2 * HALO, CONV_WIDTH), jnp.float32),
            pltpu.VMEM((TM, D_MIX), jnp.bfloat16),
        ],
        compiler_params=pltpu.CompilerParams(
            dimension_semantics=("arbitrary", "arbitrary"), vmem_limit_bytes=VMEM_LIMIT),
        name="post",
    )(x, rest, rest, rest, att, wout, gpost, dw, dwb, cg, cb, sg, sb, sw, sbias)


def _rope_tables(S):
    half = ROPE_AXIS_DIM // 2
    t = jnp.arange(S, dtype=jnp.int32)
    row = (t // GRID_W).astype(jnp.float32)
    col = (t % GRID_W).astype(jnp.float32)
    inv_freq = ROPE_THETA ** (-jnp.arange(half, dtype=jnp.float32) / half)
    ang_r = row[:, None] * inv_freq[None, :]
    ang_c = col[:, None] * inv_freq[None, :]
    cos_h = jnp.concatenate([jnp.cos(ang_r)] * 2 + [jnp.cos(ang_c)] * 2, axis=-1)
    sin_h = jnp.concatenate([-jnp.sin(ang_r), jnp.sin(ang_r), -jnp.sin(ang_c), jnp.sin(ang_c)], axis=-1)
    reps = LANES // HEAD_DIM
    return jnp.tile(cos_h, (1, reps)), jnp.tile(sin_h, (1, reps))


def kernel(x, pre_norm, post_norm, w_in, w_out, q_norm, k_norm, conv_dw, conv_dw_b,
           conv_ln_g, conv_ln_b, sg_ln_g, sg_ln_b, sg_w, sg_b):
    B, S, _ = x.shape
    depth = w_in.shape[0]
    cos_t, sin_t = _rope_tables(S)
    head_id = jnp.arange(ATT_WIDTH, dtype=jnp.int32) // HEAD_DIM
    bd = _bf16(jnp.where(head_id[:, None] == head_id[None, :], 1.0 / HEAD_DIM, 0.0))
    row2 = lambda a: a.reshape(1, -1)
    for l in range(depth):
        qt, k, vt, rest = _in_proj(
            x, row2(pre_norm[l]), _bf16(w_in[l]),
            row2(jnp.tile(q_norm[l], ATT_HEADS)), row2(jnp.tile(k_norm[l], ATT_KV_HEADS)),
            bd, cos_t, sin_t)
        att = _attention(qt, k, vt)
        sbias = jnp.repeat(sg_b[l].T, HEAD_DIM, axis=1)
        x = _post(x, rest, att, _bf16(w_out[l]), row2(post_norm[l]),
                  conv_dw[l], row2(conv_dw_b[l]), row2(conv_ln_g[l]), row2(conv_ln_b[l]),
                  row2(sg_ln_g[l]), row2(sg_ln_b[l]), _bf16(sg_w[l]), sbias)
    return x
```

```python
import math

import jax
import jax.numpy as jnp
from jax import lax
from jax.experimental import pallas as pl
from jax.experimental.pallas import tpu as pltpu

D_MODEL = 1024
HEAD_DIM = 64
GRID_W = 64
EPS = 1e-6
ATT_HEADS = 8
ATT_KV_HEADS = 2
GQA = ATT_HEADS // ATT_KV_HEADS
ATT_WIDTH = ATT_HEADS * HEAD_DIM
KV_WIDTH = ATT_KV_HEADS * HEAD_DIM
ROPE_THETA = 10000.0
ROPE_AXIS_DIM = HEAD_DIM // 2
CONV_WIDTH = 256
CONV_KERNEL = 31
CONV_PAD = CONV_KERNEL // 2
SG_HEADS = 4
SG_WIDTH = SG_HEADS * HEAD_DIM
SG_CHUNK = 128
D_MIX = ATT_WIDTH + CONV_WIDTH + SG_WIDTH
QKV_WIDTH = ATT_WIDTH + 2 * KV_WIDTH
REST_WIDTH = ATT_WIDTH + 3 * CONV_WIDTH + 3 * SG_WIDTH
D_IN = QKV_WIDTH + REST_WIDTH

R_GATT = 0
R_ACONV = ATT_WIDTH
R_GCONV = R_ACONV + 2 * CONV_WIDTH
R_U = R_GCONV + CONV_WIDTH
R_V = R_U + SG_WIDTH
R_GSG = R_V + SG_WIDTH

LANES = 128
HALO = 16
TM = 512
TQ = 512
TK = 512
ATT_UNROLL = 8
NQ = GQA * TQ
LANE_GROUP = 256
V_ROWS = HEAD_DIM + 16
SCORE_SCALE = HEAD_DIM ** -0.5 * math.log2(math.e)
VMEM_LIMIT = 56 * 1024 * 1024


def _bf16(x):
    return x.astype(jnp.bfloat16)


def _group_mean_sq(x, bd):
    x2 = x * x
    hi = _bf16(x2)
    lo = _bf16(x2 - hi.astype(jnp.float32))
    return (jnp.dot(hi, bd, preferred_element_type=jnp.float32)
            + jnp.dot(lo, bd, preferred_element_type=jnp.float32))


def _rope_slab(xn, cos, sin_signed, first_half):
    fwd = pltpu.roll(xn, LANES - ROPE_AXIS_DIM // 2, 1)
    bwd = pltpu.roll(xn, ROPE_AXIS_DIM // 2, 1)
    partner = jnp.where(first_half, fwd, bwd)
    return xn * cos + partner * sin_signed


def _in_proj_kernel(x_ref, g_ref, w_ref, gq_ref, gk_ref, bd_ref, cos_ref, sin_ref,
                    qt_ref, k_ref, vt_ref, rest_ref):
    x = x_ref[...]
    ms = jnp.mean(x * x, axis=-1, keepdims=True)
    h = _bf16(x * lax.rsqrt(ms + EPS) * g_ref[...])

    cos = cos_ref[...]
    sin_signed = sin_ref[...]
    lane = lax.broadcasted_iota(jnp.int32, (1, LANES), 1)
    first_half = (lane % ROPE_AXIS_DIM) < (ROPE_AXIS_DIM // 2)

    q = jnp.dot(h, w_ref[:, 0:ATT_WIDTH], preferred_element_type=jnp.float32)
    qn = q * lax.rsqrt(_group_mean_sq(q, bd_ref[...]) + EPS) * gq_ref[...]
    for j in range(ATT_WIDTH // LANES):
        sl = slice(j * LANES, (j + 1) * LANES)
        r = _rope_slab(qn[:, sl], cos, sin_signed, first_half) * SCORE_SCALE
        qt_ref[sl, :] = _bf16(r.T)

    kk = jnp.dot(h, w_ref[:, ATT_WIDTH:ATT_WIDTH + KV_WIDTH], preferred_element_type=jnp.float32)
    kn = kk * lax.rsqrt(_group_mean_sq(kk, bd_ref[0:KV_WIDTH, 0:KV_WIDTH]) + EPS) * gk_ref[...]
    k_ref[...] = _bf16(_rope_slab(kn, cos, sin_signed, first_half))

    v = jnp.dot(h, w_ref[:, ATT_WIDTH + KV_WIDTH:QKV_WIDTH], preferred_element_type=jnp.float32)
    vt = _bf16(v.T)
    for c in range(TM // TK):
        for hd in range(ATT_KV_HEADS):
            vt_ref[c, hd, 0:HEAD_DIM, :] = vt[hd * HEAD_DIM:(hd + 1) * HEAD_DIM, c * TK:(c + 1) * TK]
            vt_ref[c, hd, HEAD_DIM:V_ROWS, :] = jnp.ones((V_ROWS - HEAD_DIM, TK), jnp.bfloat16)

    for c in range(REST_WIDTH // 512):
        rest_ref[:, c * 512:(c + 1) * 512] = jnp.dot(
            h, w_ref[:, QKV_WIDTH + c * 512:QKV_WIDTH + (c + 1) * 512],
            preferred_element_type=jnp.float32)


def _in_proj(x, g, w, gq, gk, bd, cos_t, sin_t):
    B, S, _ = x.shape
    ns = S // TM
    const = lambda b, s: (0, 0)
    return pl.pallas_call(
        _in_proj_kernel,
        grid=(B, ns),
        in_specs=[
            pl.BlockSpec((None, TM, D_MODEL), lambda b, s: (b, s, 0)),
            pl.BlockSpec((1, D_MODEL), const),
            pl.BlockSpec((D_MODEL, D_IN), const),
            pl.BlockSpec((1, ATT_WIDTH), const),
            pl.BlockSpec((1, KV_WIDTH), const),
            pl.BlockSpec((ATT_WIDTH, ATT_WIDTH), const),
            pl.BlockSpec((TM, LANES), lambda b, s: (s, 0)),
            pl.BlockSpec((TM, LANES), lambda b, s: (s, 0)),
        ],
        out_specs=[
            pl.BlockSpec((None, ATT_WIDTH, TM), lambda b, s: (b, 0, s)),
            pl.BlockSpec((None, TM, KV_WIDTH), lambda b, s: (b, s, 0)),
            pl.BlockSpec((None, TM // TK, ATT_KV_HEADS, V_ROWS, TK), lambda b, s: (b, s, 0, 0, 0)),
            pl.BlockSpec((None, TM, REST_WIDTH), lambda b, s: (b, s, 0)),
        ],
        out_shape=[
            jax.ShapeDtypeStruct((B, ATT_WIDTH, S), jnp.bfloat16),
            jax.ShapeDtypeStruct((B, S, KV_WIDTH), jnp.bfloat16),
            jax.ShapeDtypeStruct((B, S // TK, ATT_KV_HEADS, V_ROWS, TK), jnp.bfloat16),
            jax.ShapeDtypeStruct((B, S, REST_WIDTH), jnp.float32),
        ],
        compiler_params=pltpu.CompilerParams(
            dimension_semantics=("arbitrary", "arbitrary"), vmem_limit_bytes=VMEM_LIMIT),
        name="in_proj",
    )(x, g, w, gq, gk, bd, cos_t, sin_t)


def _attention_kernel(qt_ref, k_ref, vt_ref, o_ref, qext_ref, s_ref, p_ref, acc_ref, m_ref, mx_ref):
    kv = pl.program_id(1)
    n = k_ref.shape[0] // TK
    for g in range(GQA):
        qg = qt_ref[g * HEAD_DIM:(g + 1) * HEAD_DIM, :]
        zero = jnp.zeros_like(qg)
        qext_ref[0:HEAD_DIM, g * TQ:(g + 1) * TQ] = jnp.where(kv == 0, qg, zero)
        qext_ref[HEAD_DIM:2 * HEAD_DIM, g * TQ:(g + 1) * TQ] = jnp.where(kv == 1, qg, zero)

    def scores(j, g, cols):
        kblk = k_ref[pl.ds(pl.multiple_of(j * TK, TK), TK), :]
        s = jnp.dot(kblk, qext_ref[:, cols], preferred_element_type=jnp.float32)
        s_ref[g] = s
        mx_ref[:, cols] = jnp.max(s, axis=0, keepdims=True)

    def values(j, g):
        return jnp.dot(vt_ref[j], p_ref[g], preferred_element_type=jnp.float32)

    groups = [(g, slice(g * LANE_GROUP, (g + 1) * LANE_GROUP)) for g in range(NQ // LANE_GROUP)]
    acc_ref[...] = jnp.zeros_like(acc_ref)
    p_ref[...] = jnp.zeros_like(p_ref)
    m_ref[...] = jnp.full_like(m_ref, -jnp.inf)
    for g, cols in groups:
        scores(0, g, cols)

    def step(j, carry):
        for g, cols in groups:
            pv = values(jnp.maximum(j - 1, 0), g)
            m_old = m_ref[:, cols]
            m_new = jnp.maximum(m_old, mx_ref[:, cols])
            m_ref[:, cols] = m_new
            p_ref[g] = _bf16(jnp.exp2(s_ref[g] - m_new))
            acc_ref[:, cols] = jnp.exp2(m_old - m_new) * (acc_ref[:, cols] + pv)
            scores(jnp.minimum(j + 1, n - 1), g, cols)
        return carry

    lax.fori_loop(0, n, step, 0, unroll=ATT_UNROLL)
    for g, cols in groups:
        acc_ref[:, cols] = acc_ref[:, cols] + values(n - 1, g)
    o = acc_ref[0:HEAD_DIM, :] / acc_ref[HEAD_DIM:HEAD_DIM + 1, :]
    o4 = jnp.concatenate([o[:, g * TQ:(g + 1) * TQ] for g in range(GQA)], axis=0)
    o_ref[...] = o4.T


def _attention(qt, k, vt):
    B, _, S = qt.shape
    nk = vt.shape[1]
    assert vt.shape[-1] == TK
    return pl.pallas_call(
        _attention_kernel,
        grid=(B, ATT_KV_HEADS, S // TQ),
        in_specs=[
            pl.BlockSpec((None, GQA * HEAD_DIM, TQ), lambda b, kv, qi: (b, kv, qi)),
            pl.BlockSpec((None, S, KV_WIDTH), lambda b, kv, qi: (b, 0, 0)),
            pl.BlockSpec((None, nk, None, V_ROWS, TK), lambda b, kv, qi: (b, 0, kv, 0, 0)),
        ],
        out_specs=pl.BlockSpec((None, TQ, GQA * HEAD_DIM), lambda b, kv, qi: (b, qi, kv)),
        out_shape=jax.ShapeDtypeStruct((B, S, ATT_WIDTH), jnp.float32),
        scratch_shapes=[
            pltpu.VMEM((KV_WIDTH, NQ), jnp.bfloat16),
            pltpu.VMEM((NQ // LANE_GROUP, TK, LANE_GROUP), jnp.float32),
            pltpu.VMEM((NQ // LANE_GROUP, TK, LANE_GROUP), jnp.bfloat16),
            pltpu.VMEM((V_ROWS, NQ), jnp.float32),
            pltpu.VMEM((1, NQ), jnp.float32),
            pltpu.VMEM((1, NQ), jnp.float32),
        ],
        compiler_params=pltpu.CompilerParams(
            dimension_semantics=("arbitrary", "arbitrary", "arbitrary"),
            vmem_limit_bytes=VMEM_LIMIT),
        name="attention",
    )(qt, k, vt)


def _silu(x):
    return x * jax.nn.sigmoid(x)


def _gelu(x):
    return 0.5 * x * (1.0 + lax.erf(x * (2.0 ** -0.5)))


def _layer_norm(x, g, b):
    mu = jnp.mean(x, axis=-1, keepdims=True)
    xc = x - mu
    var = jnp.mean(xc * xc, axis=-1, keepdims=True)
    return xc * lax.rsqrt(var + EPS) * g + b


def _glu(a):
    return a[:, :CONV_WIDTH] * jax.nn.sigmoid(a[:, CONV_WIDTH:])


CONV_ROWS = 64


def _post_kernel(x_ref, rest_ref, prev_ref, next_ref, att_ref, wout_ref, gpost_ref,
                 dw_ref, dwb_ref, cg_ref, cb_ref, sg_ref, sb_ref, sw_ref, sbias_ref,
                 o_ref, hext_ref, cat_ref):
    si = pl.program_id(1)
    last = pl.num_programs(1) - 1

    cat_ref[:, 0:ATT_WIDTH] = _bf16(att_ref[...] * _silu(rest_ref[:, R_GATT:R_GATT + ATT_WIDTH]))

    hext_ref[0:HALO, :] = jnp.where(si > 0, _glu(prev_ref[...]), 0.0)
    hext_ref[HALO:HALO + TM, :] = _glu(rest_ref[:, R_ACONV:R_ACONV + 2 * CONV_WIDTH])
    hext_ref[HALO + TM:2 * HALO + TM, :] = jnp.where(si < last, _glu(next_ref[...]), 0.0)
    for rb in range(TM // CONV_ROWS):
        base = rb * CONV_ROWS + HALO - CONV_PAD
        acc = jnp.zeros((CONV_ROWS, CONV_WIDTH), jnp.float32) + dwb_ref[...]
        for j in range(CONV_KERNEL):
            acc = acc + hext_ref[base + j:base + j + CONV_ROWS, :] * dw_ref[j:j + 1, :]
        rows = slice(rb * CONV_ROWS, (rb + 1) * CONV_ROWS)
        c = _silu(_layer_norm(acc, cg_ref[...], cb_ref[...]))
        c = c * _silu(rest_ref[rows, R_GCONV:R_GCONV + CONV_WIDTH])
        cat_ref[rows, ATT_WIDTH:ATT_WIDTH + CONV_WIDTH] = _bf16(c)

    lane = lax.broadcasted_iota(jnp.int32, (1, SG_WIDTH), 1)
    for c in range(TM // SG_CHUNK):
        rows = slice(c * SG_CHUNK, (c + 1) * SG_CHUNK)
        vln = _bf16(_layer_norm(_gelu(rest_ref[rows, R_V:R_V + SG_WIDTH]), sg_ref[...], sb_ref[...]))
        mixed = sbias_ref[...]
        for hd in range(SG_HEADS):
            mh = jnp.dot(sw_ref[hd], vln, preferred_element_type=jnp.float32)
            mixed = mixed + jnp.where(lane // HEAD_DIM == hd, mh, 0.0)
        sgu = _gelu(rest_ref[rows, R_U:R_U + SG_WIDTH]) * mixed
        sgu = sgu * _silu(rest_ref[rows, R_GSG:R_GSG + SG_WIDTH])
        cat_ref[rows, ATT_WIDTH + CONV_WIDTH:D_MIX] = _bf16(sgu)

    mix = jnp.dot(cat_ref[...], wout_ref[...], preferred_element_type=jnp.float32)
    ms = jnp.mean(mix * mix, axis=-1, keepdims=True)
    o_ref[...] = x_ref[...] + mix * lax.rsqrt(ms + EPS) * gpost_ref[...]


def _post(x, rest, att, wout, gpost, dw, dwb, cg, cb, sg, sb, sw, sbias):
    B, S, _ = x.shape
    ns = S // TM
    hb = TM // HALO
    const2 = lambda b, s: (0, 0)
    const3 = lambda b, s: (0, 0, 0)
    aconv_blk = R_ACONV // (2 * CONV_WIDTH)
    return pl.pallas_call(
        _post_kernel,
        grid=(B, ns),
        in_specs=[
            pl.BlockSpec((None, TM, D_MODEL), lambda b, s: (b, s, 0)),
            pl.BlockSpec((None, TM, REST_WIDTH), lambda b, s: (b, s, 0)),
            pl.BlockSpec((None, HALO, 2 * CONV_WIDTH),
                         lambda b, s: (b, jnp.maximum(s * hb - 1, 0), aconv_blk)),
            pl.BlockSpec((None, HALO, 2 * CONV_WIDTH),
                         lambda b, s: (b, jnp.minimum((s + 1) * hb, S // HALO - 1), aconv_blk)),
            pl.BlockSpec((None, TM, ATT_WIDTH), lambda b, s: (b, s, 0)),
            pl.BlockSpec((D_MIX, D_MODEL), const2),
            pl.BlockSpec((1, D_MODEL), const2),
            pl.BlockSpec((CONV_KERNEL, CONV_WIDTH), const2),
            pl.BlockSpec((1, CONV_WIDTH), const2),
            pl.BlockSpec((1, CONV_WIDTH), const2),
            pl.BlockSpec((1, CONV_WIDTH), const2),
            pl.BlockSpec((1, SG_WIDTH), const2),
            pl.BlockSpec((1, SG_WIDTH), const2),
            pl.BlockSpec((SG_HEADS, SG_CHUNK, SG_CHUNK), const3),
            pl.BlockSpec((SG_CHUNK, SG_WIDTH), const2),
        ],
        out_specs=pl.BlockSpec((None, TM, D_MODEL), lambda b, s: (b, s, 0)),
        out_shape=jax.ShapeDtypeStruct((B, S, D_MODEL), jnp.float32),
        scratch_shapes=[
            pltpu.VMEM((TM + 2 * HALO, CONV_WIDTH), jnp.float32),
            pltpu.VMEM((TM, D_MIX), jnp.bfloat16),
        ],
        compiler_params=pltpu.CompilerParams(
            dimension_semantics=("arbitrary", "arbitrary"), vmem_limit_bytes=VMEM_LIMIT),
        name="post",
    )(x, rest, rest, rest, att, wout, gpost, dw, dwb, cg, cb, sg, sb, sw, sbias)


def _rope_tables(S):
    half = ROPE_AXIS_DIM // 2
    t = jnp.arange(S, dtype=jnp.int32)
    row = (t // GRID_W).astype(jnp.float32)
    col = (t % GRID_W).astype(jnp.float32)
    inv_freq = ROPE_THETA ** (-jnp.arange(half, dtype=jnp.float32) / half)
    ang_r = row[:, None] * inv_freq[None, :]
    ang_c = col[:, None] * inv_freq[None, :]
    cos_h = jnp.concatenate([jnp.cos(ang_r)] * 2 + [jnp.cos(ang_c)] * 2, axis=-1)
    sin_h = jnp.concatenate([-jnp.sin(ang_r), jnp.sin(ang_r), -jnp.sin(ang_c), jnp.sin(ang_c)], axis=-1)
    reps = LANES // HEAD_DIM
    return jnp.tile(cos_h, (1, reps)), jnp.tile(sin_h, (1, reps))


def kernel(x, pre_norm, post_norm, w_in, w_out, q_norm, k_norm, conv_dw, conv_dw_b,
           conv_ln_g, conv_ln_b, sg_ln_g, sg_ln_b, sg_w, sg_b):
    B, S, _ = x.shape
    depth = w_in.shape[0]
    cos_t, sin_t = _rope_tables(S)
    head_id = jnp.arange(ATT_WIDTH, dtype=jnp.int32) // HEAD_DIM
    bd = _bf16(jnp.where(head_id[:, None] == head_id[None, :], 1.0 / HEAD_DIM, 0.0))
    row2 = lambda a: a.reshape(1, -1)
    for l in range(depth):
        qt, k, vt, rest = _in_proj(
            x, row2(pre_norm[l]), _bf16(w_in[l]),
            row2(jnp.tile(q_norm[l], ATT_HEADS)), row2(jnp.tile(k_norm[l], ATT_KV_HEADS)),
            bd, cos_t, sin_t)
        att = _attention(qt, k, vt)
        sbias = jnp.repeat(sg_b[l].T, HEAD_DIM, axis=1)
        x = _post(x, rest, att, _bf16(w_out[l]), row2(post_norm[l]),
                  conv_dw[l], row2(conv_dw_b[l]), row2(conv_ln_g[l]), row2(conv_ln_b[l]),
                  row2(sg_ln_g[l]), row2(sg_ln_b[l]), _bf16(sg_w[l]), sbias)
    return x
```

```python
import math

import jax
import jax.numpy as jnp
from jax import lax
from jax.experimental import pallas as pl
from jax.experimental.pallas import tpu as pltpu

D_MODEL = 1024
HEAD_DIM = 64
GRID_W = 64
EPS = 1e-6
ATT_HEADS = 8
ATT_KV_HEADS = 2
GQA = ATT_HEADS // ATT_KV_HEADS
ATT_WIDTH = ATT_HEADS * HEAD_DIM
KV_WIDTH = ATT_KV_HEADS * HEAD_DIM
ROPE_THETA = 10000.0
ROPE_AXIS_DIM = HEAD_DIM // 2
CONV_WIDTH = 256
CONV_KERNEL = 31
CONV_PAD = CONV_KERNEL // 2
SG_HEADS = 4
SG_WIDTH = SG_HEADS * HEAD_DIM
SG_CHUNK = 128
D_MIX = ATT_WIDTH + CONV_WIDTH + SG_WIDTH
QKV_WIDTH = ATT_WIDTH + 2 * KV_WIDTH
REST_WIDTH = ATT_WIDTH + 3 * CONV_WIDTH + 3 * SG_WIDTH
D_IN = QKV_WIDTH + REST_WIDTH

R_GATT = 0
R_ACONV = ATT_WIDTH
R_GCONV = R_ACONV + 2 * CONV_WIDTH
R_U = R_GCONV + CONV_WIDTH
R_V = R_U + SG_WIDTH
R_GSG = R_V + SG_WIDTH

LANES = 128
SUBLANES = 8
HALO = 16
TM = 512
TQ = 512
TK = 512
ATT_UNROLL = 8
NQ = GQA * TQ
LANE_GROUP = 256
V_ROWS = HEAD_DIM + 16
SCORE_SCALE = HEAD_DIM ** -0.5 * math.log2(math.e)
VMEM_LIMIT = 56 * 1024 * 1024


def _bf16(x):
    return x.astype(jnp.bfloat16)


def _group_mean_sq(x, bd):
    x2 = x * x
    hi = _bf16(x2)
    lo = _bf16(x2 - hi.astype(jnp.float32))
    return (jnp.dot(hi, bd, preferred_element_type=jnp.float32)
            + jnp.dot(lo, bd, preferred_element_type=jnp.float32))


def _rope_slab(xn, cos, sin_signed, first_half):
    fwd = pltpu.roll(xn, LANES - ROPE_AXIS_DIM // 2, 1)
    bwd = pltpu.roll(xn, ROPE_AXIS_DIM // 2, 1)
    partner = jnp.where(first_half, fwd, bwd)
    return xn * cos + partner * sin_signed


def _in_proj_kernel(x_ref, g_ref, w_ref, gq_ref, gk_ref, bd_ref, cos_ref, sin_ref,
                    qt_ref, k_ref, vt_ref, rest_ref):
    x = x_ref[...]
    ms = jnp.mean(x * x, axis=-1, keepdims=True)
    h = _bf16(x * lax.rsqrt(ms + EPS) * g_ref[...])

    cos = cos_ref[...]
    sin_signed = sin_ref[...]
    lane = lax.broadcasted_iota(jnp.int32, (1, LANES), 1)
    first_half = (lane % ROPE_AXIS_DIM) < (ROPE_AXIS_DIM // 2)

    q = jnp.dot(h, w_ref[:, 0:ATT_WIDTH], preferred_element_type=jnp.float32)
    qn = q * lax.rsqrt(_group_mean_sq(q, bd_ref[...]) + EPS) * gq_ref[...]
    for j in range(ATT_WIDTH // LANES):
        sl = slice(j * LANES, (j + 1) * LANES)
        r = _rope_slab(qn[:, sl], cos, sin_signed, first_half) * SCORE_SCALE
        qt_ref[sl, :] = _bf16(r.T)

    kk = jnp.dot(h, w_ref[:, ATT_WIDTH:ATT_WIDTH + KV_WIDTH], preferred_element_type=jnp.float32)
    kn = kk * lax.rsqrt(_group_mean_sq(kk, bd_ref[0:KV_WIDTH, 0:KV_WIDTH]) + EPS) * gk_ref[...]
    k_ref[...] = _bf16(_rope_slab(kn, cos, sin_signed, first_half))

    v = jnp.dot(h, w_ref[:, ATT_WIDTH + KV_WIDTH:QKV_WIDTH], preferred_element_type=jnp.float32)
    vt = _bf16(v.T)
    for c in range(TM // TK):
        for hd in range(ATT_KV_HEADS):
            vt_ref[c, hd, 0:HEAD_DIM, :] = vt[hd * HEAD_DIM:(hd + 1) * HEAD_DIM, c * TK:(c + 1) * TK]
            vt_ref[c, hd, HEAD_DIM:V_ROWS, :] = jnp.ones((V_ROWS - HEAD_DIM, TK), jnp.bfloat16)

    for c in range(REST_WIDTH // 512):
        rest_ref[:, c * 512:(c + 1) * 512] = jnp.dot(
            h, w_ref[:, QKV_WIDTH + c * 512:QKV_WIDTH + (c + 1) * 512],
            preferred_element_type=jnp.float32)


def _in_proj(x, g, w, gq, gk, bd, cos_t, sin_t):
    B, S, _ = x.shape
    ns = S // TM
    const = lambda b, s: (0, 0)
    return pl.pallas_call(
        _in_proj_kernel,
        grid=(B, ns),
        in_specs=[
            pl.BlockSpec((None, TM, D_MODEL), lambda b, s: (b, s, 0)),
            pl.BlockSpec((1, D_MODEL), const),
            pl.BlockSpec((D_MODEL, D_IN), const),
            pl.BlockSpec((1, ATT_WIDTH), const),
            pl.BlockSpec((1, KV_WIDTH), const),
            pl.BlockSpec((ATT_WIDTH, ATT_WIDTH), const),
            pl.BlockSpec((TM, LANES), lambda b, s: (s, 0)),
            pl.BlockSpec((TM, LANES), lambda b, s: (s, 0)),
        ],
        out_specs=[
            pl.BlockSpec((None, ATT_WIDTH, TM), lambda b, s: (b, 0, s)),
            pl.BlockSpec((None, TM, KV_WIDTH), lambda b, s: (b, s, 0)),
            pl.BlockSpec((None, TM // TK, ATT_KV_HEADS, V_ROWS, TK), lambda b, s: (b, s, 0, 0, 0)),
            pl.BlockSpec((None, TM, REST_WIDTH), lambda b, s: (b, s, 0)),
        ],
        out_shape=[
            jax.ShapeDtypeStruct((B, ATT_WIDTH, S), jnp.bfloat16),
            jax.ShapeDtypeStruct((B, S, KV_WIDTH), jnp.bfloat16),
            jax.ShapeDtypeStruct((B, S // TK, ATT_KV_HEADS, V_ROWS, TK), jnp.bfloat16),
            jax.ShapeDtypeStruct((B, S, REST_WIDTH), jnp.float32),
        ],
        compiler_params=pltpu.CompilerParams(
            dimension_semantics=("arbitrary", "arbitrary"), vmem_limit_bytes=VMEM_LIMIT),
        name="in_proj",
    )(x, g, w, gq, gk, bd, cos_t, sin_t)


def _attention_kernel(qt_ref, k_ref, vt_ref, o_ref, qext_ref, s_ref, p_ref, acc_ref, m_ref, mx_ref):
    kv = pl.program_id(1)
    n = k_ref.shape[0] // TK
    for g in range(GQA):
        qg = qt_ref[g * HEAD_DIM:(g + 1) * HEAD_DIM, :]
        zero = jnp.zeros_like(qg)
        qext_ref[0:HEAD_DIM, g * TQ:(g + 1) * TQ] = jnp.where(kv == 0, qg, zero)
        qext_ref[HEAD_DIM:2 * HEAD_DIM, g * TQ:(g + 1) * TQ] = jnp.where(kv == 1, qg, zero)

    def scores(j, g, cols):
        kblk = k_ref[pl.ds(pl.multiple_of(j * TK, TK), TK), :]
        s = jnp.dot(kblk, qext_ref[:, cols], preferred_element_type=jnp.float32)
        s_ref[g] = s
        mx_ref[:, cols] = jnp.max(s, axis=0, keepdims=True)

    def values(j, g):
        return jnp.dot(vt_ref[j], p_ref[g], preferred_element_type=jnp.float32)

    groups = [(g, slice(g * LANE_GROUP, (g + 1) * LANE_GROUP)) for g in range(NQ // LANE_GROUP)]
    acc_ref[...] = jnp.zeros_like(acc_ref)
    p_ref[...] = jnp.zeros_like(p_ref)
    m_ref[...] = jnp.full_like(m_ref, -jnp.inf)
    for g, cols in groups:
        scores(0, g, cols)

    def step(j, carry):
        for g, cols in groups:
            pv = values(jnp.maximum(j - 1, 0), g)
            m_old = m_ref[:, cols]
            m_new = jnp.maximum(m_old, mx_ref[:, cols])
            m_ref[:, cols] = m_new
            p_ref[g] = _bf16(jnp.exp2(s_ref[g] - m_new))
            acc_ref[:, cols] = jnp.exp2(m_old - m_new) * (acc_ref[:, cols] + pv)
            scores(jnp.minimum(j + 1, n - 1), g, cols)
        return carry

    lax.fori_loop(0, n, step, 0, unroll=ATT_UNROLL)
    for g, cols in groups:
        acc_ref[:, cols] = acc_ref[:, cols] + values(n - 1, g)
    o = acc_ref[0:HEAD_DIM, :] / acc_ref[HEAD_DIM:HEAD_DIM + 1, :]
    o4 = jnp.concatenate([o[:, g * TQ:(g + 1) * TQ] for g in range(GQA)], axis=0)
    o_ref[...] = o4.T


def _attention(qt, k, vt):
    B, _, S = qt.shape
    nk = vt.shape[1]
    assert vt.shape[-1] == TK
    return pl.pallas_call(
        _attention_kernel,
        grid=(B, ATT_KV_HEADS, S // TQ),
        in_specs=[
            pl.BlockSpec((None, GQA * HEAD_DIM, TQ), lambda b, kv, qi: (b, kv, qi)),
            pl.BlockSpec((None, S, KV_WIDTH), lambda b, kv, qi: (b, 0, 0)),
            pl.BlockSpec((None, nk, None, V_ROWS, TK), lambda b, kv, qi: (b, 0, kv, 0, 0)),
        ],
        out_specs=pl.BlockSpec((None, TQ, GQA * HEAD_DIM), lambda b, kv, qi: (b, qi, kv)),
        out_shape=jax.ShapeDtypeStruct((B, S, ATT_WIDTH), jnp.float32),
        scratch_shapes=[
            pltpu.VMEM((KV_WIDTH, NQ), jnp.bfloat16),
            pltpu.VMEM((NQ // LANE_GROUP, TK, LANE_GROUP), jnp.float32),
            pltpu.VMEM((NQ // LANE_GROUP, TK, LANE_GROUP), jnp.bfloat16),
            pltpu.VMEM((V_ROWS, NQ), jnp.float32),
            pltpu.VMEM((1, NQ), jnp.float32),
            pltpu.VMEM((1, NQ), jnp.float32),
        ],
        compiler_params=pltpu.CompilerParams(
            dimension_semantics=("arbitrary", "arbitrary", "arbitrary"),
            vmem_limit_bytes=VMEM_LIMIT),
        name="attention",
    )(qt, k, vt)


def _silu(x):
    return x * jax.nn.sigmoid(x)


def _gelu(x):
    return 0.5 * x * (1.0 + lax.erf(x * (2.0 ** -0.5)))


def _layer_norm(x, g, b):
    mu = jnp.mean(x, axis=-1, keepdims=True)
    xc = x - mu
    var = jnp.mean(xc * xc, axis=-1, keepdims=True)
    return xc * lax.rsqrt(var + EPS) * g + b


def _glu(a):
    return a[:, :CONV_WIDTH] * jax.nn.sigmoid(a[:, CONV_WIDTH:])


CONV_ROWS = 64


def _post_kernel(x_ref, rest_ref, prev_ref, next_ref, att_ref, wout_ref, gpost_ref,
                 dw_ref, dwb_ref, cg_ref, cb_ref, sg_ref, sb_ref, sw_ref, sbias_ref,
                 o_ref, hext_ref, hsh_ref, cat_ref):
    si = pl.program_id(1)
    last = pl.num_programs(1) - 1

    cat_ref[:, 0:ATT_WIDTH] = _bf16(att_ref[...] * _silu(rest_ref[:, R_GATT:R_GATT + ATT_WIDTH]))

    hext_ref[0:HALO, :] = jnp.where(si > 0, _glu(prev_ref[...]), 0.0)
    hext_ref[HALO:HALO + TM, :] = _glu(rest_ref[:, R_ACONV:R_ACONV + 2 * CONV_WIDTH])
    hext_ref[HALO + TM:2 * HALO + TM, :] = jnp.where(si < last, _glu(next_ref[...]), 0.0)
    for r in range(SUBLANES):
        hsh_ref[r] = hext_ref[r:r + TM + 2 * HALO - SUBLANES, :]
    for rb in range(TM // CONV_ROWS):
        acc = jnp.zeros((CONV_ROWS, CONV_WIDTH), jnp.float32) + dwb_ref[...]
        for j in range(CONV_KERNEL):
            r = (j + HALO - CONV_PAD) % SUBLANES
            start = rb * CONV_ROWS + j + HALO - CONV_PAD - r
            acc = acc + hsh_ref[r, start:start + CONV_ROWS, :] * dw_ref[j:j + 1, :]
        rows = slice(rb * CONV_ROWS, (rb + 1) * CONV_ROWS)
        c = _silu(_layer_norm(acc, cg_ref[...], cb_ref[...]))
        c = c * _silu(rest_ref[rows, R_GCONV:R_GCONV + CONV_WIDTH])
        cat_ref[rows, ATT_WIDTH:ATT_WIDTH + CONV_WIDTH] = _bf16(c)

    lane = lax.broadcasted_iota(jnp.int32, (1, SG_WIDTH), 1)
    for c in range(TM // SG_CHUNK):
        rows = slice(c * SG_CHUNK, (c + 1) * SG_CHUNK)
        vln = _bf16(_layer_norm(_gelu(rest_ref[rows, R_V:R_V + SG_WIDTH]), sg_ref[...], sb_ref[...]))
        mixed = sbias_ref[...]
        for hd in range(SG_HEADS):
            mh = jnp.dot(sw_ref[hd], vln, preferred_element_type=jnp.float32)
            mixed = mixed + jnp.where(lane // HEAD_DIM == hd, mh, 0.0)
        sgu = _gelu(rest_ref[rows, R_U:R_U + SG_WIDTH]) * mixed
        sgu = sgu * _silu(rest_ref[rows, R_GSG:R_GSG + SG_WIDTH])
        cat_ref[rows, ATT_WIDTH + CONV_WIDTH:D_MIX] = _bf16(sgu)

    mix = jnp.dot(cat_ref[...], wout_ref[...], preferred_element_type=jnp.float32)
    ms = jnp.mean(mix * mix, axis=-1, keepdims=True)
    o_ref[...] = x_ref[...] + mix * lax.rsqrt(ms + EPS) * gpost_ref[...]


def _post(x, rest, att, wout, gpost, dw, dwb, cg, cb, sg, sb, sw, sbias):
    B, S, _ = x.shape
    ns = S // TM
    hb = TM // HALO
    const2 = lambda b, s: (0, 0)
    const3 = lambda b, s: (0, 0, 0)
    aconv_blk = R_ACONV // (2 * CONV_WIDTH)
    return pl.pallas_call(
        _post_kernel,
        grid=(B, ns),
        in_specs=[
            pl.BlockSpec((None, TM, D_MODEL), lambda b, s: (b, s, 0)),
            pl.BlockSpec((None, TM, REST_WIDTH), lambda b, s: (b, s, 0)),
            pl.BlockSpec((None, HALO, 2 * CONV_WIDTH),
                         lambda b, s: (b, jnp.maximum(s * hb - 1, 0), aconv_blk)),
            pl.BlockSpec((None, HALO, 2 * CONV_WIDTH),
                         lambda b, s: (b, jnp.minimum((s + 1) * hb, S // HALO - 1), aconv_blk)),
            pl.BlockSpec((None, TM, ATT_WIDTH), lambda b, s: (b, s, 0)),
            pl.BlockSpec((D_MIX, D_MODEL), const2),
            pl.BlockSpec((1, D_MODEL), const2),
            pl.BlockSpec((CONV_KERNEL, CONV_WIDTH), const2),
            pl.BlockSpec((1, CONV_WIDTH), const2),
            pl.BlockSpec((1, CONV_WIDTH), const2),
            pl.BlockSpec((1, CONV_WIDTH), const2),
            pl.BlockSpec((1, SG_WIDTH), const2),
            pl.BlockSpec((1, SG_WIDTH), const2),
            pl.BlockSpec((SG_HEADS, SG_CHUNK, SG_CHUNK), const3),
            pl.BlockSpec((SG_CHUNK, SG_WIDTH), const2),
        ],
        out_specs=pl.BlockSpec((None, TM, D_MODEL), lambda b, s: (b, s, 0)),
        out_shape=jax.ShapeDtypeStruct((B, S, D_MODEL), jnp.float32),
        scratch_shapes=[
            pltpu.VMEM((TM + 2 * HALO, CONV_WIDTH), jnp.float32),
            pltpu.VMEM((SUBLANES, TM + 2 * HALO - SUBLANES, CONV_WIDTH), jnp.float32),
            pltpu.VMEM((TM, D_MIX), jnp.bfloat16),
        ],
        compiler_params=pltpu.CompilerParams(
            dimension_semantics=("arbitrary", "arbitrary"), vmem_limit_bytes=VMEM_LIMIT),
        name="post",
    )(x, rest, rest, rest, att, wout, gpost, dw, dwb, cg, cb, sg, sb, sw, sbias)


def _rope_tables(S):
    half = ROPE_AXIS_DIM // 2
    t = jnp.arange(S, dtype=jnp.int32)
    row = (t // GRID_W).astype(jnp.float32)
    col = (t % GRID_W).astype(jnp.float32)
    inv_freq = ROPE_THETA ** (-jnp.arange(half, dtype=jnp.float32) / half)
    ang_r = row[:, None] * inv_freq[None, :]
    ang_c = col[:, None] * inv_freq[None, :]
    cos_h = jnp.concatenate([jnp.cos(ang_r)] * 2 + [jnp.cos(ang_c)] * 2, axis=-1)
    sin_h = jnp.concatenate([-jnp.sin(ang_r), jnp.sin(ang_r), -jnp.sin(ang_c), jnp.sin(ang_c)], axis=-1)
    reps = LANES // HEAD_DIM
    return jnp.tile(cos_h, (1, reps)), jnp.tile(sin_h, (1, reps))


def kernel(x, pre_norm, post_norm, w_in, w_out, q_norm, k_norm, conv_dw, conv_dw_b,
           conv_ln_g, conv_ln_b, sg_ln_g, sg_ln_b, sg_w, sg_b):
    B, S, _ = x.shape
    depth = w_in.shape[0]
    cos_t, sin_t = _rope_tables(S)
    head_id = jnp.arange(ATT_WIDTH, dtype=jnp.int32) // HEAD_DIM
    bd = _bf16(jnp.where(head_id[:, None] == head_id[None, :], 1.0 / HEAD_DIM, 0.0))
    row2 = lambda a: a.reshape(1, -1)
    for l in range(depth):
        qt, k, vt, rest = _in_proj(
            x, row2(pre_norm[l]), _bf16(w_in[l]),
            row2(jnp.tile(q_norm[l], ATT_HEADS)), row2(jnp.tile(k_norm[l], ATT_KV_HEADS)),
            bd, cos_t, sin_t)
        att = _attention(qt, k, vt)
        sbias = jnp.repeat(sg_b[l].T, HEAD_DIM, axis=1)
        x = _post(x, rest, att, _bf16(w_out[l]), row2(post_norm[l]),
                  conv_dw[l], row2(conv_dw_b[l]), row2(conv_ln_g[l]), row2(conv_ln_b[l]),
                  row2(sg_ln_g[l]), row2(sg_ln_b[l]), _bf16(sg_w[l]), sbias)
    return x
```

```python
import math

import jax
import jax.numpy as jnp
from jax import lax
from jax.experimental import pallas as pl
from jax.experimental.pallas import tpu as pltpu

D_MODEL = 1024
HEAD_DIM = 64
GRID_W = 64
EPS = 1e-6
ATT_HEADS = 8
ATT_KV_HEADS = 2
GQA = ATT_HEADS // ATT_KV_HEADS
ATT_WIDTH = ATT_HEADS * HEAD_DIM
KV_WIDTH = ATT_KV_HEADS * HEAD_DIM
ROPE_THETA = 10000.0
ROPE_AXIS_DIM = HEAD_DIM // 2
CONV_WIDTH = 256
CONV_KERNEL = 31
CONV_PAD = CONV_KERNEL // 2
SG_HEADS = 4
SG_WIDTH = SG_HEADS * HEAD_DIM
SG_CHUNK = 128
D_MIX = ATT_WIDTH + CONV_WIDTH + SG_WIDTH
QKV_WIDTH = ATT_WIDTH + 2 * KV_WIDTH
REST_WIDTH = ATT_WIDTH + 3 * CONV_WIDTH + 3 * SG_WIDTH
D_IN = QKV_WIDTH + REST_WIDTH

R_GATT = 0
R_ACONV = ATT_WIDTH
R_GCONV = R_ACONV + 2 * CONV_WIDTH
R_U = R_GCONV + CONV_WIDTH
R_V = R_U + SG_WIDTH
R_GSG = R_V + SG_WIDTH

LANES = 128
SUBLANES = 8
HALO = 16
TM = 512
TQ = 512
TK = 512
ATT_UNROLL = 16
NQ = GQA * TQ
LANE_GROUP = 256
V_ROWS = HEAD_DIM + 16
SCORE_SCALE = HEAD_DIM ** -0.5 * math.log2(math.e)
VMEM_LIMIT = 56 * 1024 * 1024


def _bf16(x):
    return x.astype(jnp.bfloat16)


def _group_mean_sq(x, bd):
    x2 = x * x
    hi = _bf16(x2)
    lo = _bf16(x2 - hi.astype(jnp.float32))
    return (jnp.dot(hi, bd, preferred_element_type=jnp.float32)
            + jnp.dot(lo, bd, preferred_element_type=jnp.float32))


def _rope_slab(xn, cos, sin_signed, first_half):
    fwd = pltpu.roll(xn, LANES - ROPE_AXIS_DIM // 2, 1)
    bwd = pltpu.roll(xn, ROPE_AXIS_DIM // 2, 1)
    partner = jnp.where(first_half, fwd, bwd)
    return xn * cos + partner * sin_signed


def _in_proj_kernel(x_ref, g_ref, w_ref, gq_ref, gk_ref, bd_ref, cos_ref, sin_ref,
                    qt_ref, k_ref, vt_ref, rest_ref):
    x = x_ref[...]
    ms = jnp.mean(x * x, axis=-1, keepdims=True)
    h = _bf16(x * lax.rsqrt(ms + EPS) * g_ref[...])

    cos = cos_ref[...]
    sin_signed = sin_ref[...]
    lane = lax.broadcasted_iota(jnp.int32, (1, LANES), 1)
    first_half = (lane % ROPE_AXIS_DIM) < (ROPE_AXIS_DIM // 2)

    q = jnp.dot(h, w_ref[:, 0:ATT_WIDTH], preferred_element_type=jnp.float32)
    qn = q * lax.rsqrt(_group_mean_sq(q, bd_ref[...]) + EPS) * gq_ref[...]
    for j in range(ATT_WIDTH // LANES):
        sl = slice(j * LANES, (j + 1) * LANES)
        r = _rope_slab(qn[:, sl], cos, sin_signed, first_half) * SCORE_SCALE
        qt_ref[sl, :] = _bf16(r.T)

    kk = jnp.dot(h, w_ref[:, ATT_WIDTH:ATT_WIDTH + KV_WIDTH], preferred_element_type=jnp.float32)
    kn = kk * lax.rsqrt(_group_mean_sq(kk, bd_ref[0:KV_WIDTH, 0:KV_WIDTH]) + EPS) * gk_ref[...]
    k_ref[...] = _bf16(_rope_slab(kn, cos, sin_signed, first_half))

    v = jnp.dot(h, w_ref[:, ATT_WIDTH + KV_WIDTH:QKV_WIDTH], preferred_element_type=jnp.float32)
    vt = _bf16(v.T)
    for c in range(TM // TK):
        for hd in range(ATT_KV_HEADS):
            vt_ref[c, hd, 0:HEAD_DIM, :] = vt[hd * HEAD_DIM:(hd + 1) * HEAD_DIM, c * TK:(c + 1) * TK]
            vt_ref[c, hd, HEAD_DIM:V_ROWS, :] = jnp.ones((V_ROWS - HEAD_DIM, TK), jnp.bfloat16)

    for c in range(REST_WIDTH // 512):
        rest_ref[:, c * 512:(c + 1) * 512] = jnp.dot(
            h, w_ref[:, QKV_WIDTH + c * 512:QKV_WIDTH + (c + 1) * 512],
            preferred_element_type=jnp.float32)


def _in_proj(x, g, w, gq, gk, bd, cos_t, sin_t):
    B, S, _ = x.shape
    ns = S // TM
    const = lambda b, s: (0, 0)
    return pl.pallas_call(
        _in_proj_kernel,
        grid=(B, ns),
        in_specs=[
            pl.BlockSpec((None, TM, D_MODEL), lambda b, s: (b, s, 0)),
            pl.BlockSpec((1, D_MODEL), const),
            pl.BlockSpec((D_MODEL, D_IN), const),
            pl.BlockSpec((1, ATT_WIDTH), const),
            pl.BlockSpec((1, KV_WIDTH), const),
            pl.BlockSpec((ATT_WIDTH, ATT_WIDTH), const),
            pl.BlockSpec((TM, LANES), lambda b, s: (s, 0)),
            pl.BlockSpec((TM, LANES), lambda b, s: (s, 0)),
        ],
        out_specs=[
            pl.BlockSpec((None, ATT_WIDTH, TM), lambda b, s: (b, 0, s)),
            pl.BlockSpec((None, TM, KV_WIDTH), lambda b, s: (b, s, 0)),
            pl.BlockSpec((None, TM // TK, ATT_KV_HEADS, V_ROWS, TK), lambda b, s: (b, s, 0, 0, 0)),
            pl.BlockSpec((None, TM, REST_WIDTH), lambda b, s: (b, s, 0)),
        ],
        out_shape=[
            jax.ShapeDtypeStruct((B, ATT_WIDTH, S), jnp.bfloat16),
            jax.ShapeDtypeStruct((B, S, KV_WIDTH), jnp.bfloat16),
            jax.ShapeDtypeStruct((B, S // TK, ATT_KV_HEADS, V_ROWS, TK), jnp.bfloat16),
            jax.ShapeDtypeStruct((B, S, REST_WIDTH), jnp.float32),
        ],
        compiler_params=pltpu.CompilerParams(
            dimension_semantics=("arbitrary", "arbitrary"), vmem_limit_bytes=VMEM_LIMIT),
        name="in_proj",
    )(x, g, w, gq, gk, bd, cos_t, sin_t)


def _attention_kernel(qt_ref, k_ref, vt_ref, o_ref, qext_ref, s_ref, p_ref, acc_ref, m_ref, mx_ref):
    kv = pl.program_id(1)
    n = k_ref.shape[0] // TK
    for g in range(GQA):
        qg = qt_ref[g * HEAD_DIM:(g + 1) * HEAD_DIM, :]
        zero = jnp.zeros_like(qg)
        qext_ref[0:HEAD_DIM, g * TQ:(g + 1) * TQ] = jnp.where(kv == 0, qg, zero)
        qext_ref[HEAD_DIM:2 * HEAD_DIM, g * TQ:(g + 1) * TQ] = jnp.where(kv == 1, qg, zero)

    def scores(j, g, cols):
        kblk = k_ref[pl.ds(pl.multiple_of(j * TK, TK), TK), :]
        s = jnp.dot(kblk, qext_ref[:, cols], preferred_element_type=jnp.float32)
        s_ref[g] = s
        mx_ref[:, cols] = jnp.max(s, axis=0, keepdims=True)

    def values(j, g):
        return jnp.dot(vt_ref[j], p_ref[g], preferred_element_type=jnp.float32)

    groups = [(g, slice(g * LANE_GROUP, (g + 1) * LANE_GROUP)) for g in range(NQ // LANE_GROUP)]
    acc_ref[...] = jnp.zeros_like(acc_ref)
    p_ref[...] = jnp.zeros_like(p_ref)
    m_ref[...] = jnp.full_like(m_ref, -jnp.inf)
    for g, cols in groups:
        scores(0, g, cols)

    def step(j, carry):
        for g, cols in groups:
            pv = values(jnp.maximum(j - 1, 0), g)
            m_old = m_ref[:, cols]
            m_new = jnp.maximum(m_old, mx_ref[:, cols])
            m_ref[:, cols] = m_new
            p_ref[g] = _bf16(jnp.exp2(s_ref[g] - m_new))
            acc_ref[:, cols] = jnp.exp2(m_old - m_new) * (acc_ref[:, cols] + pv)
            scores(jnp.minimum(j + 1, n - 1), g, cols)
        return carry

    lax.fori_loop(0, n, step, 0, unroll=ATT_UNROLL)
    for g, cols in groups:
        acc_ref[:, cols] = acc_ref[:, cols] + values(n - 1, g)
    o = acc_ref[0:HEAD_DIM, :] / acc_ref[HEAD_DIM:HEAD_DIM + 1, :]
    o4 = jnp.concatenate([o[:, g * TQ:(g + 1) * TQ] for g in range(GQA)], axis=0)
    o_ref[...] = o4.T


def _attention(qt, k, vt):
    B, _, S = qt.shape
    nk = vt.shape[1]
    assert vt.shape[-1] == TK
    return pl.pallas_call(
        _attention_kernel,
        grid=(B, ATT_KV_HEADS, S // TQ),
        in_specs=[
            pl.BlockSpec((None, GQA * HEAD_DIM, TQ), lambda b, kv, qi: (b, kv, qi)),
            pl.BlockSpec((None, S, KV_WIDTH), lambda b, kv, qi: (b, 0, 0)),
            pl.BlockSpec((None, nk, None, V_ROWS, TK), lambda b, kv, qi: (b, 0, kv, 0, 0)),
        ],
        out_specs=pl.BlockSpec((None, TQ, GQA * HEAD_DIM), lambda b, kv, qi: (b, qi, kv)),
        out_shape=jax.ShapeDtypeStruct((B, S, ATT_WIDTH), jnp.float32),
        scratch_shapes=[
            pltpu.VMEM((KV_WIDTH, NQ), jnp.bfloat16),
            pltpu.VMEM((NQ // LANE_GROUP, TK, LANE_GROUP), jnp.float32),
            pltpu.VMEM((NQ // LANE_GROUP, TK, LANE_GROUP), jnp.bfloat16),
            pltpu.VMEM((V_ROWS, NQ), jnp.float32),
            pltpu.VMEM((1, NQ), jnp.float32),
            pltpu.VMEM((1, NQ), jnp.float32),
        ],
        compiler_params=pltpu.CompilerParams(
            dimension_semantics=("arbitrary", "arbitrary", "arbitrary"),
            vmem_limit_bytes=VMEM_LIMIT),
        name="attention",
    )(qt, k, vt)


def _silu(x):
    return x * jax.nn.sigmoid(x)


def _gelu(x):
    return 0.5 * x * (1.0 + lax.erf(x * (2.0 ** -0.5)))


def _layer_norm(x, g, b):
    mu = jnp.mean(x, axis=-1, keepdims=True)
    xc = x - mu
    var = jnp.mean(xc * xc, axis=-1, keepdims=True)
    return xc * lax.rsqrt(var + EPS) * g + b


def _glu(a):
    return a[:, :CONV_WIDTH] * jax.nn.sigmoid(a[:, CONV_WIDTH:])


CONV_ROWS = 64


def _post_kernel(x_ref, rest_ref, prev_ref, next_ref, att_ref, wout_ref, gpost_ref,
                 dw_ref, dwb_ref, cg_ref, cb_ref, sg_ref, sb_ref, sw_ref, sbias_ref,
                 o_ref, hext_ref, hsh_ref, cat_ref):
    si = pl.program_id(1)
    last = pl.num_programs(1) - 1

    cat_ref[:, 0:ATT_WIDTH] = _bf16(att_ref[...] * _silu(rest_ref[:, R_GATT:R_GATT + ATT_WIDTH]))

    hext_ref[0:HALO, :] = jnp.where(si > 0, _glu(prev_ref[...]), 0.0)
    hext_ref[HALO:HALO + TM, :] = _glu(rest_ref[:, R_ACONV:R_ACONV + 2 * CONV_WIDTH])
    hext_ref[HALO + TM:2 * HALO + TM, :] = jnp.where(si < last, _glu(next_ref[...]), 0.0)
    for r in range(SUBLANES):
        hsh_ref[r] = hext_ref[r:r + TM + 2 * HALO - SUBLANES, :]
    for rb in range(TM // CONV_ROWS):
        acc = jnp.zeros((CONV_ROWS, CONV_WIDTH), jnp.float32) + dwb_ref[...]
        for j in range(CONV_KERNEL):
            r = (j + HALO - CONV_PAD) % SUBLANES
            start = rb * CONV_ROWS + j + HALO - CONV_PAD - r
            acc = acc + hsh_ref[r, start:start + CONV_ROWS, :] * dw_ref[j:j + 1, :]
        rows = slice(rb * CONV_ROWS, (rb + 1) * CONV_ROWS)
        c = _silu(_layer_norm(acc, cg_ref[...], cb_ref[...]))
        c = c * _silu(rest_ref[rows, R_GCONV:R_GCONV + CONV_WIDTH])
        cat_ref[rows, ATT_WIDTH:ATT_WIDTH + CONV_WIDTH] = _bf16(c)

    lane = lax.broadcasted_iota(jnp.int32, (1, SG_WIDTH), 1)
    for c in range(TM // SG_CHUNK):
        rows = slice(c * SG_CHUNK, (c + 1) * SG_CHUNK)
        vln = _bf16(_layer_norm(_gelu(rest_ref[rows, R_V:R_V + SG_WIDTH]), sg_ref[...], sb_ref[...]))
        mixed = sbias_ref[...]
        for hd in range(SG_HEADS):
            mh = jnp.dot(sw_ref[hd], vln, preferred_element_type=jnp.float32)
            mixed = mixed + jnp.where(lane // HEAD_DIM == hd, mh, 0.0)
        sgu = _gelu(rest_ref[rows, R_U:R_U + SG_WIDTH]) * mixed
        sgu = sgu * _silu(rest_ref[rows, R_GSG:R_GSG + SG_WIDTH])
        cat_ref[rows, ATT_WIDTH + CONV_WIDTH:D_MIX] = _bf16(sgu)

    mix = jnp.dot(cat_ref[...], wout_ref[...], preferred_element_type=jnp.float32)
    ms = jnp.mean(mix * mix, axis=-1, keepdims=True)
    o_ref[...] = x_ref[...] + mix * lax.rsqrt(ms + EPS) * gpost_ref[...]


def _post(x, rest, att, wout, gpost, dw, dwb, cg, cb, sg, sb, sw, sbias):
    B, S, _ = x.shape
    ns = S // TM
    hb = TM // HALO
    const2 = lambda b, s: (0, 0)
    const3 = lambda b, s: (0, 0, 0)
    aconv_blk = R_ACONV // (2 * CONV_WIDTH)
    return pl.pallas_call(
        _post_kernel,
        grid=(B, ns),
        in_specs=[
            pl.BlockSpec((None, TM, D_MODEL), lambda b, s: (b, s, 0)),
            pl.BlockSpec((None, TM, REST_WIDTH), lambda b, s: (b, s, 0)),
            pl.BlockSpec((None, HALO, 2 * CONV_WIDTH),
                         lambda b, s: (b, jnp.maximum(s * hb - 1, 0), aconv_blk)),
            pl.BlockSpec((None, HALO, 2 * CONV_WIDTH),
                         lambda b, s: (b, jnp.minimum((s + 1) * hb, S // HALO - 1), aconv_blk)),
            pl.BlockSpec((None, TM, ATT_WIDTH), lambda b, s: (b, s, 0)),
            pl.BlockSpec((D_MIX, D_MODEL), const2),
            pl.BlockSpec((1, D_MODEL), const2),
            pl.BlockSpec((CONV_KERNEL, CONV_WIDTH), const2),
            pl.BlockSpec((1, CONV_WIDTH), const2),
            pl.BlockSpec((1, CONV_WIDTH), const2),
            pl.BlockSpec((1, CONV_WIDTH), const2),
            pl.BlockSpec((1, SG_WIDTH), const2),
            pl.BlockSpec((1, SG_WIDTH), const2),
            pl.BlockSpec((SG_HEADS, SG_CHUNK, SG_CHUNK), const3),
            pl.BlockSpec((SG_CHUNK, SG_WIDTH), const2),
        ],
        out_specs=pl.BlockSpec((None, TM, D_MODEL), lambda b, s: (b, s, 0)),
        out_shape=jax.ShapeDtypeStruct((B, S, D_MODEL), jnp.float32),
        scratch_shapes=[
            pltpu.VMEM((TM + 2 * HALO, CONV_WIDTH), jnp.float32),
            pltpu.VMEM((SUBLANES, TM + 2 * HALO - SUBLANES, CONV_WIDTH), jnp.float32),
            pltpu.VMEM((TM, D_MIX), jnp.bfloat16),
        ],
        compiler_params=pltpu.CompilerParams(
            dimension_semantics=("arbitrary", "arbitrary"), vmem_limit_bytes=VMEM_LIMIT),
        name="post",
    )(x, rest, rest, rest, att, wout, gpost, dw, dwb, cg, cb, sg, sb, sw, sbias)


def _rope_tables(S):
    half = ROPE_AXIS_DIM // 2
    t = jnp.arange(S, dtype=jnp.int32)
    row = (t // GRID_W).astype(jnp.float32)
    col = (t % GRID_W).astype(jnp.float32)
    inv_freq = ROPE_THETA ** (-jnp.arange(half, dtype=jnp.float32) / half)
    ang_r = row[:, None] * inv_freq[None, :]
    ang_c = col[:, None] * inv_freq[None, :]
    cos_h = jnp.concatenate([jnp.cos(ang_r)] * 2 + [jnp.cos(ang_c)] * 2, axis=-1)
    sin_h = jnp.concatenate([-jnp.sin(ang_r), jnp.sin(ang_r), -jnp.sin(ang_c), jnp.sin(ang_c)], axis=-1)
    reps = LANES // HEAD_DIM
    return jnp.tile(cos_h, (1, reps)), jnp.tile(sin_h, (1, reps))


def kernel(x, pre_norm, post_norm, w_in, w_out, q_norm, k_norm, conv_dw, conv_dw_b,
           conv_ln_g, conv_ln_b, sg_ln_g, sg_ln_b, sg_w, sg_b):
    B, S, _ = x.shape
    depth = w_in.shape[0]
    cos_t, sin_t = _rope_tables(S)
    head_id = jnp.arange(ATT_WIDTH, dtype=jnp.int32) // HEAD_DIM
    bd = _bf16(jnp.where(head_id[:, None] == head_id[None, :], 1.0 / HEAD_DIM, 0.0))
    row2 = lambda a: a.reshape(1, -1)
    for l in range(depth):
        qt, k, vt, rest = _in_proj(
            x, row2(pre_norm[l]), _bf16(w_in[l]),
            row2(jnp.tile(q_norm[l], ATT_HEADS)), row2(jnp.tile(k_norm[l], ATT_KV_HEADS)),
            bd, cos_t, sin_t)
        att = _attention(qt, k, vt)
        sbias = jnp.repeat(sg_b[l].T, HEAD_DIM, axis=1)
        x = _post(x, rest, att, _bf16(w_out[l]), row2(post_norm[l]),
                  conv_dw[l], row2(conv_dw_b[l]), row2(conv_ln_g[l]), row2(conv_ln_b[l]),
                  row2(sg_ln_g[l]), row2(sg_ln_b[l]), _bf16(sg_w[l]), sbias)
    return x
```

```python
import math

import jax
import jax.numpy as jnp
from jax import lax
from jax.experimental import pallas as pl
from jax.experimental.pallas import tpu as pltpu

D_MODEL = 1024
HEAD_DIM = 64
GRID_W = 64
EPS = 1e-6
ATT_HEADS = 8
ATT_KV_HEADS = 2
GQA = ATT_HEADS // ATT_KV_HEADS
ATT_WIDTH = ATT_HEADS * HEAD_DIM
KV_WIDTH = ATT_KV_HEADS * HEAD_DIM
ROPE_THETA = 10000.0
ROPE_AXIS_DIM = HEAD_DIM // 2
CONV_WIDTH = 256
CONV_KERNEL = 31
CONV_PAD = CONV_KERNEL // 2
SG_HEADS = 4
SG_WIDTH = SG_HEADS * HEAD_DIM
SG_CHUNK = 128
D_MIX = ATT_WIDTH + CONV_WIDTH + SG_WIDTH
QKV_WIDTH = ATT_WIDTH + 2 * KV_WIDTH
REST_WIDTH = ATT_WIDTH + 3 * CONV_WIDTH + 3 * SG_WIDTH
D_IN = QKV_WIDTH + REST_WIDTH

R_GATT = 0
R_ACONV = ATT_WIDTH
R_GCONV = R_ACONV + 2 * CONV_WIDTH
R_U = R_GCONV + CONV_WIDTH
R_V = R_U + SG_WIDTH
R_GSG = R_V + SG_WIDTH

LANES = 128
SUBLANES = 8
HALO = 16
TM = 512
TQ = 512
TK = 512
ATT_UNROLL = 10
NQ = GQA * TQ
LANE_GROUP = 256
V_ROWS = HEAD_DIM + 16
SCORE_SCALE = HEAD_DIM ** -0.5 * math.log2(math.e)
VMEM_LIMIT = 56 * 1024 * 1024


def _bf16(x):
    return x.astype(jnp.bfloat16)


def _group_mean_sq(x, bd):
    x2 = x * x
    hi = _bf16(x2)
    lo = _bf16(x2 - hi.astype(jnp.float32))
    return (jnp.dot(hi, bd, preferred_element_type=jnp.float32)
            + jnp.dot(lo, bd, preferred_element_type=jnp.float32))


def _rope_slab(xn, cos, sin_signed, first_half):
    fwd = pltpu.roll(xn, LANES - ROPE_AXIS_DIM // 2, 1)
    bwd = pltpu.roll(xn, ROPE_AXIS_DIM // 2, 1)
    partner = jnp.where(first_half, fwd, bwd)
    return xn * cos + partner * sin_signed


def _in_proj_kernel(x_ref, g_ref, w_ref, gq_ref, gk_ref, bd_ref, cos_ref, sin_ref,
                    qt_ref, k_ref, vt_ref, rest_ref):
    x = x_ref[...]
    ms = jnp.mean(x * x, axis=-1, keepdims=True)
    h = _bf16(x * lax.rsqrt(ms + EPS) * g_ref[...])

    cos = cos_ref[...]
    sin_signed = sin_ref[...]
    lane = lax.broadcasted_iota(jnp.int32, (1, LANES), 1)
    first_half = (lane % ROPE_AXIS_DIM) < (ROPE_AXIS_DIM // 2)

    q = jnp.dot(h, w_ref[:, 0:ATT_WIDTH], preferred_element_type=jnp.float32)
    qn = q * lax.rsqrt(_group_mean_sq(q, bd_ref[...]) + EPS) * gq_ref[...]
    for j in range(ATT_WIDTH // LANES):
        sl = slice(j * LANES, (j + 1) * LANES)
        r = _rope_slab(qn[:, sl], cos, sin_signed, first_half) * SCORE_SCALE
        qt_ref[sl, :] = _bf16(r.T)

    kk = jnp.dot(h, w_ref[:, ATT_WIDTH:ATT_WIDTH + KV_WIDTH], preferred_element_type=jnp.float32)
    kn = kk * lax.rsqrt(_group_mean_sq(kk, bd_ref[0:KV_WIDTH, 0:KV_WIDTH]) + EPS) * gk_ref[...]
    k_ref[...] = _bf16(_rope_slab(kn, cos, sin_signed, first_half))

    v = jnp.dot(h, w_ref[:, ATT_WIDTH + KV_WIDTH:QKV_WIDTH], preferred_element_type=jnp.float32)
    vt = _bf16(v.T)
    for c in range(TM // TK):
        for hd in range(ATT_KV_HEADS):
            vt_ref[c, hd, 0:HEAD_DIM, :] = vt[hd * HEAD_DIM:(hd + 1) * HEAD_DIM, c * TK:(c + 1) * TK]
            vt_ref[c, hd, HEAD_DIM:V_ROWS, :] = jnp.ones((V_ROWS - HEAD_DIM, TK), jnp.bfloat16)

    for c in range(REST_WIDTH // 512):
        rest_ref[:, c * 512:(c + 1) * 512] = jnp.dot(
            h, w_ref[:, QKV_WIDTH + c * 512:QKV_WIDTH + (c + 1) * 512],
            preferred_element_type=jnp.float32)


def _in_proj(x, g, w, gq, gk, bd, cos_t, sin_t):
    B, S, _ = x.shape
    ns = S // TM
    const = lambda b, s: (0, 0)
    return pl.pallas_call(
        _in_proj_kernel,
        grid=(B, ns),
        in_specs=[
            pl.BlockSpec((None, TM, D_MODEL), lambda b, s: (b, s, 0)),
            pl.BlockSpec((1, D_MODEL), const),
            pl.BlockSpec((D_MODEL, D_IN), const),
            pl.BlockSpec((1, ATT_WIDTH), const),
            pl.BlockSpec((1, KV_WIDTH), const),
            pl.BlockSpec((ATT_WIDTH, ATT_WIDTH), const),
            pl.BlockSpec((TM, LANES), lambda b, s: (s, 0)),
            pl.BlockSpec((TM, LANES), lambda b, s: (s, 0)),
        ],
        out_specs=[
            pl.BlockSpec((None, ATT_WIDTH, TM), lambda b, s: (b, 0, s)),
            pl.BlockSpec((None, TM, KV_WIDTH), lambda b, s: (b, s, 0)),
            pl.BlockSpec((None, TM // TK, ATT_KV_HEADS, V_ROWS, TK), lambda b, s: (b, s, 0, 0, 0)),
            pl.BlockSpec((None, TM, REST_WIDTH), lambda b, s: (b, s, 0)),
        ],
        out_shape=[
            jax.ShapeDtypeStruct((B, ATT_WIDTH, S), jnp.bfloat16),
            jax.ShapeDtypeStruct((B, S, KV_WIDTH), jnp.bfloat16),
            jax.ShapeDtypeStruct((B, S // TK, ATT_KV_HEADS, V_ROWS, TK), jnp.bfloat16),
            jax.ShapeDtypeStruct((B, S, REST_WIDTH), jnp.float32),
        ],
        compiler_params=pltpu.CompilerParams(
            dimension_semantics=("arbitrary", "arbitrary"), vmem_limit_bytes=VMEM_LIMIT),
        name="in_proj",
    )(x, g, w, gq, gk, bd, cos_t, sin_t)


def _attention_kernel(qt_ref, k_ref, vt_ref, o_ref, qext_ref, s_ref, p_ref, acc_ref, m_ref, mx_ref):
    kv = pl.program_id(1)
    n = k_ref.shape[0] // TK
    for g in range(GQA):
        qg = qt_ref[g * HEAD_DIM:(g + 1) * HEAD_DIM, :]
        zero = jnp.zeros_like(qg)
        qext_ref[0:HEAD_DIM, g * TQ:(g + 1) * TQ] = jnp.where(kv == 0, qg, zero)
        qext_ref[HEAD_DIM:2 * HEAD_DIM, g * TQ:(g + 1) * TQ] = jnp.where(kv == 1, qg, zero)

    def scores(j, g, cols):
        kblk = k_ref[pl.ds(pl.multiple_of(j * TK, TK), TK), :]
        s = jnp.dot(kblk, qext_ref[:, cols], preferred_element_type=jnp.float32)
        s_ref[g] = s
        mx_ref[:, cols] = jnp.max(s, axis=0, keepdims=True)

    def values(j, g):
        return jnp.dot(vt_ref[j], p_ref[g], preferred_element_type=jnp.float32)

    groups = [(g, slice(g * LANE_GROUP, (g + 1) * LANE_GROUP)) for g in range(NQ // LANE_GROUP)]

    def stage(j, first=False, last=False):
        for g, cols in groups:
            if first:
                m_new = mx_ref[:, cols]
            else:
                pv = values(j - 1, g)
                m_old = m_ref[:, cols]
                m_new = jnp.maximum(m_old, mx_ref[:, cols])
                acc_ref[:, cols] = jnp.exp2(m_old - m_new) * (acc_ref[:, cols] + pv)
            m_ref[:, cols] = m_new
            p_ref[g] = _bf16(jnp.exp2(s_ref[g] - m_new))
            if not last:
                scores(j + 1, g, cols)

    acc_ref[...] = jnp.zeros_like(acc_ref)
    for g, cols in groups:
        scores(0, g, cols)
    stage(0, first=True)

    def step(j, carry):
        stage(j)
        return carry

    assert (n - 2) % ATT_UNROLL == 0
    lax.fori_loop(1, n - 1, step, 0, unroll=ATT_UNROLL)
    stage(n - 1, last=True)
    for g, cols in groups:
        acc_ref[:, cols] = acc_ref[:, cols] + values(n - 1, g)
    o = acc_ref[0:HEAD_DIM, :] / acc_ref[HEAD_DIM:HEAD_DIM + 1, :]
    o4 = jnp.concatenate([o[:, g * TQ:(g + 1) * TQ] for g in range(GQA)], axis=0)
    o_ref[...] = o4.T


def _attention(qt, k, vt):
    B, _, S = qt.shape
    nk = vt.shape[1]
    assert vt.shape[-1] == TK
    return pl.pallas_call(
        _attention_kernel,
        grid=(B, ATT_KV_HEADS, S // TQ),
        in_specs=[
            pl.BlockSpec((None, GQA * HEAD_DIM, TQ), lambda b, kv, qi: (b, kv, qi)),
            pl.BlockSpec((None, S, KV_WIDTH), lambda b, kv, qi: (b, 0, 0)),
            pl.BlockSpec((None, nk, None, V_ROWS, TK), lambda b, kv, qi: (b, 0, kv, 0, 0)),
        ],
        out_specs=pl.BlockSpec((None, TQ, GQA * HEAD_DIM), lambda b, kv, qi: (b, qi, kv)),
        out_shape=jax.ShapeDtypeStruct((B, S, ATT_WIDTH), jnp.float32),
        scratch_shapes=[
            pltpu.VMEM((KV_WIDTH, NQ), jnp.bfloat16),
            pltpu.VMEM((NQ // LANE_GROUP, TK, LANE_GROUP), jnp.float32),
            pltpu.VMEM((NQ // LANE_GROUP, TK, LANE_GROUP), jnp.bfloat16),
            pltpu.VMEM((V_ROWS, NQ), jnp.float32),
            pltpu.VMEM((1, NQ), jnp.float32),
            pltpu.VMEM((1, NQ), jnp.float32),
        ],
        compiler_params=pltpu.CompilerParams(
            dimension_semantics=("arbitrary", "arbitrary", "arbitrary"),
            vmem_limit_bytes=VMEM_LIMIT),
        name="attention",
    )(qt, k, vt)


def _silu(x):
    return x * jax.nn.sigmoid(x)


def _gelu(x):
    return 0.5 * x * (1.0 + lax.erf(x * (2.0 ** -0.5)))


def _layer_norm(x, g, b):
    mu = jnp.mean(x, axis=-1, keepdims=True)
    xc = x - mu
    var = jnp.mean(xc * xc, axis=-1, keepdims=True)
    return xc * lax.rsqrt(var + EPS) * g + b


def _glu(a):
    return a[:, :CONV_WIDTH] * jax.nn.sigmoid(a[:, CONV_WIDTH:])


CONV_ROWS = 64


def _post_kernel(x_ref, rest_ref, prev_ref, next_ref, att_ref, wout_ref, gpost_ref,
                 dw_ref, dwb_ref, cg_ref, cb_ref, sg_ref, sb_ref, sw_ref, sbias_ref,
                 o_ref, hext_ref, hsh_ref, cat_ref):
    si = pl.program_id(1)
    last = pl.num_programs(1) - 1

    cat_ref[:, 0:ATT_WIDTH] = _bf16(att_ref[...] * _silu(rest_ref[:, R_GATT:R_GATT + ATT_WIDTH]))

    hext_ref[0:HALO, :] = jnp.where(si > 0, _glu(prev_ref[...]), 0.0)
    hext_ref[HALO:HALO + TM, :] = _glu(rest_ref[:, R_ACONV:R_ACONV + 2 * CONV_WIDTH])
    hext_ref[HALO + TM:2 * HALO + TM, :] = jnp.where(si < last, _glu(next_ref[...]), 0.0)
    for r in range(SUBLANES):
        hsh_ref[r] = hext_ref[r:r + TM + 2 * HALO - SUBLANES, :]
    for rb in range(TM // CONV_ROWS):
        acc = jnp.zeros((CONV_ROWS, CONV_WIDTH), jnp.float32) + dwb_ref[...]
        for j in range(CONV_KERNEL):
            r = (j + HALO - CONV_PAD) % SUBLANES
            start = rb * CONV_ROWS + j + HALO - CONV_PAD - r
            acc = acc + hsh_ref[r, start:start + CONV_ROWS, :] * dw_ref[j:j + 1, :]
        rows = slice(rb * CONV_ROWS, (rb + 1) * CONV_ROWS)
        c = _silu(_layer_norm(acc, cg_ref[...], cb_ref[...]))
        c = c * _silu(rest_ref[rows, R_GCONV:R_GCONV + CONV_WIDTH])
        cat_ref[rows, ATT_WIDTH:ATT_WIDTH + CONV_WIDTH] = _bf16(c)

    lane = lax.broadcasted_iota(jnp.int32, (1, SG_WIDTH), 1)
    for c in range(TM // SG_CHUNK):
        rows = slice(c * SG_CHUNK, (c + 1) * SG_CHUNK)
        vln = _bf16(_layer_norm(_gelu(rest_ref[rows, R_V:R_V + SG_WIDTH]), sg_ref[...], sb_ref[...]))
        mixed = sbias_ref[...]
        for hd in range(SG_HEADS):
            mh = jnp.dot(sw_ref[hd], vln, preferred_element_type=jnp.float32)
            mixed = mixed + jnp.where(lane // HEAD_DIM == hd, mh, 0.0)
        sgu = _gelu(rest_ref[rows, R_U:R_U + SG_WIDTH]) * mixed
        sgu = sgu * _silu(rest_ref[rows, R_GSG:R_GSG + SG_WIDTH])
        cat_ref[rows, ATT_WIDTH + CONV_WIDTH:D_MIX] = _bf16(sgu)

    mix = jnp.dot(cat_ref[...], wout_ref[...], preferred_element_type=jnp.float32)
    ms = jnp.mean(mix * mix, axis=-1, keepdims=True)
    o_ref[...] = x_ref[...] + mix * lax.rsqrt(ms + EPS) * gpost_ref[...]


def _post(x, rest, att, wout, gpost, dw, dwb, cg, cb, sg, sb, sw, sbias):
    B, S, _ = x.shape
    ns = S // TM
    hb = TM // HALO
    const2 = lambda b, s: (0, 0)
    const3 = lambda b, s: (0, 0, 0)
    aconv_blk = R_ACONV // (2 * CONV_WIDTH)
    return pl.pallas_call(
        _post_kernel,
        grid=(B, ns),
        in_specs=[
            pl.BlockSpec((None, TM, D_MODEL), lambda b, s: (b, s, 0)),
            pl.BlockSpec((None, TM, REST_WIDTH), lambda b, s: (b, s, 0)),
            pl.BlockSpec((None, HALO, 2 * CONV_WIDTH),
                         lambda b, s: (b, jnp.maximum(s * hb - 1, 0), aconv_blk)),
            pl.BlockSpec((None, HALO, 2 * CONV_WIDTH),
                         lambda b, s: (b, jnp.minimum((s + 1) * hb, S // HALO - 1), aconv_blk)),
            pl.BlockSpec((None, TM, ATT_WIDTH), lambda b, s: (b, s, 0)),
            pl.BlockSpec((D_MIX, D_MODEL), const2),
            pl.BlockSpec((1, D_MODEL), const2),
            pl.BlockSpec((CONV_KERNEL, CONV_WIDTH), const2),
            pl.BlockSpec((1, CONV_WIDTH), const2),
            pl.BlockSpec((1, CONV_WIDTH), const2),
            pl.BlockSpec((1, CONV_WIDTH), const2),
            pl.BlockSpec((1, SG_WIDTH), const2),
            pl.BlockSpec((1, SG_WIDTH), const2),
            pl.BlockSpec((SG_HEADS, SG_CHUNK, SG_CHUNK), const3),
            pl.BlockSpec((SG_CHUNK, SG_WIDTH), const2),
        ],
        out_specs=pl.BlockSpec((None, TM, D_MODEL), lambda b, s: (b, s, 0)),
        out_shape=jax.ShapeDtypeStruct((B, S, D_MODEL), jnp.float32),
        scratch_shapes=[
            pltpu.VMEM((TM + 2 * HALO, CONV_WIDTH), jnp.float32),
            pltpu.VMEM((SUBLANES, TM + 2 * HALO - SUBLANES, CONV_WIDTH), jnp.float32),
            pltpu.VMEM((TM, D_MIX), jnp.bfloat16),
        ],
        compiler_params=pltpu.CompilerParams(
            dimension_semantics=("arbitrary", "arbitrary"), vmem_limit_bytes=VMEM_LIMIT),
        name="post",
    )(x, rest, rest, rest, att, wout, gpost, dw, dwb, cg, cb, sg, sb, sw, sbias)


def _rope_tables(S):
    half = ROPE_AXIS_DIM // 2
    t = jnp.arange(S, dtype=jnp.int32)
    row = (t // GRID_W).astype(jnp.float32)
    col = (t % GRID_W).astype(jnp.float32)
    inv_freq = ROPE_THETA ** (-jnp.arange(half, dtype=jnp.float32) / half)
    ang_r = row[:, None] * inv_freq[None, :]
    ang_c = col[:, None] * inv_freq[None, :]
    cos_h = jnp.concatenate([jnp.cos(ang_r)] * 2 + [jnp.cos(ang_c)] * 2, axis=-1)
    sin_h = jnp.concatenate([-jnp.sin(ang_r), jnp.sin(ang_r), -jnp.sin(ang_c), jnp.sin(ang_c)], axis=-1)
    reps = LANES // HEAD_DIM
    return jnp.tile(cos_h, (1, reps)), jnp.tile(sin_h, (1, reps))


def kernel(x, pre_norm, post_norm, w_in, w_out, q_norm, k_norm, conv_dw, conv_dw_b,
           conv_ln_g, conv_ln_b, sg_ln_g, sg_ln_b, sg_w, sg_b):
    B, S, _ = x.shape
    depth = w_in.shape[0]
    cos_t, sin_t = _rope_tables(S)
    head_id = jnp.arange(ATT_WIDTH, dtype=jnp.int32) // HEAD_DIM
    bd = _bf16(jnp.where(head_id[:, None] == head_id[None, :], 1.0 / HEAD_DIM, 0.0))
    row2 = lambda a: a.reshape(1, -1)
    for l in range(depth):
        qt, k, vt, rest = _in_proj(
            x, row2(pre_norm[l]), _bf16(w_in[l]),
            row2(jnp.tile(q_norm[l], ATT_HEADS)), row2(jnp.tile(k_norm[l], ATT_KV_HEADS)),
            bd, cos_t, sin_t)
        att = _attention(qt, k, vt)
        sbias = jnp.repeat(sg_b[l].T, HEAD_DIM, axis=1)
        x = _post(x, rest, att, _bf16(w_out[l]), row2(post_norm[l]),
                  conv_dw[l], row2(conv_dw_b[l]), row2(conv_ln_g[l]), row2(conv_ln_b[l]),
                  row2(sg_ln_g[l]), row2(sg_ln_b[l]), _bf16(sg_w[l]), sbias)
    return x
```

```python
import math

import jax
import jax.numpy as jnp
from jax import lax
from jax.experimental import pallas as pl
from jax.experimental.pallas import tpu as pltpu

D_MODEL = 1024
HEAD_DIM = 64
GRID_W = 64
EPS = 1e-6
ATT_HEADS = 8
ATT_KV_HEADS = 2
GQA = ATT_HEADS // ATT_KV_HEADS
ATT_WIDTH = ATT_HEADS * HEAD_DIM
KV_WIDTH = ATT_KV_HEADS * HEAD_DIM
ROPE_THETA = 10000.0
ROPE_AXIS_DIM = HEAD_DIM // 2
CONV_WIDTH = 256
CONV_KERNEL = 31
CONV_PAD = CONV_KERNEL // 2
SG_HEADS = 4
SG_WIDTH = SG_HEADS * HEAD_DIM
SG_CHUNK = 128
D_MIX = ATT_WIDTH + CONV_WIDTH + SG_WIDTH
QKV_WIDTH = ATT_WIDTH + 2 * KV_WIDTH
REST_WIDTH = ATT_WIDTH + 3 * CONV_WIDTH + 3 * SG_WIDTH
D_IN = QKV_WIDTH + REST_WIDTH

R_GATT = 0
R_ACONV = ATT_WIDTH
R_GCONV = R_ACONV + 2 * CONV_WIDTH
R_U = R_GCONV + CONV_WIDTH
R_V = R_U + SG_WIDTH
R_GSG = R_V + SG_WIDTH

LANES = 128
SUBLANES = 8
HALO = 16
TM = 512
TQ = 512
TK = 512
ATT_UNROLL = 15
NQ = GQA * TQ
LANE_GROUP = 256
V_ROWS = HEAD_DIM + 16
SCORE_SCALE = HEAD_DIM ** -0.5 * math.log2(math.e)
VMEM_LIMIT = 56 * 1024 * 1024


def _bf16(x):
    return x.astype(jnp.bfloat16)


def _group_mean_sq(x, bd):
    x2 = x * x
    hi = _bf16(x2)
    lo = _bf16(x2 - hi.astype(jnp.float32))
    return (jnp.dot(hi, bd, preferred_element_type=jnp.float32)
            + jnp.dot(lo, bd, preferred_element_type=jnp.float32))


def _rope_slab(xn, cos, sin_signed, first_half):
    fwd = pltpu.roll(xn, LANES - ROPE_AXIS_DIM // 2, 1)
    bwd = pltpu.roll(xn, ROPE_AXIS_DIM // 2, 1)
    partner = jnp.where(first_half, fwd, bwd)
    return xn * cos + partner * sin_signed


def _in_proj_kernel(x_ref, g_ref, w_ref, gq_ref, gk_ref, bd_ref, cos_ref, sin_ref,
                    qt_ref, k_ref, vt_ref, rest_ref):
    x = x_ref[...]
    ms = jnp.mean(x * x, axis=-1, keepdims=True)
    h = _bf16(x * lax.rsqrt(ms + EPS) * g_ref[...])

    cos = cos_ref[...]
    sin_signed = sin_ref[...]
    lane = lax.broadcasted_iota(jnp.int32, (1, LANES), 1)
    first_half = (lane % ROPE_AXIS_DIM) < (ROPE_AXIS_DIM // 2)

    q = jnp.dot(h, w_ref[:, 0:ATT_WIDTH], preferred_element_type=jnp.float32)
    qn = q * lax.rsqrt(_group_mean_sq(q, bd_ref[...]) + EPS) * gq_ref[...]
    for j in range(ATT_WIDTH // LANES):
        sl = slice(j * LANES, (j + 1) * LANES)
        r = _rope_slab(qn[:, sl], cos, sin_signed, first_half) * SCORE_SCALE
        qt_ref[sl, :] = _bf16(r.T)

    kk = jnp.dot(h, w_ref[:, ATT_WIDTH:ATT_WIDTH + KV_WIDTH], preferred_element_type=jnp.float32)
    kn = kk * lax.rsqrt(_group_mean_sq(kk, bd_ref[0:KV_WIDTH, 0:KV_WIDTH]) + EPS) * gk_ref[...]
    k_ref[...] = _bf16(_rope_slab(kn, cos, sin_signed, first_half))

    v = jnp.dot(h, w_ref[:, ATT_WIDTH + KV_WIDTH:QKV_WIDTH], preferred_element_type=jnp.float32)
    vt = _bf16(v.T)
    for c in range(TM // TK):
        for hd in range(ATT_KV_HEADS):
            vt_ref[c, hd, 0:HEAD_DIM, :] = vt[hd * HEAD_DIM:(hd + 1) * HEAD_DIM, c * TK:(c + 1) * TK]
            vt_ref[c, hd, HEAD_DIM:V_ROWS, :] = jnp.ones((V_ROWS - HEAD_DIM, TK), jnp.bfloat16)

    for c in range(REST_WIDTH // 512):
        rest_ref[:, c * 512:(c + 1) * 512] = jnp.dot(
            h, w_ref[:, QKV_WIDTH + c * 512:QKV_WIDTH + (c + 1) * 512],
            preferred_element_type=jnp.float32)


def _in_proj(x, g, w, gq, gk, bd, cos_t, sin_t):
    B, S, _ = x.shape
    ns = S // TM
    const = lambda b, s: (0, 0)
    return pl.pallas_call(
        _in_proj_kernel,
        grid=(B, ns),
        in_specs=[
            pl.BlockSpec((None, TM, D_MODEL), lambda b, s: (b, s, 0)),
            pl.BlockSpec((1, D_MODEL), const),
            pl.BlockSpec((D_MODEL, D_IN), const),
            pl.BlockSpec((1, ATT_WIDTH), const),
            pl.BlockSpec((1, KV_WIDTH), const),
            pl.BlockSpec((ATT_WIDTH, ATT_WIDTH), const),
            pl.BlockSpec((TM, LANES), lambda b, s: (s, 0)),
            pl.BlockSpec((TM, LANES), lambda b, s: (s, 0)),
        ],
        out_specs=[
            pl.BlockSpec((None, ATT_WIDTH, TM), lambda b, s: (b, 0, s)),
            pl.BlockSpec((None, TM, KV_WIDTH), lambda b, s: (b, s, 0)),
            pl.BlockSpec((None, TM // TK, ATT_KV_HEADS, V_ROWS, TK), lambda b, s: (b, s, 0, 0, 0)),
            pl.BlockSpec((None, TM, REST_WIDTH), lambda b, s: (b, s, 0)),
        ],
        out_shape=[
            jax.ShapeDtypeStruct((B, ATT_WIDTH, S), jnp.bfloat16),
            jax.ShapeDtypeStruct((B, S, KV_WIDTH), jnp.bfloat16),
            jax.ShapeDtypeStruct((B, S // TK, ATT_KV_HEADS, V_ROWS, TK), jnp.bfloat16),
            jax.ShapeDtypeStruct((B, S, REST_WIDTH), jnp.float32),
        ],
        compiler_params=pltpu.CompilerParams(
            dimension_semantics=("arbitrary", "arbitrary"), vmem_limit_bytes=VMEM_LIMIT),
        name="in_proj",
    )(x, g, w, gq, gk, bd, cos_t, sin_t)


def _attention_kernel(qt_ref, k_ref, vt_ref, o_ref, qext_ref, s_ref, p_ref, acc_ref, m_ref, mx_ref):
    kv = pl.program_id(1)
    n = k_ref.shape[0] // TK
    for g in range(GQA):
        qg = qt_ref[g * HEAD_DIM:(g + 1) * HEAD_DIM, :]
        zero = jnp.zeros_like(qg)
        qext_ref[0:HEAD_DIM, g * TQ:(g + 1) * TQ] = jnp.where(kv == 0, qg, zero)
        qext_ref[HEAD_DIM:2 * HEAD_DIM, g * TQ:(g + 1) * TQ] = jnp.where(kv == 1, qg, zero)

    def scores(j, g, cols):
        kblk = k_ref[pl.ds(pl.multiple_of(j * TK, TK), TK), :]
        s = jnp.dot(kblk, qext_ref[:, cols], preferred_element_type=jnp.float32)
        s_ref[g] = s
        mx_ref[:, cols] = jnp.max(s, axis=0, keepdims=True)

    def values(j, g):
        return jnp.dot(vt_ref[j], p_ref[g], preferred_element_type=jnp.float32)

    groups = [(g, slice(g * LANE_GROUP, (g + 1) * LANE_GROUP)) for g in range(NQ // LANE_GROUP)]

    def stage(j, first=False, last=False):
        for g, cols in groups:
            if first:
                m_new = mx_ref[:, cols]
            else:
                pv = values(j - 1, g)
                m_old = m_ref[:, cols]
                m_new = jnp.maximum(m_old, mx_ref[:, cols])
                acc_ref[:, cols] = jnp.exp2(m_old - m_new) * (acc_ref[:, cols] + pv)
            m_ref[:, cols] = m_new
            p_ref[g] = _bf16(jnp.exp2(s_ref[g] - m_new))
            if not last:
                scores(j + 1, g, cols)

    acc_ref[...] = jnp.zeros_like(acc_ref)
    for g, cols in groups:
        scores(0, g, cols)
    stage(0, first=True)

    def step(j, carry):
        stage(j)
        return carry

    assert (n - 2) % ATT_UNROLL == 0
    lax.fori_loop(1, n - 1, step, 0, unroll=ATT_UNROLL)
    stage(n - 1, last=True)
    for g, cols in groups:
        acc_ref[:, cols] = acc_ref[:, cols] + values(n - 1, g)
    o = acc_ref[0:HEAD_DIM, :] / acc_ref[HEAD_DIM:HEAD_DIM + 1, :]
    o4 = jnp.concatenate([o[:, g * TQ:(g + 1) * TQ] for g in range(GQA)], axis=0)
    o_ref[...] = o4.T


def _attention(qt, k, vt):
    B, _, S = qt.shape
    nk = vt.shape[1]
    assert vt.shape[-1] == TK
    return pl.pallas_call(
        _attention_kernel,
        grid=(B, ATT_KV_HEADS, S // TQ),
        in_specs=[
            pl.BlockSpec((None, GQA * HEAD_DIM, TQ), lambda b, kv, qi: (b, kv, qi)),
            pl.BlockSpec((None, S, KV_WIDTH), lambda b, kv, qi: (b, 0, 0)),
            pl.BlockSpec((None, nk, None, V_ROWS, TK), lambda b, kv, qi: (b, 0, kv, 0, 0)),
        ],
        out_specs=pl.BlockSpec((None, TQ, GQA * HEAD_DIM), lambda b, kv, qi: (b, qi, kv)),
        out_shape=jax.ShapeDtypeStruct((B, S, ATT_WIDTH), jnp.float32),
        scratch_shapes=[
            pltpu.VMEM((KV_WIDTH, NQ), jnp.bfloat16),
            pltpu.VMEM((NQ // LANE_GROUP, TK, LANE_GROUP), jnp.float32),
            pltpu.VMEM((NQ // LANE_GROUP, TK, LANE_GROUP), jnp.bfloat16),
            pltpu.VMEM((V_ROWS, NQ), jnp.float32),
            pltpu.VMEM((1, NQ), jnp.float32),
            pltpu.VMEM((1, NQ), jnp.float32),
        ],
        compiler_params=pltpu.CompilerParams(
            dimension_semantics=("arbitrary", "arbitrary", "arbitrary"),
            vmem_limit_bytes=VMEM_LIMIT),
        name="attention",
    )(qt, k, vt)


def _silu(x):
    return x * jax.nn.sigmoid(x)


def _gelu(x):
    return 0.5 * x * (1.0 + lax.erf(x * (2.0 ** -0.5)))


def _layer_norm(x, g, b):
    mu = jnp.mean(x, axis=-1, keepdims=True)
    xc = x - mu
    var = jnp.mean(xc * xc, axis=-1, keepdims=True)
    return xc * lax.rsqrt(var + EPS) * g + b


def _glu(a):
    return a[:, :CONV_WIDTH] * jax.nn.sigmoid(a[:, CONV_WIDTH:])


CONV_ROWS = 64


def _post_kernel(x_ref, rest_ref, prev_ref, next_ref, att_ref, wout_ref, gpost_ref,
                 dw_ref, dwb_ref, cg_ref, cb_ref, sg_ref, sb_ref, sw_ref, sbias_ref,
                 o_ref, hext_ref, hsh_ref, cat_ref):
    si = pl.program_id(1)
    last = pl.num_programs(1) - 1

    cat_ref[:, 0:ATT_WIDTH] = _bf16(att_ref[...] * _silu(rest_ref[:, R_GATT:R_GATT + ATT_WIDTH]))

    hext_ref[0:HALO, :] = jnp.where(si > 0, _glu(prev_ref[...]), 0.0)
    hext_ref[HALO:HALO + TM, :] = _glu(rest_ref[:, R_ACONV:R_ACONV + 2 * CONV_WIDTH])
    hext_ref[HALO + TM:2 * HALO + TM, :] = jnp.where(si < last, _glu(next_ref[...]), 0.0)
    for r in range(SUBLANES):
        hsh_ref[r] = hext_ref[r:r + TM + 2 * HALO - SUBLANES, :]
    for rb in range(TM // CONV_ROWS):
        acc = jnp.zeros((CONV_ROWS, CONV_WIDTH), jnp.float32) + dwb_ref[...]
        for j in range(CONV_KERNEL):
            r = (j + HALO - CONV_PAD) % SUBLANES
            start = rb * CONV_ROWS + j + HALO - CONV_PAD - r
            acc = acc + hsh_ref[r, start:start + CONV_ROWS, :] * dw_ref[j:j + 1, :]
        rows = slice(rb * CONV_ROWS, (rb + 1) * CONV_ROWS)
        c = _silu(_layer_norm(acc, cg_ref[...], cb_ref[...]))
        c = c * _silu(rest_ref[rows, R_GCONV:R_GCONV + CONV_WIDTH])
        cat_ref[rows, ATT_WIDTH:ATT_WIDTH + CONV_WIDTH] = _bf16(c)

    lane = lax.broadcasted_iota(jnp.int32, (1, SG_WIDTH), 1)
    for c in range(TM // SG_CHUNK):
        rows = slice(c * SG_CHUNK, (c + 1) * SG_CHUNK)
        vln = _bf16(_layer_norm(_gelu(rest_ref[rows, R_V:R_V + SG_WIDTH]), sg_ref[...], sb_ref[...]))
        mixed = sbias_ref[...]
        for hd in range(SG_HEADS):
            mh = jnp.dot(sw_ref[hd], vln, preferred_element_type=jnp.float32)
            mixed = mixed + jnp.where(lane // HEAD_DIM == hd, mh, 0.0)
        sgu = _gelu(rest_ref[rows, R_U:R_U + SG_WIDTH]) * mixed
        sgu = sgu * _silu(rest_ref[rows, R_GSG:R_GSG + SG_WIDTH])
        cat_ref[rows, ATT_WIDTH + CONV_WIDTH:D_MIX] = _bf16(sgu)

    mix = jnp.dot(cat_ref[...], wout_ref[...], preferred_element_type=jnp.float32)
    ms = jnp.mean(mix * mix, axis=-1, keepdims=True)
    o_ref[...] = x_ref[...] + mix * lax.rsqrt(ms + EPS) * gpost_ref[...]


def _post(x, rest, att, wout, gpost, dw, dwb, cg, cb, sg, sb, sw, sbias):
    B, S, _ = x.shape
    ns = S // TM
    hb = TM // HALO
    const2 = lambda b, s: (0, 0)
    const3 = lambda b, s: (0, 0, 0)
    aconv_blk = R_ACONV // (2 * CONV_WIDTH)
    return pl.pallas_call(
        _post_kernel,
        grid=(B, ns),
        in_specs=[
            pl.BlockSpec((None, TM, D_MODEL), lambda b, s: (b, s, 0)),
            pl.BlockSpec((None, TM, REST_WIDTH), lambda b, s: (b, s, 0)),
            pl.BlockSpec((None, HALO, 2 * CONV_WIDTH),
                         lambda b, s: (b, jnp.maximum(s * hb - 1, 0), aconv_blk)),
            pl.BlockSpec((None, HALO, 2 * CONV_WIDTH),
                         lambda b, s: (b, jnp.minimum((s + 1) * hb, S // HALO - 1), aconv_blk)),
            pl.BlockSpec((None, TM, ATT_WIDTH), lambda b, s: (b, s, 0)),
            pl.BlockSpec((D_MIX, D_MODEL), const2),
            pl.BlockSpec((1, D_MODEL), const2),
            pl.BlockSpec((CONV_KERNEL, CONV_WIDTH), const2),
            pl.BlockSpec((1, CONV_WIDTH), const2),
            pl.BlockSpec((1, CONV_WIDTH), const2),
            pl.BlockSpec((1, CONV_WIDTH), const2),
            pl.BlockSpec((1, SG_WIDTH), const2),
            pl.BlockSpec((1, SG_WIDTH), const2),
            pl.BlockSpec((SG_HEADS, SG_CHUNK, SG_CHUNK), const3),
            pl.BlockSpec((SG_CHUNK, SG_WIDTH), const2),
        ],
        out_specs=pl.BlockSpec((None, TM, D_MODEL), lambda b, s: (b, s, 0)),
        out_shape=jax.ShapeDtypeStruct((B, S, D_MODEL), jnp.float32),
        scratch_shapes=[
            pltpu.VMEM((TM + 2 * HALO, CONV_WIDTH), jnp.float32),
            pltpu.VMEM((SUBLANES, TM + 2 * HALO - SUBLANES, CONV_WIDTH), jnp.float32),
            pltpu.VMEM((TM, D_MIX), jnp.bfloat16),
        ],
        compiler_params=pltpu.CompilerParams(
            dimension_semantics=("arbitrary", "arbitrary"), vmem_limit_bytes=VMEM_LIMIT),
        name="post",
    )(x, rest, rest, rest, att, wout, gpost, dw, dwb, cg, cb, sg, sb, sw, sbias)


def _rope_tables(S):
    half = ROPE_AXIS_DIM // 2
    t = jnp.arange(S, dtype=jnp.int32)
    row = (t // GRID_W).astype(jnp.float32)
    col = (t % GRID_W).astype(jnp.float32)
    inv_freq = ROPE_THETA ** (-jnp.arange(half, dtype=jnp.float32) / half)
    ang_r = row[:, None] * inv_freq[None, :]
    ang_c = col[:, None] * inv_freq[None, :]
    cos_h = jnp.concatenate([jnp.cos(ang_r)] * 2 + [jnp.cos(ang_c)] * 2, axis=-1)
    sin_h = jnp.concatenate([-jnp.sin(ang_r), jnp.sin(ang_r), -jnp.sin(ang_c), jnp.sin(ang_c)], axis=-1)
    reps = LANES // HEAD_DIM
    return jnp.tile(cos_h, (1, reps)), jnp.tile(sin_h, (1, reps))


def kernel(x, pre_norm, post_norm, w_in, w_out, q_norm, k_norm, conv_dw, conv_dw_b,
           conv_ln_g, conv_ln_b, sg_ln_g, sg_ln_b, sg_w, sg_b):
    B, S, _ = x.shape
    depth = w_in.shape[0]
    cos_t, sin_t = _rope_tables(S)
    head_id = jnp.arange(ATT_WIDTH, dtype=jnp.int32) // HEAD_DIM
    bd = _bf16(jnp.where(head_id[:, None] == head_id[None, :], 1.0 / HEAD_DIM, 0.0))
    row2 = lambda a: a.reshape(1, -1)
    for l in range(depth):
        qt, k, vt, rest = _in_proj(
            x, row2(pre_norm[l]), _bf16(w_in[l]),
            row2(jnp.tile(q_norm[l], ATT_HEADS)), row2(jnp.tile(k_norm[l], ATT_KV_HEADS)),
            bd, cos_t, sin_t)
        att = _attention(qt, k, vt)
        sbias = jnp.repeat(sg_b[l].T, HEAD_DIM, axis=1)
        x = _post(x, rest, att, _bf16(w_out[l]), row2(post_norm[l]),
                  conv_dw[l], row2(conv_dw_b[l]), row2(conv_ln_g[l]), row2(conv_ln_b[l]),
                  row2(sg_ln_g[l]), row2(sg_ln_b[l]), _bf16(sg_w[l]), sbias)
    return x
```

```python
import math

import jax
import jax.numpy as jnp
from jax import lax
from jax.experimental import pallas as pl
from jax.experimental.pallas import tpu as pltpu

D_MODEL = 1024
HEAD_DIM = 64
GRID_W = 64
EPS = 1e-6
ATT_HEADS = 8
ATT_KV_HEADS = 2
GQA = ATT_HEADS // ATT_KV_HEADS
ATT_WIDTH = ATT_HEADS * HEAD_DIM
KV_WIDTH = ATT_KV_HEADS * HEAD_DIM
ROPE_THETA = 10000.0
ROPE_AXIS_DIM = HEAD_DIM // 2
CONV_WIDTH = 256
CONV_KERNEL = 31
CONV_PAD = CONV_KERNEL // 2
SG_HEADS = 4
SG_WIDTH = SG_HEADS * HEAD_DIM
SG_CHUNK = 128
D_MIX = ATT_WIDTH + CONV_WIDTH + SG_WIDTH
QKV_WIDTH = ATT_WIDTH + 2 * KV_WIDTH
REST_WIDTH = ATT_WIDTH + 3 * CONV_WIDTH + 3 * SG_WIDTH
D_IN = QKV_WIDTH + REST_WIDTH

R_GATT = 0
R_ACONV = ATT_WIDTH
R_GCONV = R_ACONV + 2 * CONV_WIDTH
R_U = R_GCONV + CONV_WIDTH
R_V = R_U + SG_WIDTH
R_GSG = R_V + SG_WIDTH

LANES = 128
SUBLANES = 8
HALO = 16
TM = 512
TQ = 512
TK = 512
ATT_UNROLL = 15
NQ = GQA * TQ
LANE_GROUP = 256
V_ROWS = HEAD_DIM + 16
SCORE_SCALE = HEAD_DIM ** -0.5 * math.log2(math.e)
VMEM_LIMIT = 56 * 1024 * 1024


def _bf16(x):
    return x.astype(jnp.bfloat16)


def _group_mean_sq(x, bd):
    x2 = x * x
    hi = _bf16(x2)
    lo = _bf16(x2 - hi.astype(jnp.float32))
    return (jnp.dot(hi, bd, preferred_element_type=jnp.float32)
            + jnp.dot(lo, bd, preferred_element_type=jnp.float32))


def _rope_slab(xn, cos, sin_signed, first_half):
    fwd = pltpu.roll(xn, LANES - ROPE_AXIS_DIM // 2, 1)
    bwd = pltpu.roll(xn, ROPE_AXIS_DIM // 2, 1)
    partner = jnp.where(first_half, fwd, bwd)
    return xn * cos + partner * sin_signed


def _in_proj_kernel(x_ref, g_ref, w_ref, gq_ref, gk_ref, bd_ref, cos_ref, sin_ref, cost_ref, sint_ref,
                    qt_ref, k_ref, vt_ref, rest_ref):
    x = x_ref[...]
    ms = jnp.mean(x * x, axis=-1, keepdims=True)
    h = _bf16(x * lax.rsqrt(ms + EPS) * g_ref[...])

    q = jnp.dot(h, w_ref[:, 0:ATT_WIDTH], preferred_element_type=jnp.float32)
    cost = cost_ref[...]
    sint = sint_ref[...]
    gq = jnp.concatenate([gq_ref[...]] * (TM // LANES), axis=1)
    quarter = ROPE_AXIS_DIM // 2
    for j in range(ATT_WIDTH // LANES):
        qt = q[:, j * LANES:(j + 1) * LANES].T
        for hh in range(LANES // HEAD_DIM):
            xh = qt[hh * HEAD_DIM:(hh + 1) * HEAD_DIM, :]
            xn = xh * lax.rsqrt(jnp.mean(xh * xh, axis=0, keepdims=True) + EPS) * gq
            partner = jnp.concatenate([xn[quarter:2 * quarter], xn[0:quarter],
                                       xn[3 * quarter:4 * quarter], xn[2 * quarter:3 * quarter]], axis=0)
            row0 = j * LANES + hh * HEAD_DIM
            qt_ref[row0:row0 + HEAD_DIM, :] = _bf16((xn * cost + partner * sint) * SCORE_SCALE)

    lane = lax.broadcasted_iota(jnp.int32, (1, LANES), 1)
    first_half = (lane % ROPE_AXIS_DIM) < (ROPE_AXIS_DIM // 2)
    kv = jnp.dot(h, w_ref[:, ATT_WIDTH:QKV_WIDTH], preferred_element_type=jnp.float32)
    kk = kv[:, 0:KV_WIDTH]
    kn = kk * lax.rsqrt(_group_mean_sq(kk, bd_ref[...]) + EPS) * gk_ref[...]
    k_ref[...] = _bf16(_rope_slab(kn, cos_ref[...], sin_ref[...], first_half))

    vt = _bf16(kv[:, KV_WIDTH:2 * KV_WIDTH].T)
    for c in range(TM // TK):
        for hd in range(ATT_KV_HEADS):
            vt_ref[c, hd, 0:HEAD_DIM, :] = vt[hd * HEAD_DIM:(hd + 1) * HEAD_DIM, c * TK:(c + 1) * TK]
            vt_ref[c, hd, HEAD_DIM:V_ROWS, :] = jnp.ones((V_ROWS - HEAD_DIM, TK), jnp.bfloat16)

    rest_ref[...] = jnp.dot(h, w_ref[:, QKV_WIDTH:D_IN], preferred_element_type=jnp.float32)


def _in_proj(x, g, w, gq, gk, bd, cos_t, sin_t, cos_tt, sin_tt):
    B, S, _ = x.shape
    ns = S // TM
    const = lambda b, s: (0, 0)
    return pl.pallas_call(
        _in_proj_kernel,
        grid=(B, ns),
        in_specs=[
            pl.BlockSpec((None, TM, D_MODEL), lambda b, s: (b, s, 0)),
            pl.BlockSpec((1, D_MODEL), const),
            pl.BlockSpec((D_MODEL, D_IN), const),
            pl.BlockSpec((HEAD_DIM, LANES), const),
            pl.BlockSpec((1, KV_WIDTH), const),
            pl.BlockSpec((KV_WIDTH, KV_WIDTH), const),
            pl.BlockSpec((TM, LANES), lambda b, s: (s, 0)),
            pl.BlockSpec((TM, LANES), lambda b, s: (s, 0)),
            pl.BlockSpec((HEAD_DIM, TM), lambda b, s: (0, s)),
            pl.BlockSpec((HEAD_DIM, TM), lambda b, s: (0, s)),
        ],
        out_specs=[
            pl.BlockSpec((None, ATT_WIDTH, TM), lambda b, s: (b, 0, s)),
            pl.BlockSpec((None, TM, KV_WIDTH), lambda b, s: (b, s, 0)),
            pl.BlockSpec((None, TM // TK, ATT_KV_HEADS, V_ROWS, TK), lambda b, s: (b, s, 0, 0, 0)),
            pl.BlockSpec((None, TM, REST_WIDTH), lambda b, s: (b, s, 0)),
        ],
        out_shape=[
            jax.ShapeDtypeStruct((B, ATT_WIDTH, S), jnp.bfloat16),
            jax.ShapeDtypeStruct((B, S, KV_WIDTH), jnp.bfloat16),
            jax.ShapeDtypeStruct((B, S // TK, ATT_KV_HEADS, V_ROWS, TK), jnp.bfloat16),
            jax.ShapeDtypeStruct((B, S, REST_WIDTH), jnp.float32),
        ],
        compiler_params=pltpu.CompilerParams(
            dimension_semantics=("arbitrary", "arbitrary"), vmem_limit_bytes=VMEM_LIMIT),
        name="in_proj",
    )(x, g, w, gq, gk, bd, cos_t, sin_t, cos_tt, sin_tt)


def _attention_kernel(qt_ref, k_ref, vt_ref, o_ref, qext_ref, s_ref, p_ref, acc_ref, m_ref, mx_ref):
    kv = pl.program_id(1)
    n = k_ref.shape[0] // TK
    for g in range(GQA):
        qg = qt_ref[g * HEAD_DIM:(g + 1) * HEAD_DIM, :]
        zero = jnp.zeros_like(qg)
        qext_ref[0:HEAD_DIM, g * TQ:(g + 1) * TQ] = jnp.where(kv == 0, qg, zero)
        qext_ref[HEAD_DIM:2 * HEAD_DIM, g * TQ:(g + 1) * TQ] = jnp.where(kv == 1, qg, zero)

    def scores(j, g, cols):
        kblk = k_ref[pl.ds(pl.multiple_of(j * TK, TK), TK), :]
        s = jnp.dot(kblk, qext_ref[:, cols], preferred_element_type=jnp.float32)
        s_ref[g] = s
        mx_ref[:, cols] = jnp.max(s, axis=0, keepdims=True)

    def values(j, g):
        return jnp.dot(vt_ref[j], p_ref[g], preferred_element_type=jnp.float32)

    groups = [(g, slice(g * LANE_GROUP, (g + 1) * LANE_GROUP)) for g in range(NQ // LANE_GROUP)]

    def stage(j, first=False, last=False):
        for g, cols in groups:
            if first:
                m_new = mx_ref[:, cols]
            else:
                pv = values(j - 1, g)
                m_old = m_ref[:, cols]
                m_new = jnp.maximum(m_old, mx_ref[:, cols])
                acc_ref[:, cols] = jnp.exp2(m_old - m_new) * (acc_ref[:, cols] + pv)
            m_ref[:, cols] = m_new
            p_ref[g] = _bf16(jnp.exp2(s_ref[g] - m_new))
            if not last:
                scores(j + 1, g, cols)

    acc_ref[...] = jnp.zeros_like(acc_ref)
    for g, cols in groups:
        scores(0, g, cols)
    stage(0, first=True)

    def step(j, carry):
        stage(j)
        return carry

    assert (n - 2) % ATT_UNROLL == 0
    lax.fori_loop(1, n - 1, step, 0, unroll=ATT_UNROLL)
    stage(n - 1, last=True)
    for g, cols in groups:
        acc_ref[:, cols] = acc_ref[:, cols] + values(n - 1, g)
    o = acc_ref[0:HEAD_DIM, :] / acc_ref[HEAD_DIM:HEAD_DIM + 1, :]
    o4 = jnp.concatenate([o[:, g * TQ:(g + 1) * TQ] for g in range(GQA)], axis=0)
    o_ref[...] = o4.T


def _attention(qt, k, vt):
    B, _, S = qt.shape
    nk = vt.shape[1]
    assert vt.shape[-1] == TK
    return pl.pallas_call(
        _attention_kernel,
        grid=(B, ATT_KV_HEADS, S // TQ),
        in_specs=[
            pl.BlockSpec((None, GQA * HEAD_DIM, TQ), lambda b, kv, qi: (b, kv, qi)),
            pl.BlockSpec((None, S, KV_WIDTH), lambda b, kv, qi: (b, 0, 0)),
            pl.BlockSpec((None, nk, None, V_ROWS, TK), lambda b, kv, qi: (b, 0, kv, 0, 0)),
        ],
        out_specs=pl.BlockSpec((None, TQ, GQA * HEAD_DIM), lambda b, kv, qi: (b, qi, kv)),
        out_shape=jax.ShapeDtypeStruct((B, S, ATT_WIDTH), jnp.float32),
        scratch_shapes=[
            pltpu.VMEM((KV_WIDTH, NQ), jnp.bfloat16),
            pltpu.VMEM((NQ // LANE_GROUP, TK, LANE_GROUP), jnp.float32),
            pltpu.VMEM((NQ // LANE_GROUP, TK, LANE_GROUP), jnp.bfloat16),
            pltpu.VMEM((V_ROWS, NQ), jnp.float32),
            pltpu.VMEM((1, NQ), jnp.float32),
            pltpu.VMEM((1, NQ), jnp.float32),
        ],
        compiler_params=pltpu.CompilerParams(
            dimension_semantics=("arbitrary", "arbitrary", "arbitrary"),
            vmem_limit_bytes=VMEM_LIMIT),
        name="attention",
    )(qt, k, vt)


def _silu(x):
    return x * jax.nn.sigmoid(x)


def _gelu(x):
    return 0.5 * x * (1.0 + lax.erf(x * (2.0 ** -0.5)))


def _layer_norm(x, g, b):
    mu = jnp.mean(x, axis=-1, keepdims=True)
    xc = x - mu
    var = jnp.mean(xc * xc, axis=-1, keepdims=True)
    return xc * lax.rsqrt(var + EPS) * g + b


def _glu(a):
    return a[:, :CONV_WIDTH] * jax.nn.sigmoid(a[:, CONV_WIDTH:])


CONV_ROWS = 64


def _post_kernel(x_ref, rest_ref, prev_ref, next_ref, att_ref, wout_ref, gpost_ref,
                 dw_ref, dwb_ref, cg_ref, cb_ref, sg_ref, sb_ref, sw_ref, sbias_ref,
                 o_ref, hext_ref, hsh_ref, cat_ref):
    si = pl.program_id(1)
    last = pl.num_programs(1) - 1

    cat_ref[:, 0:ATT_WIDTH] = _bf16(att_ref[...] * _silu(rest_ref[:, R_GATT:R_GATT + ATT_WIDTH]))

    hext_ref[0:HALO, :] = jnp.where(si > 0, _glu(prev_ref[...]), 0.0)
    hext_ref[HALO:HALO + TM, :] = _glu(rest_ref[:, R_ACONV:R_ACONV + 2 * CONV_WIDTH])
    hext_ref[HALO + TM:2 * HALO + TM, :] = jnp.where(si < last, _glu(next_ref[...]), 0.0)
    for r in range(SUBLANES):
        hsh_ref[r] = hext_ref[r:r + TM + 2 * HALO - SUBLANES, :]
    for rb in range(TM // CONV_ROWS):
        acc = jnp.zeros((CONV_ROWS, CONV_WIDTH), jnp.float32) + dwb_ref[...]
        for j in range(CONV_KERNEL):
            r = (j + HALO - CONV_PAD) % SUBLANES
            start = rb * CONV_ROWS + j + HALO - CONV_PAD - r
            acc = acc + hsh_ref[r, start:start + CONV_ROWS, :] * dw_ref[j:j + 1, :]
        rows = slice(rb * CONV_ROWS, (rb + 1) * CONV_ROWS)
        c = _silu(_layer_norm(acc, cg_ref[...], cb_ref[...]))
        c = c * _silu(rest_ref[rows, R_GCONV:R_GCONV + CONV_WIDTH])
        cat_ref[rows, ATT_WIDTH:ATT_WIDTH + CONV_WIDTH] = _bf16(c)

    lane = lax.broadcasted_iota(jnp.int32, (1, SG_WIDTH), 1)
    for c in range(TM // SG_CHUNK):
        rows = slice(c * SG_CHUNK, (c + 1) * SG_CHUNK)
        vln = _bf16(_layer_norm(_gelu(rest_ref[rows, R_V:R_V + SG_WIDTH]), sg_ref[...], sb_ref[...]))
        mixed = sbias_ref[...]
        for hd in range(SG_HEADS):
            mh = jnp.dot(sw_ref[hd], vln, preferred_element_type=jnp.float32)
            mixed = mixed + jnp.where(lane // HEAD_DIM == hd, mh, 0.0)
        sgu = _gelu(rest_ref[rows, R_U:R_U + SG_WIDTH]) * mixed
        sgu = sgu * _silu(rest_ref[rows, R_GSG:R_GSG + SG_WIDTH])
        cat_ref[rows, ATT_WIDTH + CONV_WIDTH:D_MIX] = _bf16(sgu)

    mix = jnp.dot(cat_ref[...], wout_ref[...], preferred_element_type=jnp.float32)
    ms = jnp.mean(mix * mix, axis=-1, keepdims=True)
    o_ref[...] = x_ref[...] + mix * lax.rsqrt(ms + EPS) * gpost_ref[...]


def _post(x, rest, att, wout, gpost, dw, dwb, cg, cb, sg, sb, sw, sbias):
    B, S, _ = x.shape
    ns = S // TM
    hb = TM // HALO
    const2 = lambda b, s: (0, 0)
    const3 = lambda b, s: (0, 0, 0)
    aconv_blk = R_ACONV // (2 * CONV_WIDTH)
    return pl.pallas_call(
        _post_kernel,
        grid=(B, ns),
        in_specs=[
            pl.BlockSpec((None, TM, D_MODEL), lambda b, s: (b, s, 0)),
            pl.BlockSpec((None, TM, REST_WIDTH), lambda b, s: (b, s, 0)),
            pl.BlockSpec((None, HALO, 2 * CONV_WIDTH),
                         lambda b, s: (b, jnp.maximum(s * hb - 1, 0), aconv_blk)),
            pl.BlockSpec((None, HALO, 2 * CONV_WIDTH),
                         lambda b, s: (b, jnp.minimum((s + 1) * hb, S // HALO - 1), aconv_blk)),
            pl.BlockSpec((None, TM, ATT_WIDTH), lambda b, s: (b, s, 0)),
            pl.BlockSpec((D_MIX, D_MODEL), const2),
            pl.BlockSpec((1, D_MODEL), const2),
            pl.BlockSpec((CONV_KERNEL, CONV_WIDTH), const2),
            pl.BlockSpec((1, CONV_WIDTH), const2),
            pl.BlockSpec((1, CONV_WIDTH), const2),
            pl.BlockSpec((1, CONV_WIDTH), const2),
            pl.BlockSpec((1, SG_WIDTH), const2),
            pl.BlockSpec((1, SG_WIDTH), const2),
            pl.BlockSpec((SG_HEADS, SG_CHUNK, SG_CHUNK), const3),
            pl.BlockSpec((SG_CHUNK, SG_WIDTH), const2),
        ],
        out_specs=pl.BlockSpec((None, TM, D_MODEL), lambda b, s: (b, s, 0)),
        out_shape=jax.ShapeDtypeStruct((B, S, D_MODEL), jnp.float32),
        scratch_shapes=[
            pltpu.VMEM((TM + 2 * HALO, CONV_WIDTH), jnp.float32),
            pltpu.VMEM((SUBLANES, TM + 2 * HALO - SUBLANES, CONV_WIDTH), jnp.float32),
            pltpu.VMEM((TM, D_MIX), jnp.bfloat16),
        ],
        compiler_params=pltpu.CompilerParams(
            dimension_semantics=("arbitrary", "arbitrary"), vmem_limit_bytes=VMEM_LIMIT),
        name="post",
    )(x, rest, rest, rest, att, wout, gpost, dw, dwb, cg, cb, sg, sb, sw, sbias)


def _rope_tables(S):
    half = ROPE_AXIS_DIM // 2
    t = jnp.arange(S, dtype=jnp.int32)
    row = (t // GRID_W).astype(jnp.float32)
    col = (t % GRID_W).astype(jnp.float32)
    inv_freq = ROPE_THETA ** (-jnp.arange(half, dtype=jnp.float32) / half)
    ang_r = row[:, None] * inv_freq[None, :]
    ang_c = col[:, None] * inv_freq[None, :]
    cos_h = jnp.concatenate([jnp.cos(ang_r)] * 2 + [jnp.cos(ang_c)] * 2, axis=-1)
    sin_h = jnp.concatenate([-jnp.sin(ang_r), jnp.sin(ang_r), -jnp.sin(ang_c), jnp.sin(ang_c)], axis=-1)
    reps = LANES // HEAD_DIM
    return jnp.tile(cos_h, (1, reps)), jnp.tile(sin_h, (1, reps)), cos_h.T, sin_h.T


def kernel(x, pre_norm, post_norm, w_in, w_out, q_norm, k_norm, conv_dw, conv_dw_b,
           conv_ln_g, conv_ln_b, sg_ln_g, sg_ln_b, sg_w, sg_b):
    B, S, _ = x.shape
    depth = w_in.shape[0]
    cos_t, sin_t, cos_tt, sin_tt = _rope_tables(S)
    head_id = jnp.arange(KV_WIDTH, dtype=jnp.int32) // HEAD_DIM
    bd = _bf16(jnp.where(head_id[:, None] == head_id[None, :], 1.0 / HEAD_DIM, 0.0))
    row2 = lambda a: a.reshape(1, -1)
    for l in range(depth):
        qt, k, vt, rest = _in_proj(
            x, row2(pre_norm[l]), _bf16(w_in[l]),
            jnp.broadcast_to(q_norm[l][:, None], (HEAD_DIM, LANES)), row2(jnp.tile(k_norm[l], ATT_KV_HEADS)),
            bd, cos_t, sin_t, cos_tt, sin_tt)
        att = _attention(qt, k, vt)
        sbias = jnp.repeat(sg_b[l].T, HEAD_DIM, axis=1)
        x = _post(x, rest, att, _bf16(w_out[l]), row2(post_norm[l]),
                  conv_dw[l], row2(conv_dw_b[l]), row2(conv_ln_g[l]), row2(conv_ln_b[l]),
                  row2(sg_ln_g[l]), row2(sg_ln_b[l]), _bf16(sg_w[l]), sbias)
    return x
```

```python
import math

import jax
import jax.numpy as jnp
from jax import lax
from jax.experimental import pallas as pl
from jax.experimental.pallas import tpu as pltpu

D_MODEL = 1024
HEAD_DIM = 64
GRID_W = 64
EPS = 1e-6
ATT_HEADS = 8
ATT_KV_HEADS = 2
GQA = ATT_HEADS // ATT_KV_HEADS
ATT_WIDTH = ATT_HEADS * HEAD_DIM
KV_WIDTH = ATT_KV_HEADS * HEAD_DIM
ROPE_THETA = 10000.0
ROPE_AXIS_DIM = HEAD_DIM // 2
CONV_WIDTH = 256
CONV_KERNEL = 31
CONV_PAD = CONV_KERNEL // 2
SG_HEADS = 4
SG_WIDTH = SG_HEADS * HEAD_DIM
SG_CHUNK = 128
D_MIX = ATT_WIDTH + CONV_WIDTH + SG_WIDTH
QKV_WIDTH = ATT_WIDTH + 2 * KV_WIDTH
REST_WIDTH = ATT_WIDTH + 3 * CONV_WIDTH + 3 * SG_WIDTH
D_IN = QKV_WIDTH + REST_WIDTH

R_GATT = 0
R_ACONV = ATT_WIDTH
R_GCONV = R_ACONV + 2 * CONV_WIDTH
R_U = R_GCONV + CONV_WIDTH
R_V = R_U + SG_WIDTH
R_GSG = R_V + SG_WIDTH

LANES = 128
SUBLANES = 8
HALO = 16
TM = 512
TQ = 512
TK = 512
ATT_UNROLL = 30
NQ = GQA * TQ
LANE_GROUP = 256
V_ROWS = HEAD_DIM + 16
SCORE_SCALE = HEAD_DIM ** -0.5 * math.log2(math.e)
VMEM_LIMIT = 56 * 1024 * 1024


def _bf16(x):
    return x.astype(jnp.bfloat16)


def _group_mean_sq(x, bd):
    x2 = x * x
    hi = _bf16(x2)
    lo = _bf16(x2 - hi.astype(jnp.float32))
    return (jnp.dot(hi, bd, preferred_element_type=jnp.float32)
            + jnp.dot(lo, bd, preferred_element_type=jnp.float32))


def _rope_slab(xn, cos, sin_signed, first_half):
    fwd = pltpu.roll(xn, LANES - ROPE_AXIS_DIM // 2, 1)
    bwd = pltpu.roll(xn, ROPE_AXIS_DIM // 2, 1)
    partner = jnp.where(first_half, fwd, bwd)
    return xn * cos + partner * sin_signed


def _in_proj_kernel(x_ref, g_ref, w_ref, gq_ref, gk_ref, bd_ref, cos_ref, sin_ref, cost_ref, sint_ref,
                    qt_ref, k_ref, vt_ref, rest_ref):
    x = x_ref[...]
    ms = jnp.mean(x * x, axis=-1, keepdims=True)
    h = _bf16(x * lax.rsqrt(ms + EPS) * g_ref[...])

    q = jnp.dot(h, w_ref[:, 0:ATT_WIDTH], preferred_element_type=jnp.float32)
    cost = cost_ref[...]
    sint = sint_ref[...]
    gq = jnp.concatenate([gq_ref[...]] * (TM // LANES), axis=1)
    quarter = ROPE_AXIS_DIM // 2
    for j in range(ATT_WIDTH // LANES):
        qt = q[:, j * LANES:(j + 1) * LANES].T
        for hh in range(LANES // HEAD_DIM):
            xh = qt[hh * HEAD_DIM:(hh + 1) * HEAD_DIM, :]
            xn = xh * lax.rsqrt(jnp.mean(xh * xh, axis=0, keepdims=True) + EPS) * gq
            partner = jnp.concatenate([xn[quarter:2 * quarter], xn[0:quarter],
                                       xn[3 * quarter:4 * quarter], xn[2 * quarter:3 * quarter]], axis=0)
            row0 = j * LANES + hh * HEAD_DIM
            qt_ref[row0:row0 + HEAD_DIM, :] = _bf16((xn * cost + partner * sint) * SCORE_SCALE)

    lane = lax.broadcasted_iota(jnp.int32, (1, LANES), 1)
    first_half = (lane % ROPE_AXIS_DIM) < (ROPE_AXIS_DIM // 2)
    kv = jnp.dot(h, w_ref[:, ATT_WIDTH:QKV_WIDTH], preferred_element_type=jnp.float32)
    kk = kv[:, 0:KV_WIDTH]
    kn = kk * lax.rsqrt(_group_mean_sq(kk, bd_ref[...]) + EPS) * gk_ref[...]
    k_ref[...] = _bf16(_rope_slab(kn, cos_ref[...], sin_ref[...], first_half))

    vt = _bf16(kv[:, KV_WIDTH:2 * KV_WIDTH].T)
    for c in range(TM // TK):
        for hd in range(ATT_KV_HEADS):
            vt_ref[c, hd, 0:HEAD_DIM, :] = vt[hd * HEAD_DIM:(hd + 1) * HEAD_DIM, c * TK:(c + 1) * TK]
            vt_ref[c, hd, HEAD_DIM:V_ROWS, :] = jnp.ones((V_ROWS - HEAD_DIM, TK), jnp.bfloat16)

    rest_ref[...] = jnp.dot(h, w_ref[:, QKV_WIDTH:D_IN], preferred_element_type=jnp.float32)


def _in_proj(x, g, w, gq, gk, bd, cos_t, sin_t, cos_tt, sin_tt):
    B, S, _ = x.shape
    ns = S // TM
    const = lambda b, s: (0, 0)
    return pl.pallas_call(
        _in_proj_kernel,
        grid=(B, ns),
        in_specs=[
            pl.BlockSpec((None, TM, D_MODEL), lambda b, s: (b, s, 0)),
            pl.BlockSpec((1, D_MODEL), const),
            pl.BlockSpec((D_MODEL, D_IN), const),
            pl.BlockSpec((HEAD_DIM, LANES), const),
            pl.BlockSpec((1, KV_WIDTH), const),
            pl.BlockSpec((KV_WIDTH, KV_WIDTH), const),
            pl.BlockSpec((TM, LANES), lambda b, s: (s, 0)),
            pl.BlockSpec((TM, LANES), lambda b, s: (s, 0)),
            pl.BlockSpec((HEAD_DIM, TM), lambda b, s: (0, s)),
            pl.BlockSpec((HEAD_DIM, TM), lambda b, s: (0, s)),
        ],
        out_specs=[
            pl.BlockSpec((None, ATT_WIDTH, TM), lambda b, s: (b, 0, s)),
            pl.BlockSpec((None, TM, KV_WIDTH), lambda b, s: (b, s, 0)),
            pl.BlockSpec((None, TM // TK, ATT_KV_HEADS, V_ROWS, TK), lambda b, s: (b, s, 0, 0, 0)),
            pl.BlockSpec((None, TM, REST_WIDTH), lambda b, s: (b, s, 0)),
        ],
        out_shape=[
            jax.ShapeDtypeStruct((B, ATT_WIDTH, S), jnp.bfloat16),
            jax.ShapeDtypeStruct((B, S, KV_WIDTH), jnp.bfloat16),
            jax.ShapeDtypeStruct((B, S // TK, ATT_KV_HEADS, V_ROWS, TK), jnp.bfloat16),
            jax.ShapeDtypeStruct((B, S, REST_WIDTH), jnp.float32),
        ],
        compiler_params=pltpu.CompilerParams(
            dimension_semantics=("arbitrary", "arbitrary"), vmem_limit_bytes=VMEM_LIMIT),
        name="in_proj",
    )(x, g, w, gq, gk, bd, cos_t, sin_t, cos_tt, sin_tt)


def _attention_kernel(qt_ref, k_ref, vt_ref, o_ref, qext_ref, s_ref, p_ref, acc_ref, m_ref, mx_ref):
    kv = pl.program_id(1)
    n = k_ref.shape[0] // TK
    for g in range(GQA):
        qg = qt_ref[g * HEAD_DIM:(g + 1) * HEAD_DIM, :]
        zero = jnp.zeros_like(qg)
        qext_ref[0:HEAD_DIM, g * TQ:(g + 1) * TQ] = jnp.where(kv == 0, qg, zero)
        qext_ref[HEAD_DIM:2 * HEAD_DIM, g * TQ:(g + 1) * TQ] = jnp.where(kv == 1, qg, zero)

    def scores(j, g, cols):
        kblk = k_ref[pl.ds(pl.multiple_of(j * TK, TK), TK), :]
        s = jnp.dot(kblk, qext_ref[:, cols], preferred_element_type=jnp.float32)
        s_ref[g] = s
        mx_ref[:, cols] = jnp.max(s, axis=0, keepdims=True)

    def values(j, g):
        return jnp.dot(vt_ref[j], p_ref[g], preferred_element_type=jnp.float32)

    groups = [(g, slice(g * LANE_GROUP, (g + 1) * LANE_GROUP)) for g in range(NQ // LANE_GROUP)]

    def stage(j, first=False, last=False):
        for g, cols in groups:
            if first:
                m_new = mx_ref[:, cols]
            else:
                pv = values(j - 1, g)
                m_old = m_ref[:, cols]
                m_new = jnp.maximum(m_old, mx_ref[:, cols])
                acc_ref[:, cols] = jnp.exp2(m_old - m_new) * (acc_ref[:, cols] + pv)
            m_ref[:, cols] = m_new
            p_ref[g] = _bf16(jnp.exp2(s_ref[g] - m_new))
            if not last:
                scores(j + 1, g, cols)

    acc_ref[...] = jnp.zeros_like(acc_ref)
    for g, cols in groups:
        scores(0, g, cols)
    stage(0, first=True)

    def step(j, carry):
        stage(j)
        return carry

    assert (n - 2) % ATT_UNROLL == 0
    lax.fori_loop(1, n - 1, step, 0, unroll=ATT_UNROLL)
    stage(n - 1, last=True)
    for g, cols in groups:
        acc_ref[:, cols] = acc_ref[:, cols] + values(n - 1, g)
    o = acc_ref[0:HEAD_DIM, :] / acc_ref[HEAD_DIM:HEAD_DIM + 1, :]
    o4 = jnp.concatenate([o[:, g * TQ:(g + 1) * TQ] for g in range(GQA)], axis=0)
    o_ref[...] = o4.T


def _attention(qt, k, vt):
    B, _, S = qt.shape
    nk = vt.shape[1]
    assert vt.shape[-1] == TK
    return pl.pallas_call(
        _attention_kernel,
        grid=(B, ATT_KV_HEADS, S // TQ),
        in_specs=[
            pl.BlockSpec((None, GQA * HEAD_DIM, TQ), lambda b, kv, qi: (b, kv, qi)),
            pl.BlockSpec((None, S, KV_WIDTH), lambda b, kv, qi: (b, 0, 0)),
            pl.BlockSpec((None, nk, None, V_ROWS, TK), lambda b, kv, qi: (b, 0, kv, 0, 0)),
        ],
        out_specs=pl.BlockSpec((None, TQ, GQA * HEAD_DIM), lambda b, kv, qi: (b, qi, kv)),
        out_shape=jax.ShapeDtypeStruct((B, S, ATT_WIDTH), jnp.float32),
        scratch_shapes=[
            pltpu.VMEM((KV_WIDTH, NQ), jnp.bfloat16),
            pltpu.VMEM((NQ // LANE_GROUP, TK, LANE_GROUP), jnp.float32),
            pltpu.VMEM((NQ // LANE_GROUP, TK, LANE_GROUP), jnp.bfloat16),
            pltpu.VMEM((V_ROWS, NQ), jnp.float32),
            pltpu.VMEM((1, NQ), jnp.float32),
            pltpu.VMEM((1, NQ), jnp.float32),
        ],
        compiler_params=pltpu.CompilerParams(
            dimension_semantics=("arbitrary", "arbitrary", "arbitrary"),
            vmem_limit_bytes=VMEM_LIMIT),
        name="attention",
    )(qt, k, vt)


def _silu(x):
    return x * jax.nn.sigmoid(x)


def _gelu(x):
    return 0.5 * x * (1.0 + lax.erf(x * (2.0 ** -0.5)))


def _layer_norm(x, g, b):
    mu = jnp.mean(x, axis=-1, keepdims=True)
    xc = x - mu
    var = jnp.mean(xc * xc, axis=-1, keepdims=True)
    return xc * lax.rsqrt(var + EPS) * g + b


def _glu(a):
    return a[:, :CONV_WIDTH] * jax.nn.sigmoid(a[:, CONV_WIDTH:])


CONV_ROWS = 64


def _post_kernel(x_ref, rest_ref, prev_ref, next_ref, att_ref, wout_ref, gpost_ref,
                 dw_ref, dwb_ref, cg_ref, cb_ref, sg_ref, sb_ref, sw_ref, sbias_ref,
                 o_ref, hext_ref, hsh_ref, cat_ref):
    si = pl.program_id(1)
    last = pl.num_programs(1) - 1

    cat_ref[:, 0:ATT_WIDTH] = _bf16(att_ref[...] * _silu(rest_ref[:, R_GATT:R_GATT + ATT_WIDTH]))

    hext_ref[0:HALO, :] = jnp.where(si > 0, _glu(prev_ref[...]), 0.0)
    hext_ref[HALO:HALO + TM, :] = _glu(rest_ref[:, R_ACONV:R_ACONV + 2 * CONV_WIDTH])
    hext_ref[HALO + TM:2 * HALO + TM, :] = jnp.where(si < last, _glu(next_ref[...]), 0.0)
    for r in range(SUBLANES):
        hsh_ref[r] = hext_ref[r:r + TM + 2 * HALO - SUBLANES, :]
    for rb in range(TM // CONV_ROWS):
        acc = jnp.zeros((CONV_ROWS, CONV_WIDTH), jnp.float32) + dwb_ref[...]
        for j in range(CONV_KERNEL):
            r = (j + HALO - CONV_PAD) % SUBLANES
            start = rb * CONV_ROWS + j + HALO - CONV_PAD - r
            acc = acc + hsh_ref[r, start:start + CONV_ROWS, :] * dw_ref[j:j + 1, :]
        rows = slice(rb * CONV_ROWS, (rb + 1) * CONV_ROWS)
        c = _silu(_layer_norm(acc, cg_ref[...], cb_ref[...]))
        c = c * _silu(rest_ref[rows, R_GCONV:R_GCONV + CONV_WIDTH])
        cat_ref[rows, ATT_WIDTH:ATT_WIDTH + CONV_WIDTH] = _bf16(c)

    lane = lax.broadcasted_iota(jnp.int32, (1, SG_WIDTH), 1)
    for c in range(TM // SG_CHUNK):
        rows = slice(c * SG_CHUNK, (c + 1) * SG_CHUNK)
        vln = _bf16(_layer_norm(_gelu(rest_ref[rows, R_V:R_V + SG_WIDTH]), sg_ref[...], sb_ref[...]))
        mixed = sbias_ref[...]
        for hd in range(SG_HEADS):
            mh = jnp.dot(sw_ref[hd], vln, preferred_element_type=jnp.float32)
            mixed = mixed + jnp.where(lane // HEAD_DIM == hd, mh, 0.0)
        sgu = _gelu(rest_ref[rows, R_U:R_U + SG_WIDTH]) * mixed
        sgu = sgu * _silu(rest_ref[rows, R_GSG:R_GSG + SG_WIDTH])
        cat_ref[rows, ATT_WIDTH + CONV_WIDTH:D_MIX] = _bf16(sgu)

    mix = jnp.dot(cat_ref[...], wout_ref[...], preferred_element_type=jnp.float32)
    ms = jnp.mean(mix * mix, axis=-1, keepdims=True)
    o_ref[...] = x_ref[...] + mix * lax.rsqrt(ms + EPS) * gpost_ref[...]


def _post(x, rest, att, wout, gpost, dw, dwb, cg, cb, sg, sb, sw, sbias):
    B, S, _ = x.shape
    ns = S // TM
    hb = TM // HALO
    const2 = lambda b, s: (0, 0)
    const3 = lambda b, s: (0, 0, 0)
    aconv_blk = R_ACONV // (2 * CONV_WIDTH)
    return pl.pallas_call(
        _post_kernel,
        grid=(B, ns),
        in_specs=[
            pl.BlockSpec((None, TM, D_MODEL), lambda b, s: (b, s, 0)),
            pl.BlockSpec((None, TM, REST_WIDTH), lambda b, s: (b, s, 0)),
            pl.BlockSpec((None, HALO, 2 * CONV_WIDTH),
                         lambda b, s: (b, jnp.maximum(s * hb - 1, 0), aconv_blk)),
            pl.BlockSpec((None, HALO, 2 * CONV_WIDTH),
                         lambda b, s: (b, jnp.minimum((s + 1) * hb, S // HALO - 1), aconv_blk)),
            pl.BlockSpec((None, TM, ATT_WIDTH), lambda b, s: (b, s, 0)),
            pl.BlockSpec((D_MIX, D_MODEL), const2),
            pl.BlockSpec((1, D_MODEL), const2),
            pl.BlockSpec((CONV_KERNEL, CONV_WIDTH), const2),
            pl.BlockSpec((1, CONV_WIDTH), const2),
            pl.BlockSpec((1, CONV_WIDTH), const2),
            pl.BlockSpec((1, CONV_WIDTH), const2),
            pl.BlockSpec((1, SG_WIDTH), const2),
            pl.BlockSpec((1, SG_WIDTH), const2),
            pl.BlockSpec((SG_HEADS, SG_CHUNK, SG_CHUNK), const3),
            pl.BlockSpec((SG_CHUNK, SG_WIDTH), const2),
        ],
        out_specs=pl.BlockSpec((None, TM, D_MODEL), lambda b, s: (b, s, 0)),
        out_shape=jax.ShapeDtypeStruct((B, S, D_MODEL), jnp.float32),
        scratch_shapes=[
            pltpu.VMEM((TM + 2 * HALO, CONV_WIDTH), jnp.float32),
            pltpu.VMEM((SUBLANES, TM + 2 * HALO - SUBLANES, CONV_WIDTH), jnp.float32),
            pltpu.VMEM((TM, D_MIX), jnp.bfloat16),
        ],
        compiler_params=pltpu.CompilerParams(
            dimension_semantics=("arbitrary", "arbitrary"), vmem_limit_bytes=VMEM_LIMIT),
        name="post",
    )(x, rest, rest, rest, att, wout, gpost, dw, dwb, cg, cb, sg, sb, sw, sbias)


def _rope_tables(S):
    half = ROPE_AXIS_DIM // 2
    t = jnp.arange(S, dtype=jnp.int32)
    row = (t // GRID_W).astype(jnp.float32)
    col = (t % GRID_W).astype(jnp.float32)
    inv_freq = ROPE_THETA ** (-jnp.arange(half, dtype=jnp.float32) / half)
    ang_r = row[:, None] * inv_freq[None, :]
    ang_c = col[:, None] * inv_freq[None, :]
    cos_h = jnp.concatenate([jnp.cos(ang_r)] * 2 + [jnp.cos(ang_c)] * 2, axis=-1)
    sin_h = jnp.concatenate([-jnp.sin(ang_r), jnp.sin(ang_r), -jnp.sin(ang_c), jnp.sin(ang_c)], axis=-1)
    reps = LANES // HEAD_DIM
    return jnp.tile(cos_h, (1, reps)), jnp.tile(sin_h, (1, reps)), cos_h.T, sin_h.T


def kernel(x, pre_norm, post_norm, w_in, w_out, q_norm, k_norm, conv_dw, conv_dw_b,
           conv_ln_g, conv_ln_b, sg_ln_g, sg_ln_b, sg_w, sg_b):
    B, S, _ = x.shape
    depth = w_in.shape[0]
    cos_t, sin_t, cos_tt, sin_tt = _rope_tables(S)
    head_id = jnp.arange(KV_WIDTH, dtype=jnp.int32) // HEAD_DIM
    bd = _bf16(jnp.where(head_id[:, None] == head_id[None, :], 1.0 / HEAD_DIM, 0.0))
    row2 = lambda a: a.reshape(1, -1)
    for l in range(depth):
        qt, k, vt, rest = _in_proj(
            x, row2(pre_norm[l]), _bf16(w_in[l]),
            jnp.broadcast_to(q_norm[l][:, None], (HEAD_DIM, LANES)), row2(jnp.tile(k_norm[l], ATT_KV_HEADS)),
            bd, cos_t, sin_t, cos_tt, sin_tt)
        att = _attention(qt, k, vt)
        sbias = jnp.repeat(sg_b[l].T, HEAD_DIM, axis=1)
        x = _post(x, rest, att, _bf16(w_out[l]), row2(post_norm[l]),
                  conv_dw[l], row2(conv_dw_b[l]), row2(conv_ln_g[l]), row2(conv_ln_b[l]),
                  row2(sg_ln_g[l]), row2(sg_ln_b[l]), _bf16(sg_w[l]), sbias)
    return x
```

```python
import math

import jax
import jax.numpy as jnp
from jax import lax
from jax.experimental import pallas as pl
from jax.experimental.pallas import tpu as pltpu

D_MODEL = 1024
HEAD_DIM = 64
GRID_W = 64
EPS = 1e-6
ATT_HEADS = 8
ATT_KV_HEADS = 2
GQA = ATT_HEADS // ATT_KV_HEADS
ATT_WIDTH = ATT_HEADS * HEAD_DIM
KV_WIDTH = ATT_KV_HEADS * HEAD_DIM
ROPE_THETA = 10000.0
ROPE_AXIS_DIM = HEAD_DIM // 2
CONV_WIDTH = 256
CONV_KERNEL = 31
CONV_PAD = CONV_KERNEL // 2
SG_HEADS = 4
SG_WIDTH = SG_HEADS * HEAD_DIM
SG_CHUNK = 128
D_MIX = ATT_WIDTH + CONV_WIDTH + SG_WIDTH
QKV_WIDTH = ATT_WIDTH + 2 * KV_WIDTH
REST_WIDTH = ATT_WIDTH + 3 * CONV_WIDTH + 3 * SG_WIDTH
D_IN = QKV_WIDTH + REST_WIDTH

R_GATT = 0
R_ACONV = ATT_WIDTH
R_GCONV = R_ACONV + 2 * CONV_WIDTH
R_U = R_GCONV + CONV_WIDTH
R_V = R_U + SG_WIDTH
R_GSG = R_V + SG_WIDTH

LANES = 128
SUBLANES = 8
HALO = 16
TM = 512
TQ = 512
TK = 256
ATT_UNROLL = 31
NQ = GQA * TQ
LANE_GROUP = 256
V_ROWS = HEAD_DIM + 16
SCORE_SCALE = HEAD_DIM ** -0.5 * math.log2(math.e)
VMEM_LIMIT = 56 * 1024 * 1024


def _bf16(x):
    return x.astype(jnp.bfloat16)


def _group_mean_sq(x, bd):
    x2 = x * x
    hi = _bf16(x2)
    lo = _bf16(x2 - hi.astype(jnp.float32))
    return (jnp.dot(hi, bd, preferred_element_type=jnp.float32)
            + jnp.dot(lo, bd, preferred_element_type=jnp.float32))


def _rope_slab(xn, cos, sin_signed, first_half):
    fwd = pltpu.roll(xn, LANES - ROPE_AXIS_DIM // 2, 1)
    bwd = pltpu.roll(xn, ROPE_AXIS_DIM // 2, 1)
    partner = jnp.where(first_half, fwd, bwd)
    return xn * cos + partner * sin_signed


def _in_proj_kernel(x_ref, g_ref, w_ref, gq_ref, gk_ref, bd_ref, cos_ref, sin_ref, cost_ref, sint_ref,
                    qt_ref, k_ref, vt_ref, rest_ref):
    x = x_ref[...]
    ms = jnp.mean(x * x, axis=-1, keepdims=True)
    h = _bf16(x * lax.rsqrt(ms + EPS) * g_ref[...])

    q = jnp.dot(h, w_ref[:, 0:ATT_WIDTH], preferred_element_type=jnp.float32)
    cost = cost_ref[...]
    sint = sint_ref[...]
    gq = jnp.concatenate([gq_ref[...]] * (TM // LANES), axis=1)
    quarter = ROPE_AXIS_DIM // 2
    for j in range(ATT_WIDTH // LANES):
        qt = q[:, j * LANES:(j + 1) * LANES].T
        for hh in range(LANES // HEAD_DIM):
            xh = qt[hh * HEAD_DIM:(hh + 1) * HEAD_DIM, :]
            xn = xh * lax.rsqrt(jnp.mean(xh * xh, axis=0, keepdims=True) + EPS) * gq
            partner = jnp.concatenate([xn[quarter:2 * quarter], xn[0:quarter],
                                       xn[3 * quarter:4 * quarter], xn[2 * quarter:3 * quarter]], axis=0)
            row0 = j * LANES + hh * HEAD_DIM
            qt_ref[row0:row0 + HEAD_DIM, :] = _bf16((xn * cost + partner * sint) * SCORE_SCALE)

    lane = lax.broadcasted_iota(jnp.int32, (1, LANES), 1)
    first_half = (lane % ROPE_AXIS_DIM) < (ROPE_AXIS_DIM // 2)
    kv = jnp.dot(h, w_ref[:, ATT_WIDTH:QKV_WIDTH], preferred_element_type=jnp.float32)
    kk = kv[:, 0:KV_WIDTH]
    kn = kk * lax.rsqrt(_group_mean_sq(kk, bd_ref[...]) + EPS) * gk_ref[...]
    k_ref[...] = _bf16(_rope_slab(kn, cos_ref[...], sin_ref[...], first_half))

    vt = _bf16(kv[:, KV_WIDTH:2 * KV_WIDTH].T)
    for c in range(TM // TK):
        for hd in range(ATT_KV_HEADS):
            vt_ref[c, hd, 0:HEAD_DIM, :] = vt[hd * HEAD_DIM:(hd + 1) * HEAD_DIM, c * TK:(c + 1) * TK]
            vt_ref[c, hd, HEAD_DIM:V_ROWS, :] = jnp.ones((V_ROWS - HEAD_DIM, TK), jnp.bfloat16)

    rest_ref[...] = jnp.dot(h, w_ref[:, QKV_WIDTH:D_IN], preferred_element_type=jnp.float32)


def _in_proj(x, g, w, gq, gk, bd, cos_t, sin_t, cos_tt, sin_tt):
    B, S, _ = x.shape
    ns = S // TM
    const = lambda b, s: (0, 0)
    return pl.pallas_call(
        _in_proj_kernel,
        grid=(B, ns),
        in_specs=[
            pl.BlockSpec((None, TM, D_MODEL), lambda b, s: (b, s, 0)),
            pl.BlockSpec((1, D_MODEL), const),
            pl.BlockSpec((D_MODEL, D_IN), const),
            pl.BlockSpec((HEAD_DIM, LANES), const),
            pl.BlockSpec((1, KV_WIDTH), const),
            pl.BlockSpec((KV_WIDTH, KV_WIDTH), const),
            pl.BlockSpec((TM, LANES), lambda b, s: (s, 0)),
            pl.BlockSpec((TM, LANES), lambda b, s: (s, 0)),
            pl.BlockSpec((HEAD_DIM, TM), lambda b, s: (0, s)),
            pl.BlockSpec((HEAD_DIM, TM), lambda b, s: (0, s)),
        ],
        out_specs=[
            pl.BlockSpec((None, ATT_WIDTH, TM), lambda b, s: (b, 0, s)),
            pl.BlockSpec((None, TM, KV_WIDTH), lambda b, s: (b, s, 0)),
            pl.BlockSpec((None, TM // TK, ATT_KV_HEADS, V_ROWS, TK), lambda b, s: (b, s, 0, 0, 0)),
            pl.BlockSpec((None, TM, REST_WIDTH), lambda b, s: (b, s, 0)),
        ],
        out_shape=[
            jax.ShapeDtypeStruct((B, ATT_WIDTH, S), jnp.bfloat16),
            jax.ShapeDtypeStruct((B, S, KV_WIDTH), jnp.bfloat16),
            jax.ShapeDtypeStruct((B, S // TK, ATT_KV_HEADS, V_ROWS, TK), jnp.bfloat16),
            jax.ShapeDtypeStruct((B, S, REST_WIDTH), jnp.float32),
        ],
        compiler_params=pltpu.CompilerParams(
            dimension_semantics=("arbitrary", "arbitrary"), vmem_limit_bytes=VMEM_LIMIT),
        name="in_proj",
    )(x, g, w, gq, gk, bd, cos_t, sin_t, cos_tt, sin_tt)


def _attention_kernel(qt_ref, k_ref, vt_ref, o_ref, qext_ref, s_ref, p_ref, acc_ref, m_ref, mx_ref):
    kv = pl.program_id(1)
    n = k_ref.shape[0] // TK
    for g in range(GQA):
        qg = qt_ref[g * HEAD_DIM:(g + 1) * HEAD_DIM, :]
        zero = jnp.zeros_like(qg)
        qext_ref[0:HEAD_DIM, g * TQ:(g + 1) * TQ] = jnp.where(kv == 0, qg, zero)
        qext_ref[HEAD_DIM:2 * HEAD_DIM, g * TQ:(g + 1) * TQ] = jnp.where(kv == 1, qg, zero)

    def scores(j, g, cols):
        kblk = k_ref[pl.ds(pl.multiple_of(j * TK, TK), TK), :]
        s = jnp.dot(kblk, qext_ref[:, cols], preferred_element_type=jnp.float32)
        s_ref[g] = s
        mx_ref[:, cols] = jnp.max(s, axis=0, keepdims=True)

    def values(j, g):
        return jnp.dot(vt_ref[j], p_ref[g], preferred_element_type=jnp.float32)

    groups = [(g, slice(g * LANE_GROUP, (g + 1) * LANE_GROUP)) for g in range(NQ // LANE_GROUP)]

    def stage(j, first=False, last=False):
        for g, cols in groups:
            if first:
                m_new = mx_ref[:, cols]
            else:
                pv = values(j - 1, g)
                m_old = m_ref[:, cols]
                m_new = jnp.maximum(m_old, mx_ref[:, cols])
                acc_ref[:, cols] = jnp.exp2(m_old - m_new) * (acc_ref[:, cols] + pv)
            m_ref[:, cols] = m_new
            p_ref[g] = _bf16(jnp.exp2(s_ref[g] - m_new))
            if not last:
                scores(j + 1, g, cols)

    acc_ref[...] = jnp.zeros_like(acc_ref)
    for g, cols in groups:
        scores(0, g, cols)
    stage(0, first=True)

    def step(j, carry):
        stage(j)
        return carry

    assert (n - 2) % ATT_UNROLL == 0
    lax.fori_loop(1, n - 1, step, 0, unroll=ATT_UNROLL)
    stage(n - 1, last=True)
    for g, cols in groups:
        acc_ref[:, cols] = acc_ref[:, cols] + values(n - 1, g)
    o = acc_ref[0:HEAD_DIM, :] / acc_ref[HEAD_DIM:HEAD_DIM + 1, :]
    o4 = jnp.concatenate([o[:, g * TQ:(g + 1) * TQ] for g in range(GQA)], axis=0)
    o_ref[...] = o4.T


def _attention(qt, k, vt):
    B, _, S = qt.shape
    nk = vt.shape[1]
    assert vt.shape[-1] == TK
    return pl.pallas_call(
        _attention_kernel,
        grid=(B, ATT_KV_HEADS, S // TQ),
        in_specs=[
            pl.BlockSpec((None, GQA * HEAD_DIM, TQ), lambda b, kv, qi: (b, kv, qi)),
            pl.BlockSpec((None, S, KV_WIDTH), lambda b, kv, qi: (b, 0, 0)),
            pl.BlockSpec((None, nk, None, V_ROWS, TK), lambda b, kv, qi: (b, 0, kv, 0, 0)),
        ],
        out_specs=pl.BlockSpec((None, TQ, GQA * HEAD_DIM), lambda b, kv, qi: (b, qi, kv)),
        out_shape=jax.ShapeDtypeStruct((B, S, ATT_WIDTH), jnp.float32),
        scratch_shapes=[
            pltpu.VMEM((KV_WIDTH, NQ), jnp.bfloat16),
            pltpu.VMEM((NQ // LANE_GROUP, TK, LANE_GROUP), jnp.float32),
            pltpu.VMEM((NQ // LANE_GROUP, TK, LANE_GROUP), jnp.bfloat16),
            pltpu.VMEM((V_ROWS, NQ), jnp.float32),
            pltpu.VMEM((1, NQ), jnp.float32),
            pltpu.VMEM((1, NQ), jnp.float32),
        ],
        compiler_params=pltpu.CompilerParams(
            dimension_semantics=("arbitrary", "arbitrary", "arbitrary"),
            vmem_limit_bytes=VMEM_LIMIT),
        name="attention",
    )(qt, k, vt)


def _silu(x):
    return x * jax.nn.sigmoid(x)


def _gelu(x):
    return 0.5 * x * (1.0 + lax.erf(x * (2.0 ** -0.5)))


def _layer_norm(x, g, b):
    mu = jnp.mean(x, axis=-1, keepdims=True)
    xc = x - mu
    var = jnp.mean(xc * xc, axis=-1, keepdims=True)
    return xc * lax.rsqrt(var + EPS) * g + b


def _glu(a):
    return a[:, :CONV_WIDTH] * jax.nn.sigmoid(a[:, CONV_WIDTH:])


CONV_ROWS = 64


def _post_kernel(x_ref, rest_ref, prev_ref, next_ref, att_ref, wout_ref, gpost_ref,
                 dw_ref, dwb_ref, cg_ref, cb_ref, sg_ref, sb_ref, sw_ref, sbias_ref,
                 o_ref, hext_ref, hsh_ref, cat_ref):
    si = pl.program_id(1)
    last = pl.num_programs(1) - 1

    cat_ref[:, 0:ATT_WIDTH] = _bf16(att_ref[...] * _silu(rest_ref[:, R_GATT:R_GATT + ATT_WIDTH]))

    hext_ref[0:HALO, :] = jnp.where(si > 0, _glu(prev_ref[...]), 0.0)
    hext_ref[HALO:HALO + TM, :] = _glu(rest_ref[:, R_ACONV:R_ACONV + 2 * CONV_WIDTH])
    hext_ref[HALO + TM:2 * HALO + TM, :] = jnp.where(si < last, _glu(next_ref[...]), 0.0)
    for r in range(SUBLANES):
        hsh_ref[r] = hext_ref[r:r + TM + 2 * HALO - SUBLANES, :]
    for rb in range(TM // CONV_ROWS):
        acc = jnp.zeros((CONV_ROWS, CONV_WIDTH), jnp.float32) + dwb_ref[...]
        for j in range(CONV_KERNEL):
            r = (j + HALO - CONV_PAD) % SUBLANES
            start = rb * CONV_ROWS + j + HALO - CONV_PAD - r
            acc = acc + hsh_ref[r, start:start + CONV_ROWS, :] * dw_ref[j:j + 1, :]
        rows = slice(rb * CONV_ROWS, (rb + 1) * CONV_ROWS)
        c = _silu(_layer_norm(acc, cg_ref[...], cb_ref[...]))
        c = c * _silu(rest_ref[rows, R_GCONV:R_GCONV + CONV_WIDTH])
        cat_ref[rows, ATT_WIDTH:ATT_WIDTH + CONV_WIDTH] = _bf16(c)

    lane = lax.broadcasted_iota(jnp.int32, (1, SG_WIDTH), 1)
    for c in range(TM // SG_CHUNK):
        rows = slice(c * SG_CHUNK, (c + 1) * SG_CHUNK)
        vln = _bf16(_layer_norm(_gelu(rest_ref[rows, R_V:R_V + SG_WIDTH]), sg_ref[...], sb_ref[...]))
        mixed = sbias_ref[...]
        for hd in range(SG_HEADS):
            mh = jnp.dot(sw_ref[hd], vln, preferred_element_type=jnp.float32)
            mixed = mixed + jnp.where(lane // HEAD_DIM == hd, mh, 0.0)
        sgu = _gelu(rest_ref[rows, R_U:R_U + SG_WIDTH]) * mixed
        sgu = sgu * _silu(rest_ref[rows, R_GSG:R_GSG + SG_WIDTH])
        cat_ref[rows, ATT_WIDTH + CONV_WIDTH:D_MIX] = _bf16(sgu)

    mix = jnp.dot(cat_ref[...], wout_ref[...], preferred_element_type=jnp.float32)
    ms = jnp.mean(mix * mix, axis=-1, keepdims=True)
    o_ref[...] = x_ref[...] + mix * lax.rsqrt(ms + EPS) * gpost_ref[...]


def _post(x, rest, att, wout, gpost, dw, dwb, cg, cb, sg, sb, sw, sbias):
    B, S, _ = x.shape
    ns = S // TM
    hb = TM // HALO
    const2 = lambda b, s: (0, 0)
    const3 = lambda b, s: (0, 0, 0)
    aconv_blk = R_ACONV // (2 * CONV_WIDTH)
    return pl.pallas_call(
        _post_kernel,
        grid=(B, ns),
        in_specs=[
            pl.BlockSpec((None, TM, D_MODEL), lambda b, s: (b, s, 0)),
            pl.BlockSpec((None, TM, REST_WIDTH), lambda b, s: (b, s, 0)),
            pl.BlockSpec((None, HALO, 2 * CONV_WIDTH),
                         lambda b, s: (b, jnp.maximum(s * hb - 1, 0), aconv_blk)),
            pl.BlockSpec((None, HALO, 2 * CONV_WIDTH),
                         lambda b, s: (b, jnp.minimum((s + 1) * hb, S // HALO - 1), aconv_blk)),
            pl.BlockSpec((None, TM, ATT_WIDTH), lambda b, s: (b, s, 0)),
            pl.BlockSpec((D_MIX, D_MODEL), const2),
            pl.BlockSpec((1, D_MODEL), const2),
            pl.BlockSpec((CONV_KERNEL, CONV_WIDTH), const2),
            pl.BlockSpec((1, CONV_WIDTH), const2),
            pl.BlockSpec((1, CONV_WIDTH), const2),
            pl.BlockSpec((1, CONV_WIDTH), const2),
            pl.BlockSpec((1, SG_WIDTH), const2),
            pl.BlockSpec((1, SG_WIDTH), const2),
            pl.BlockSpec((SG_HEADS, SG_CHUNK, SG_CHUNK), const3),
            pl.BlockSpec((SG_CHUNK, SG_WIDTH), const2),
        ],
        out_specs=pl.BlockSpec((None, TM, D_MODEL), lambda b, s: (b, s, 0)),
        out_shape=jax.ShapeDtypeStruct((B, S, D_MODEL), jnp.float32),
        scratch_shapes=[
            pltpu.VMEM((TM + 2 * HALO, CONV_WIDTH), jnp.float32),
            pltpu.VMEM((SUBLANES, TM + 2 * HALO - SUBLANES, CONV_WIDTH), jnp.float32),
            pltpu.VMEM((TM, D_MIX), jnp.bfloat16),
        ],
        compiler_params=pltpu.CompilerParams(
            dimension_semantics=("arbitrary", "arbitrary"), vmem_limit_bytes=VMEM_LIMIT),
        name="post",
    )(x, rest, rest, rest, att, wout, gpost, dw, dwb, cg, cb, sg, sb, sw, sbias)


def _rope_tables(S):
    half = ROPE_AXIS_DIM // 2
    t = jnp.arange(S, dtype=jnp.int32)
    row = (t // GRID_W).astype(jnp.float32)
    col = (t % GRID_W).astype(jnp.float32)
    inv_freq = ROPE_THETA ** (-jnp.arange(half, dtype=jnp.float32) / half)
    ang_r = row[:, None] * inv_freq[None, :]
    ang_c = col[:, None] * inv_freq[None, :]
    cos_h = jnp.concatenate([jnp.cos(ang_r)] * 2 + [jnp.cos(ang_c)] * 2, axis=-1)
    sin_h = jnp.concatenate([-jnp.sin(ang_r), jnp.sin(ang_r), -jnp.sin(ang_c), jnp.sin(ang_c)], axis=-1)
    reps = LANES // HEAD_DIM
    return jnp.tile(cos_h, (1, reps)), jnp.tile(sin_h, (1, reps)), cos_h.T, sin_h.T


def kernel(x, pre_norm, post_norm, w_in, w_out, q_norm, k_norm, conv_dw, conv_dw_b,
           conv_ln_g, conv_ln_b, sg_ln_g, sg_ln_b, sg_w, sg_b):
    B, S, _ = x.shape
    depth = w_in.shape[0]
    cos_t, sin_t, cos_tt, sin_tt = _rope_tables(S)
    head_id = jnp.arange(KV_WIDTH, dtype=jnp.int32) // HEAD_DIM
    bd = _bf16(jnp.where(head_id[:, None] == head_id[None, :], 1.0 / HEAD_DIM, 0.0))
    row2 = lambda a: a.reshape(1, -1)
    for l in range(depth):
        qt, k, vt, rest = _in_proj(
            x, row2(pre_norm[l]), _bf16(w_in[l]),
            jnp.broadcast_to(q_norm[l][:, None], (HEAD_DIM, LANES)), row2(jnp.tile(k_norm[l], ATT_KV_HEADS)),
            bd, cos_t, sin_t, cos_tt, sin_tt)
        att = _attention(qt, k, vt)
        sbias = jnp.repeat(sg_b[l].T, HEAD_DIM, axis=1)
        x = _post(x, rest, att, _bf16(w_out[l]), row2(post_norm[l]),
                  conv_dw[l], row2(conv_dw_b[l]), row2(conv_ln_g[l]), row2(conv_ln_b[l]),
                  row2(sg_ln_g[l]), row2(sg_ln_b[l]), _bf16(sg_w[l]), sbias)
    return x
```

```python
import math

import jax
import jax.numpy as jnp
from jax import lax
from jax.experimental import pallas as pl
from jax.experimental.pallas import tpu as pltpu

D_MODEL = 1024
HEAD_DIM = 64
GRID_W = 64
EPS = 1e-6
ATT_HEADS = 8
ATT_KV_HEADS = 2
GQA = ATT_HEADS // ATT_KV_HEADS
ATT_WIDTH = ATT_HEADS * HEAD_DIM
KV_WIDTH = ATT_KV_HEADS * HEAD_DIM
ROPE_THETA = 10000.0
ROPE_AXIS_DIM = HEAD_DIM // 2
CONV_WIDTH = 256
CONV_KERNEL = 31
CONV_PAD = CONV_KERNEL // 2
SG_HEADS = 4
SG_WIDTH = SG_HEADS * HEAD_DIM
SG_CHUNK = 128
D_MIX = ATT_WIDTH + CONV_WIDTH + SG_WIDTH
QKV_WIDTH = ATT_WIDTH + 2 * KV_WIDTH
REST_WIDTH = ATT_WIDTH + 3 * CONV_WIDTH + 3 * SG_WIDTH
D_IN = QKV_WIDTH + REST_WIDTH

R_GATT = 0
R_ACONV = ATT_WIDTH
R_GCONV = R_ACONV + 2 * CONV_WIDTH
R_U = R_GCONV + CONV_WIDTH
R_V = R_U + SG_WIDTH
R_GSG = R_V + SG_WIDTH

LANES = 128
SUBLANES = 8
HALO = 16
TM = 512
TQ = 512
TK = 256
ATT_UNROLL = 31
NQ = GQA * TQ
LANE_GROUP = 256
V_ROWS = HEAD_DIM + 16
SCORE_SCALE = HEAD_DIM ** -0.5 * math.log2(math.e)
VMEM_LIMIT = 56 * 1024 * 1024


def _bf16(x):
    return x.astype(jnp.bfloat16)


def _group_mean_sq(x, bd):
    x2 = x * x
    hi = _bf16(x2)
    lo = _bf16(x2 - hi.astype(jnp.float32))
    return (jnp.dot(hi, bd, preferred_element_type=jnp.float32)
            + jnp.dot(lo, bd, preferred_element_type=jnp.float32))


def _rope_slab(xn, cos, sin_signed, first_half):
    fwd = pltpu.roll(xn, LANES - ROPE_AXIS_DIM // 2, 1)
    bwd = pltpu.roll(xn, ROPE_AXIS_DIM // 2, 1)
    partner = jnp.where(first_half, fwd, bwd)
    return xn * cos + partner * sin_signed


def _in_proj_kernel(x_ref, g_ref, w_ref, gq_ref, gk_ref, bd_ref, cos_ref, sin_ref, cost_ref, sint_ref,
                    qt_ref, k_ref, vt_ref, rest_ref):
    x = x_ref[...]
    rstd = lax.rsqrt(jnp.mean(x * x, axis=-1, keepdims=True) + EPS)
    h = _bf16(x * g_ref[...])

    q = jnp.dot(h, w_ref[:, 0:ATT_WIDTH], preferred_element_type=jnp.float32) * rstd
    cost = cost_ref[...]
    sint = sint_ref[...]
    gq = jnp.concatenate([gq_ref[...]] * (TM // LANES), axis=1)
    quarter = ROPE_AXIS_DIM // 2
    for j in range(ATT_WIDTH // LANES):
        qt = q[:, j * LANES:(j + 1) * LANES].T
        for hh in range(LANES // HEAD_DIM):
            xh = qt[hh * HEAD_DIM:(hh + 1) * HEAD_DIM, :]
            xn = xh * lax.rsqrt(jnp.mean(xh * xh, axis=0, keepdims=True) + EPS) * gq
            partner = jnp.concatenate([xn[quarter:2 * quarter], xn[0:quarter],
                                       xn[3 * quarter:4 * quarter], xn[2 * quarter:3 * quarter]], axis=0)
            row0 = j * LANES + hh * HEAD_DIM
            qt_ref[row0:row0 + HEAD_DIM, :] = _bf16((xn * cost + partner * sint) * SCORE_SCALE)

    lane = lax.broadcasted_iota(jnp.int32, (1, LANES), 1)
    first_half = (lane % ROPE_AXIS_DIM) < (ROPE_AXIS_DIM // 2)
    kv = jnp.dot(h, w_ref[:, ATT_WIDTH:QKV_WIDTH], preferred_element_type=jnp.float32) * rstd
    kk = kv[:, 0:KV_WIDTH]
    kn = kk * lax.rsqrt(_group_mean_sq(kk, bd_ref[...]) + EPS) * gk_ref[...]
    k_ref[...] = _bf16(_rope_slab(kn, cos_ref[...], sin_ref[...], first_half))

    vt = _bf16(kv[:, KV_WIDTH:2 * KV_WIDTH].T)
    for c in range(TM // TK):
        for hd in range(ATT_KV_HEADS):
            vt_ref[c, hd, 0:HEAD_DIM, :] = vt[hd * HEAD_DIM:(hd + 1) * HEAD_DIM, c * TK:(c + 1) * TK]
            vt_ref[c, hd, HEAD_DIM:V_ROWS, :] = jnp.ones((V_ROWS - HEAD_DIM, TK), jnp.bfloat16)

    rest_ref[...] = jnp.dot(h, w_ref[:, QKV_WIDTH:D_IN], preferred_element_type=jnp.float32) * rstd


def _in_proj(x, g, w, gq, gk, bd, cos_t, sin_t, cos_tt, sin_tt):
    B, S, _ = x.shape
    ns = S // TM
    const = lambda b, s: (0, 0)
    return pl.pallas_call(
        _in_proj_kernel,
        grid=(B, ns),
        in_specs=[
            pl.BlockSpec((None, TM, D_MODEL), lambda b, s: (b, s, 0)),
            pl.BlockSpec((1, D_MODEL), const),
            pl.BlockSpec((D_MODEL, D_IN), const),
            pl.BlockSpec((HEAD_DIM, LANES), const),
            pl.BlockSpec((1, KV_WIDTH), const),
            pl.BlockSpec((KV_WIDTH, KV_WIDTH), const),
            pl.BlockSpec((TM, LANES), lambda b, s: (s, 0)),
            pl.BlockSpec((TM, LANES), lambda b, s: (s, 0)),
            pl.BlockSpec((HEAD_DIM, TM), lambda b, s: (0, s)),
            pl.BlockSpec((HEAD_DIM, TM), lambda b, s: (0, s)),
        ],
        out_specs=[
            pl.BlockSpec((None, ATT_WIDTH, TM), lambda b, s: (b, 0, s)),
            pl.BlockSpec((None, TM, KV_WIDTH), lambda b, s: (b, s, 0)),
            pl.BlockSpec((None, TM // TK, ATT_KV_HEADS, V_ROWS, TK), lambda b, s: (b, s, 0, 0, 0)),
            pl.BlockSpec((None, TM, REST_WIDTH), lambda b, s: (b, s, 0)),
        ],
        out_shape=[
            jax.ShapeDtypeStruct((B, ATT_WIDTH, S), jnp.bfloat16),
            jax.ShapeDtypeStruct((B, S, KV_WIDTH), jnp.bfloat16),
            jax.ShapeDtypeStruct((B, S // TK, ATT_KV_HEADS, V_ROWS, TK), jnp.bfloat16),
            jax.ShapeDtypeStruct((B, S, REST_WIDTH), jnp.float32),
        ],
        compiler_params=pltpu.CompilerParams(
            dimension_semantics=("arbitrary", "arbitrary"), vmem_limit_bytes=VMEM_LIMIT),
        name="in_proj",
    )(x, g, w, gq, gk, bd, cos_t, sin_t, cos_tt, sin_tt)


def _attention_kernel(qt_ref, k_ref, vt_ref, o_ref, qext_ref, s_ref, p_ref, acc_ref, m_ref, mx_ref):
    kv = pl.program_id(1)
    n = k_ref.shape[0] // TK
    for g in range(GQA):
        qg = qt_ref[g * HEAD_DIM:(g + 1) * HEAD_DIM, :]
        zero = jnp.zeros_like(qg)
        qext_ref[0:HEAD_DIM, g * TQ:(g + 1) * TQ] = jnp.where(kv == 0, qg, zero)
        qext_ref[HEAD_DIM:2 * HEAD_DIM, g * TQ:(g + 1) * TQ] = jnp.where(kv == 1, qg, zero)

    def scores(j, g, cols):
        kblk = k_ref[pl.ds(pl.multiple_of(j * TK, TK), TK), :]
        s = jnp.dot(kblk, qext_ref[:, cols], preferred_element_type=jnp.float32)
        s_ref[g] = s
        mx_ref[:, cols] = jnp.max(s, axis=0, keepdims=True)

    def values(j, g):
        return jnp.dot(vt_ref[j], p_ref[g], preferred_element_type=jnp.float32)

    groups = [(g, slice(g * LANE_GROUP, (g + 1) * LANE_GROUP)) for g in range(NQ // LANE_GROUP)]

    def stage(j, first=False, last=False):
        for g, cols in groups:
            if first:
                m_new = mx_ref[:, cols]
            else:
                pv = values(j - 1, g)
                m_old = m_ref[:, cols]
                m_new = jnp.maximum(m_old, mx_ref[:, cols])
                acc_ref[:, cols] = jnp.exp2(m_old - m_new) * (acc_ref[:, cols] + pv)
            m_ref[:, cols] = m_new
            p_ref[g] = _bf16(jnp.exp2(s_ref[g] - m_new))
            if not last:
                scores(j + 1, g, cols)

    acc_ref[...] = jnp.zeros_like(acc_ref)
    for g, cols in groups:
        scores(0, g, cols)
    stage(0, first=True)

    def step(j, carry):
        stage(j)
        return carry

    assert (n - 2) % ATT_UNROLL == 0
    lax.fori_loop(1, n - 1, step, 0, unroll=ATT_UNROLL)
    stage(n - 1, last=True)
    for g, cols in groups:
        acc_ref[:, cols] = acc_ref[:, cols] + values(n - 1, g)
    o = acc_ref[0:HEAD_DIM, :] / acc_ref[HEAD_DIM:HEAD_DIM + 1, :]
    o4 = jnp.concatenate([o[:, g * TQ:(g + 1) * TQ] for g in range(GQA)], axis=0)
    o_ref[...] = o4.T


def _attention(qt, k, vt):
    B, _, S = qt.shape
    nk = vt.shape[1]
    assert vt.shape[-1] == TK
    return pl.pallas_call(
        _attention_kernel,
        grid=(B, ATT_KV_HEADS, S // TQ),
        in_specs=[
            pl.BlockSpec((None, GQA * HEAD_DIM, TQ), lambda b, kv, qi: (b, kv, qi)),
            pl.BlockSpec((None, S, KV_WIDTH), lambda b, kv, qi: (b, 0, 0)),
            pl.BlockSpec((None, nk, None, V_ROWS, TK), lambda b, kv, qi: (b, 0, kv, 0, 0)),
        ],
        out_specs=pl.BlockSpec((None, TQ, GQA * HEAD_DIM), lambda b, kv, qi: (b, qi, kv)),
        out_shape=jax.ShapeDtypeStruct((B, S, ATT_WIDTH), jnp.float32),
        scratch_shapes=[
            pltpu.VMEM((KV_WIDTH, NQ), jnp.bfloat16),
            pltpu.VMEM((NQ // LANE_GROUP, TK, LANE_GROUP), jnp.float32),
            pltpu.VMEM((NQ // LANE_GROUP, TK, LANE_GROUP), jnp.bfloat16),
            pltpu.VMEM((V_ROWS, NQ), jnp.float32),
            pltpu.VMEM((1, NQ), jnp.float32),
            pltpu.VMEM((1, NQ), jnp.float32),
        ],
        compiler_params=pltpu.CompilerParams(
            dimension_semantics=("arbitrary", "arbitrary", "arbitrary"),
            vmem_limit_bytes=VMEM_LIMIT),
        name="attention",
    )(qt, k, vt)


def _silu(x):
    return x * jax.nn.sigmoid(x)


def _gelu(x):
    return 0.5 * x * (1.0 + lax.erf(x * (2.0 ** -0.5)))


def _layer_norm(x, g, b):
    mu = jnp.mean(x, axis=-1, keepdims=True)
    xc = x - mu
    var = jnp.mean(xc * xc, axis=-1, keepdims=True)
    return xc * lax.rsqrt(var + EPS) * g + b


def _glu(a):
    return a[:, :CONV_WIDTH] * jax.nn.sigmoid(a[:, CONV_WIDTH:])


CONV_ROWS = 64


def _post_kernel(x_ref, rest_ref, prev_ref, next_ref, att_ref, wout_ref, gpost_ref,
                 dw_ref, dwb_ref, cg_ref, cb_ref, sg_ref, sb_ref, sw_ref, sbias_ref,
                 o_ref, hext_ref, hsh_ref, cat_ref):
    si = pl.program_id(1)
    last = pl.num_programs(1) - 1

    cat_ref[:, 0:ATT_WIDTH] = _bf16(att_ref[...] * _silu(rest_ref[:, R_GATT:R_GATT + ATT_WIDTH]))

    hext_ref[0:HALO, :] = jnp.where(si > 0, _glu(prev_ref[...]), 0.0)
    hext_ref[HALO:HALO + TM, :] = _glu(rest_ref[:, R_ACONV:R_ACONV + 2 * CONV_WIDTH])
    hext_ref[HALO + TM:2 * HALO + TM, :] = jnp.where(si < last, _glu(next_ref[...]), 0.0)
    for r in range(1, SUBLANES):
        hsh_ref[r - 1] = hext_ref[r:r + TM + 2 * HALO - SUBLANES, :]
    for rb in range(TM // CONV_ROWS):
        acc = jnp.zeros((CONV_ROWS, CONV_WIDTH), jnp.float32) + dwb_ref[...]
        for j in range(CONV_KERNEL):
            r = (j + HALO - CONV_PAD) % SUBLANES
            start = rb * CONV_ROWS + j + HALO - CONV_PAD - r
            tap = (hext_ref[start:start + CONV_ROWS, :] if r == 0
                   else hsh_ref[r - 1, start:start + CONV_ROWS, :])
            acc = acc + tap * dw_ref[j:j + 1, :]
        rows = slice(rb * CONV_ROWS, (rb + 1) * CONV_ROWS)
        c = _silu(_layer_norm(acc, cg_ref[...], cb_ref[...]))
        c = c * _silu(rest_ref[rows, R_GCONV:R_GCONV + CONV_WIDTH])
        cat_ref[rows, ATT_WIDTH:ATT_WIDTH + CONV_WIDTH] = _bf16(c)

    lane = lax.broadcasted_iota(jnp.int32, (1, SG_WIDTH), 1)
    for c in range(TM // SG_CHUNK):
        rows = slice(c * SG_CHUNK, (c + 1) * SG_CHUNK)
        vln = _bf16(_layer_norm(_gelu(rest_ref[rows, R_V:R_V + SG_WIDTH]), sg_ref[...], sb_ref[...]))
        mixed = jnp.dot(sw_ref[SG_HEADS - 1], vln, preferred_element_type=jnp.float32)
        for hd in range(SG_HEADS - 1):
            mh = jnp.dot(sw_ref[hd], vln, preferred_element_type=jnp.float32)
            mixed = jnp.where(lane // HEAD_DIM == hd, mh, mixed)
        sgu = _gelu(rest_ref[rows, R_U:R_U + SG_WIDTH]) * (mixed + sbias_ref[...])
        sgu = sgu * _silu(rest_ref[rows, R_GSG:R_GSG + SG_WIDTH])
        cat_ref[rows, ATT_WIDTH + CONV_WIDTH:D_MIX] = _bf16(sgu)

    mix = jnp.dot(cat_ref[...], wout_ref[...], preferred_element_type=jnp.float32)
    ms = jnp.mean(mix * mix, axis=-1, keepdims=True)
    o_ref[...] = x_ref[...] + mix * lax.rsqrt(ms + EPS) * gpost_ref[...]


def _post(x, rest, att, wout, gpost, dw, dwb, cg, cb, sg, sb, sw, sbias):
    B, S, _ = x.shape
    ns = S // TM
    hb = TM // HALO
    const2 = lambda b, s: (0, 0)
    const3 = lambda b, s: (0, 0, 0)
    aconv_blk = R_ACONV // (2 * CONV_WIDTH)
    return pl.pallas_call(
        _post_kernel,
        grid=(B, ns),
        in_specs=[
            pl.BlockSpec((None, TM, D_MODEL), lambda b, s: (b, s, 0)),
            pl.BlockSpec((None, TM, REST_WIDTH), lambda b, s: (b, s, 0)),
            pl.BlockSpec((None, HALO, 2 * CONV_WIDTH),
                         lambda b, s: (b, jnp.maximum(s * hb - 1, 0), aconv_blk)),
            pl.BlockSpec((None, HALO, 2 * CONV_WIDTH),
                         lambda b, s: (b, jnp.minimum((s + 1) * hb, S // HALO - 1), aconv_blk)),
            pl.BlockSpec((None, TM, ATT_WIDTH), lambda b, s: (b, s, 0)),
            pl.BlockSpec((D_MIX, D_MODEL), const2),
            pl.BlockSpec((1, D_MODEL), const2),
            pl.BlockSpec((CONV_KERNEL, CONV_WIDTH), const2),
            pl.BlockSpec((1, CONV_WIDTH), const2),
            pl.BlockSpec((1, CONV_WIDTH), const2),
            pl.BlockSpec((1, CONV_WIDTH), const2),
            pl.BlockSpec((1, SG_WIDTH), const2),
            pl.BlockSpec((1, SG_WIDTH), const2),
            pl.BlockSpec((SG_HEADS, SG_CHUNK, SG_CHUNK), const3),
            pl.BlockSpec((SG_CHUNK, SG_WIDTH), const2),
        ],
        out_specs=pl.BlockSpec((None, TM, D_MODEL), lambda b, s: (b, s, 0)),
        out_shape=jax.ShapeDtypeStruct((B, S, D_MODEL), jnp.float32),
        scratch_shapes=[
            pltpu.VMEM((TM + 2 * HALO, CONV_WIDTH), jnp.float32),
            pltpu.VMEM((SUBLANES - 1, TM + 2 * HALO - SUBLANES, CONV_WIDTH), jnp.float32),
            pltpu.VMEM((TM, D_MIX), jnp.bfloat16),
        ],
        compiler_params=pltpu.CompilerParams(
            dimension_semantics=("arbitrary", "arbitrary"), vmem_limit_bytes=VMEM_LIMIT),
        name="post",
    )(x, rest, rest, rest, att, wout, gpost, dw, dwb, cg, cb, sg, sb, sw, sbias)


def _rope_tables(S):
    half = ROPE_AXIS_DIM // 2
    t = jnp.arange(S, dtype=jnp.int32)
    row = (t // GRID_W).astype(jnp.float32)
    col = (t % GRID_W).astype(jnp.float32)
    inv_freq = ROPE_THETA ** (-jnp.arange(half, dtype=jnp.float32) / half)
    ang_r = row[:, None] * inv_freq[None, :]
    ang_c = col[:, None] * inv_freq[None, :]
    cos_h = jnp.concatenate([jnp.cos(ang_r)] * 2 + [jnp.cos(ang_c)] * 2, axis=-1)
    sin_h = jnp.concatenate([-jnp.sin(ang_r), jnp.sin(ang_r), -jnp.sin(ang_c), jnp.sin(ang_c)], axis=-1)
    reps = LANES // HEAD_DIM
    return jnp.tile(cos_h, (1, reps)), jnp.tile(sin_h, (1, reps)), cos_h.T, sin_h.T


def kernel(x, pre_norm, post_norm, w_in, w_out, q_norm, k_norm, conv_dw, conv_dw_b,
           conv_ln_g, conv_ln_b, sg_ln_g, sg_ln_b, sg_w, sg_b):
    B, S, _ = x.shape
    depth = w_in.shape[0]
    cos_t, sin_t, cos_tt, sin_tt = _rope_tables(S)
    head_id = jnp.arange(KV_WIDTH, dtype=jnp.int32) // HEAD_DIM
    bd = _bf16(jnp.where(head_id[:, None] == head_id[None, :], 1.0 / HEAD_DIM, 0.0))
    row2 = lambda a: a.reshape(1, -1)
    for l in range(depth):
        qt, k, vt, rest = _in_proj(
            x, row2(pre_norm[l]), _bf16(w_in[l]),
            jnp.broadcast_to(q_norm[l][:, None], (HEAD_DIM, LANES)), row2(jnp.tile(k_norm[l], ATT_KV_HEADS)),
            bd, cos_t, sin_t, cos_tt, sin_tt)
        att = _attention(qt, k, vt)
        sbias = jnp.repeat(sg_b[l].T, HEAD_DIM, axis=1)
        x = _post(x, rest, att, _bf16(w_out[l]), row2(post_norm[l]),
                  conv_dw[l], row2(conv_dw_b[l]), row2(conv_ln_g[l]), row2(conv_ln_b[l]),
                  row2(sg_ln_g[l]), row2(sg_ln_b[l]), _bf16(sg_w[l]), sbias)
    return x
```

```python
import math

import jax
import jax.numpy as jnp
from jax import lax
from jax.experimental import pallas as pl
from jax.experimental.pallas import tpu as pltpu

D_MODEL = 1024
HEAD_DIM = 64
GRID_W = 64
EPS = 1e-6
ATT_HEADS = 8
ATT_KV_HEADS = 2
GQA = ATT_HEADS // ATT_KV_HEADS
ATT_WIDTH = ATT_HEADS * HEAD_DIM
KV_WIDTH = ATT_KV_HEADS * HEAD_DIM
ROPE_THETA = 10000.0
ROPE_AXIS_DIM = HEAD_DIM // 2
CONV_WIDTH = 256
CONV_KERNEL = 31
CONV_PAD = CONV_KERNEL // 2
SG_HEADS = 4
SG_WIDTH = SG_HEADS * HEAD_DIM
SG_CHUNK = 128
D_MIX = ATT_WIDTH + CONV_WIDTH + SG_WIDTH
QKV_WIDTH = ATT_WIDTH + 2 * KV_WIDTH
REST_WIDTH = ATT_WIDTH + 3 * CONV_WIDTH + 3 * SG_WIDTH
D_IN = QKV_WIDTH + REST_WIDTH

R_GATT = 0
R_ACONV = ATT_WIDTH
R_GCONV = R_ACONV + 2 * CONV_WIDTH
R_U = R_GCONV + CONV_WIDTH
R_V = R_U + SG_WIDTH
R_GSG = R_V + SG_WIDTH

LANES = 128
SUBLANES = 8
HALO = 16
TM = 512
TQ = 512
TK = 256
ATT_UNROLL = 31
NQ = GQA * TQ
LANE_GROUP = 256
SLAB_PAD = 8
V_ROWS = HEAD_DIM + 16
SCORE_SCALE = HEAD_DIM ** -0.5 * math.log2(math.e)
VMEM_LIMIT = 56 * 1024 * 1024


def _bf16(x):
    return x.astype(jnp.bfloat16)


def _group_mean_sq(x, bd):
    x2 = x * x
    hi = _bf16(x2)
    lo = _bf16(x2 - hi.astype(jnp.float32))
    return (jnp.dot(hi, bd, preferred_element_type=jnp.float32)
            + jnp.dot(lo, bd, preferred_element_type=jnp.float32))


def _rope_slab(xn, cos, sin_signed, first_half):
    fwd = pltpu.roll(xn, LANES - ROPE_AXIS_DIM // 2, 1)
    bwd = pltpu.roll(xn, ROPE_AXIS_DIM // 2, 1)
    partner = jnp.where(first_half, fwd, bwd)
    return xn * cos + partner * sin_signed


def _in_proj_kernel(x_ref, g_ref, w_ref, gq_ref, gk_ref, bd_ref, cos_ref, sin_ref, cost_ref, sint_ref,
                    qt_ref, k_ref, vt_ref, rest_ref):
    x = x_ref[...]
    rstd = lax.rsqrt(jnp.mean(x * x, axis=-1, keepdims=True) + EPS)
    h = _bf16(x * g_ref[...])

    q = jnp.dot(h, w_ref[:, 0:ATT_WIDTH], preferred_element_type=jnp.float32) * rstd
    cost = cost_ref[...]
    sint = sint_ref[...]
    gq = jnp.concatenate([gq_ref[...]] * (TM // LANES), axis=1)
    quarter = ROPE_AXIS_DIM // 2
    for j in range(ATT_WIDTH // LANES):
        qt = q[:, j * LANES:(j + 1) * LANES].T
        for hh in range(LANES // HEAD_DIM):
            xh = qt[hh * HEAD_DIM:(hh + 1) * HEAD_DIM, :]
            xn = xh * lax.rsqrt(jnp.mean(xh * xh, axis=0, keepdims=True) + EPS) * gq
            partner = jnp.concatenate([xn[quarter:2 * quarter], xn[0:quarter],
                                       xn[3 * quarter:4 * quarter], xn[2 * quarter:3 * quarter]], axis=0)
            row0 = j * LANES + hh * HEAD_DIM
            qt_ref[row0:row0 + HEAD_DIM, :] = _bf16((xn * cost + partner * sint) * SCORE_SCALE)

    lane = lax.broadcasted_iota(jnp.int32, (1, LANES), 1)
    first_half = (lane % ROPE_AXIS_DIM) < (ROPE_AXIS_DIM // 2)
    kv = jnp.dot(h, w_ref[:, ATT_WIDTH:QKV_WIDTH], preferred_element_type=jnp.float32) * rstd
    kk = kv[:, 0:KV_WIDTH]
    kn = kk * lax.rsqrt(_group_mean_sq(kk, bd_ref[...]) + EPS) * gk_ref[...]
    k_ref[...] = _bf16(_rope_slab(kn, cos_ref[...], sin_ref[...], first_half))

    vt = _bf16(kv[:, KV_WIDTH:2 * KV_WIDTH].T)
    for c in range(TM // TK):
        for hd in range(ATT_KV_HEADS):
            vt_ref[c, hd, 0:HEAD_DIM, :] = vt[hd * HEAD_DIM:(hd + 1) * HEAD_DIM, c * TK:(c + 1) * TK]
            vt_ref[c, hd, HEAD_DIM:V_ROWS, :] = jnp.ones((V_ROWS - HEAD_DIM, TK), jnp.bfloat16)

    rest_ref[...] = jnp.dot(h, w_ref[:, QKV_WIDTH:D_IN], preferred_element_type=jnp.float32) * rstd


def _in_proj(x, g, w, gq, gk, bd, cos_t, sin_t, cos_tt, sin_tt):
    B, S, _ = x.shape
    ns = S // TM
    const = lambda b, s: (0, 0)
    return pl.pallas_call(
        _in_proj_kernel,
        grid=(B, ns),
        in_specs=[
            pl.BlockSpec((None, TM, D_MODEL), lambda b, s: (b, s, 0)),
            pl.BlockSpec((1, D_MODEL), const),
            pl.BlockSpec((D_MODEL, D_IN), const),
            pl.BlockSpec((HEAD_DIM, LANES), const),
            pl.BlockSpec((1, KV_WIDTH), const),
            pl.BlockSpec((KV_WIDTH, KV_WIDTH), const),
            pl.BlockSpec((TM, LANES), lambda b, s: (s, 0)),
            pl.BlockSpec((TM, LANES), lambda b, s: (s, 0)),
            pl.BlockSpec((HEAD_DIM, TM), lambda b, s: (0, s)),
            pl.BlockSpec((HEAD_DIM, TM), lambda b, s: (0, s)),
        ],
        out_specs=[
            pl.BlockSpec((None, ATT_WIDTH, TM), lambda b, s: (b, 0, s)),
            pl.BlockSpec((None, TM, KV_WIDTH), lambda b, s: (b, s, 0)),
            pl.BlockSpec((None, TM // TK, ATT_KV_HEADS, V_ROWS, TK), lambda b, s: (b, s, 0, 0, 0)),
            pl.BlockSpec((None, TM, REST_WIDTH), lambda b, s: (b, s, 0)),
        ],
        out_shape=[
            jax.ShapeDtypeStruct((B, ATT_WIDTH, S), jnp.bfloat16),
            jax.ShapeDtypeStruct((B, S, KV_WIDTH), jnp.bfloat16),
            jax.ShapeDtypeStruct((B, S // TK, ATT_KV_HEADS, V_ROWS, TK), jnp.bfloat16),
            jax.ShapeDtypeStruct((B, S, REST_WIDTH), jnp.float32),
        ],
        compiler_params=pltpu.CompilerParams(
            dimension_semantics=("arbitrary", "arbitrary"), vmem_limit_bytes=VMEM_LIMIT),
        name="in_proj",
    )(x, g, w, gq, gk, bd, cos_t, sin_t, cos_tt, sin_tt)


def _attention_kernel(qt_ref, k_ref, vt_ref, o_ref, qext_ref, s_ref, p_ref, acc_ref, m_ref, mx_ref):
    kv = pl.program_id(1)
    n = k_ref.shape[0] // TK
    for g in range(GQA):
        qg = qt_ref[g * HEAD_DIM:(g + 1) * HEAD_DIM, :]
        zero = jnp.zeros_like(qg)
        qext_ref[0:HEAD_DIM, g * TQ:(g + 1) * TQ] = jnp.where(kv == 0, qg, zero)
        qext_ref[HEAD_DIM:2 * HEAD_DIM, g * TQ:(g + 1) * TQ] = jnp.where(kv == 1, qg, zero)

    def scores(j, g, cols):
        kblk = k_ref[pl.ds(pl.multiple_of(j * TK, TK), TK), :]
        s = jnp.dot(kblk, qext_ref[:, cols], preferred_element_type=jnp.float32)
        s_ref[g, 0:TK, :] = s
        mx_ref[:, cols] = jnp.max(s, axis=0, keepdims=True)

    def values(j, g):
        return jnp.dot(vt_ref[j], p_ref[g, 0:TK, :], preferred_element_type=jnp.float32)

    groups = [(g, slice(g * LANE_GROUP, (g + 1) * LANE_GROUP)) for g in range(NQ // LANE_GROUP)]

    def stage(j, first=False, last=False):
        for g, cols in groups:
            if first:
                m_new = mx_ref[:, cols]
            else:
                pv = values(j - 1, g)
                m_old = m_ref[:, cols]
                m_new = jnp.maximum(m_old, mx_ref[:, cols])
                acc_ref[:, cols] = jnp.exp2(m_old - m_new) * (acc_ref[:, cols] + pv)
            m_ref[:, cols] = m_new
            p_ref[g, 0:TK, :] = _bf16(jnp.exp2(s_ref[g, 0:TK, :] - m_new))
            if not last:
                scores(j + 1, g, cols)

    acc_ref[...] = jnp.zeros_like(acc_ref)
    for g, cols in groups:
        scores(0, g, cols)
    stage(0, first=True)

    def step(j, carry):
        stage(j)
        return carry

    assert (n - 2) % ATT_UNROLL == 0
    lax.fori_loop(1, n - 1, step, 0, unroll=ATT_UNROLL)
    stage(n - 1, last=True)
    for g, cols in groups:
        acc_ref[:, cols] = acc_ref[:, cols] + values(n - 1, g)
    o = acc_ref[0:HEAD_DIM, :] / acc_ref[HEAD_DIM:HEAD_DIM + 1, :]
    o4 = jnp.concatenate([o[:, g * TQ:(g + 1) * TQ] for g in range(GQA)], axis=0)
    o_ref[...] = o4.T


def _attention(qt, k, vt):
    B, _, S = qt.shape
    nk = vt.shape[1]
    assert vt.shape[-1] == TK
    return pl.pallas_call(
        _attention_kernel,
        grid=(B, ATT_KV_HEADS, S // TQ),
        in_specs=[
            pl.BlockSpec((None, GQA * HEAD_DIM, TQ), lambda b, kv, qi: (b, kv, qi)),
            pl.BlockSpec((None, S, KV_WIDTH), lambda b, kv, qi: (b, 0, 0)),
            pl.BlockSpec((None, nk, None, V_ROWS, TK), lambda b, kv, qi: (b, 0, kv, 0, 0)),
        ],
        out_specs=pl.BlockSpec((None, TQ, GQA * HEAD_DIM), lambda b, kv, qi: (b, qi, kv)),
        out_shape=jax.ShapeDtypeStruct((B, S, ATT_WIDTH), jnp.float32),
        scratch_shapes=[
            pltpu.VMEM((KV_WIDTH, NQ), jnp.bfloat16),
            pltpu.VMEM((NQ // LANE_GROUP, TK + SLAB_PAD, LANE_GROUP), jnp.float32),
            pltpu.VMEM((NQ // LANE_GROUP, TK + 2 * SLAB_PAD, LANE_GROUP), jnp.bfloat16),
            pltpu.VMEM((V_ROWS, NQ), jnp.float32),
            pltpu.VMEM((1, NQ), jnp.float32),
            pltpu.VMEM((1, NQ), jnp.float32),
        ],
        compiler_params=pltpu.CompilerParams(
            dimension_semantics=("arbitrary", "arbitrary", "arbitrary"),
            vmem_limit_bytes=VMEM_LIMIT),
        name="attention",
    )(qt, k, vt)


def _silu(x):
    return x * jax.nn.sigmoid(x)


def _gelu(x):
    return 0.5 * x * (1.0 + lax.erf(x * (2.0 ** -0.5)))


def _layer_norm(x, g, b):
    mu = jnp.mean(x, axis=-1, keepdims=True)
    xc = x - mu
    var = jnp.mean(xc * xc, axis=-1, keepdims=True)
    return xc * lax.rsqrt(var + EPS) * g + b


def _glu(a):
    return a[:, :CONV_WIDTH] * jax.nn.sigmoid(a[:, CONV_WIDTH:])


CONV_ROWS = 64


def _post_kernel(x_ref, rest_ref, prev_ref, next_ref, att_ref, wout_ref, gpost_ref,
                 dw_ref, dwb_ref, cg_ref, cb_ref, sg_ref, sb_ref, sw_ref, sbias_ref,
                 o_ref, hext_ref, hsh_ref, cat_ref):
    si = pl.program_id(1)
    last = pl.num_programs(1) - 1

    cat_ref[:, 0:ATT_WIDTH] = _bf16(att_ref[...] * _silu(rest_ref[:, R_GATT:R_GATT + ATT_WIDTH]))

    hext_ref[0:HALO, :] = jnp.where(si > 0, _glu(prev_ref[...]), 0.0)
    hext_ref[HALO:HALO + TM, :] = _glu(rest_ref[:, R_ACONV:R_ACONV + 2 * CONV_WIDTH])
    hext_ref[HALO + TM:2 * HALO + TM, :] = jnp.where(si < last, _glu(next_ref[...]), 0.0)
    for r in range(1, SUBLANES):
        hsh_ref[r - 1] = hext_ref[r:r + TM + 2 * HALO - SUBLANES, :]
    for rb in range(TM // CONV_ROWS):
        acc = jnp.zeros((CONV_ROWS, CONV_WIDTH), jnp.float32) + dwb_ref[...]
        for j in range(CONV_KERNEL):
            r = (j + HALO - CONV_PAD) % SUBLANES
            start = rb * CONV_ROWS + j + HALO - CONV_PAD - r
            tap = (hext_ref[start:start + CONV_ROWS, :] if r == 0
                   else hsh_ref[r - 1, start:start + CONV_ROWS, :])
            acc = acc + tap * dw_ref[j:j + 1, :]
        rows = slice(rb * CONV_ROWS, (rb + 1) * CONV_ROWS)
        c = _silu(_layer_norm(acc, cg_ref[...], cb_ref[...]))
        c = c * _silu(rest_ref[rows, R_GCONV:R_GCONV + CONV_WIDTH])
        cat_ref[rows, ATT_WIDTH:ATT_WIDTH + CONV_WIDTH] = _bf16(c)

    lane = lax.broadcasted_iota(jnp.int32, (1, SG_WIDTH), 1)
    for c in range(TM // SG_CHUNK):
        rows = slice(c * SG_CHUNK, (c + 1) * SG_CHUNK)
        vln = _bf16(_layer_norm(_gelu(rest_ref[rows, R_V:R_V + SG_WIDTH]), sg_ref[...], sb_ref[...]))
        mixed = jnp.dot(sw_ref[SG_HEADS - 1], vln, preferred_element_type=jnp.float32)
        for hd in range(SG_HEADS - 1):
            mh = jnp.dot(sw_ref[hd], vln, preferred_element_type=jnp.float32)
            mixed = jnp.where(lane // HEAD_DIM == hd, mh, mixed)
        sgu = _gelu(rest_ref[rows, R_U:R_U + SG_WIDTH]) * (mixed + sbias_ref[...])
        sgu = sgu * _silu(rest_ref[rows, R_GSG:R_GSG + SG_WIDTH])
        cat_ref[rows, ATT_WIDTH + CONV_WIDTH:D_MIX] = _bf16(sgu)

    mix = jnp.dot(cat_ref[...], wout_ref[...], preferred_element_type=jnp.float32)
    ms = jnp.mean(mix * mix, axis=-1, keepdims=True)
    o_ref[...] = x_ref[...] + mix * lax.rsqrt(ms + EPS) * gpost_ref[...]


def _post(x, rest, att, wout, gpost, dw, dwb, cg, cb, sg, sb, sw, sbias):
    B, S, _ = x.shape
    ns = S // TM
    hb = TM // HALO
    const2 = lambda b, s: (0, 0)
    const3 = lambda b, s: (0, 0, 0)
    aconv_blk = R_ACONV // (2 * CONV_WIDTH)
    return pl.pallas_call(
        _post_kernel,
        grid=(B, ns),
        in_specs=[
            pl.BlockSpec((None, TM, D_MODEL), lambda b, s: (b, s, 0)),
            pl.BlockSpec((None, TM, REST_WIDTH), lambda b, s: (b, s, 0)),
            pl.BlockSpec((None, HALO, 2 * CONV_WIDTH),
                         lambda b, s: (b, jnp.maximum(s * hb - 1, 0), aconv_blk)),
            pl.BlockSpec((None, HALO, 2 * CONV_WIDTH),
                         lambda b, s: (b, jnp.minimum((s + 1) * hb, S // HALO - 1), aconv_blk)),
            pl.BlockSpec((None, TM, ATT_WIDTH), lambda b, s: (b, s, 0)),
            pl.BlockSpec((D_MIX, D_MODEL), const2),
            pl.BlockSpec((1, D_MODEL), const2),
            pl.BlockSpec((CONV_KERNEL, CONV_WIDTH), const2),
            pl.BlockSpec((1, CONV_WIDTH), const2),
            pl.BlockSpec((1, CONV_WIDTH), const2),
            pl.BlockSpec((1, CONV_WIDTH), const2),
            pl.BlockSpec((1, SG_WIDTH), const2),
            pl.BlockSpec((1, SG_WIDTH), const2),
            pl.BlockSpec((SG_HEADS, SG_CHUNK, SG_CHUNK), const3),
            pl.BlockSpec((SG_CHUNK, SG_WIDTH), const2),
        ],
        out_specs=pl.BlockSpec((None, TM, D_MODEL), lambda b, s: (b, s, 0)),
        out_shape=jax.ShapeDtypeStruct((B, S, D_MODEL), jnp.float32),
        scratch_shapes=[
            pltpu.VMEM((TM + 2 * HALO, CONV_WIDTH), jnp.float32),
            pltpu.VMEM((SUBLANES - 1, TM + 2 * HALO - SUBLANES, CONV_WIDTH), jnp.float32),
            pltpu.VMEM((TM, D_MIX), jnp.bfloat16),
        ],
        compiler_params=pltpu.CompilerParams(
            dimension_semantics=("arbitrary", "arbitrary"), vmem_limit_bytes=VMEM_LIMIT),
        name="post",
    )(x, rest, rest, rest, att, wout, gpost, dw, dwb, cg, cb, sg, sb, sw, sbias)


def _rope_tables(S):
    half = ROPE_AXIS_DIM // 2
    t = jnp.arange(S, dtype=jnp.int32)
    row = (t // GRID_W).astype(jnp.float32)
    col = (t % GRID_W).astype(jnp.float32)
    inv_freq = ROPE_THETA ** (-jnp.arange(half, dtype=jnp.float32) / half)
    ang_r = row[:, None] * inv_freq[None, :]
    ang_c = col[:, None] * inv_freq[None, :]
    cos_h = jnp.concatenate([jnp.cos(ang_r)] * 2 + [jnp.cos(ang_c)] * 2, axis=-1)
    sin_h = jnp.concatenate([-jnp.sin(ang_r), jnp.sin(ang_r), -jnp.sin(ang_c), jnp.sin(ang_c)], axis=-1)
    reps = LANES // HEAD_DIM
    return jnp.tile(cos_h, (1, reps)), jnp.tile(sin_h, (1, reps)), cos_h.T, sin_h.T


def kernel(x, pre_norm, post_norm, w_in, w_out, q_norm, k_norm, conv_dw, conv_dw_b,
           conv_ln_g, conv_ln_b, sg_ln_g, sg_ln_b, sg_w, sg_b):
    B, S, _ = x.shape
    depth = w_in.shape[0]
    cos_t, sin_t, cos_tt, sin_tt = _rope_tables(S)
    head_id = jnp.arange(KV_WIDTH, dtype=jnp.int32) // HEAD_DIM
    bd = _bf16(jnp.where(head_id[:, None] == head_id[None, :], 1.0 / HEAD_DIM, 0.0))
    row2 = lambda a: a.reshape(1, -1)
    for l in range(depth):
        qt, k, vt, rest = _in_proj(
            x, row2(pre_norm[l]), _bf16(w_in[l]),
            jnp.broadcast_to(q_norm[l][:, None], (HEAD_DIM, LANES)), row2(jnp.tile(k_norm[l], ATT_KV_HEADS)),
            bd, cos_t, sin_t, cos_tt, sin_tt)
        att = _attention(qt, k, vt)
        sbias = jnp.repeat(sg_b[l].T, HEAD_DIM, axis=1)
        x = _post(x, rest, att, _bf16(w_out[l]), row2(post_norm[l]),
                  conv_dw[l], row2(conv_dw_b[l]), row2(conv_ln_g[l]), row2(conv_ln_b[l]),
                  row2(sg_ln_g[l]), row2(sg_ln_b[l]), _bf16(sg_w[l]), sbias)
    return x
```

```python
import math

import jax
import jax.numpy as jnp
from jax import lax
from jax.experimental import pallas as pl
from jax.experimental.pallas import tpu as pltpu

D_MODEL = 1024
HEAD_DIM = 64
GRID_W = 64
EPS = 1e-6
ATT_HEADS = 8
ATT_KV_HEADS = 2
GQA = ATT_HEADS // ATT_KV_HEADS
ATT_WIDTH = ATT_HEADS * HEAD_DIM
KV_WIDTH = ATT_KV_HEADS * HEAD_DIM
ROPE_THETA = 10000.0
ROPE_AXIS_DIM = HEAD_DIM // 2
CONV_WIDTH = 256
CONV_KERNEL = 31
CONV_PAD = CONV_KERNEL // 2
SG_HEADS = 4
SG_WIDTH = SG_HEADS * HEAD_DIM
SG_CHUNK = 128
D_MIX = ATT_WIDTH + CONV_WIDTH + SG_WIDTH
QKV_WIDTH = ATT_WIDTH + 2 * KV_WIDTH
REST_WIDTH = ATT_WIDTH + 3 * CONV_WIDTH + 3 * SG_WIDTH
D_IN = QKV_WIDTH + REST_WIDTH

R_GATT = 0
R_ACONV = ATT_WIDTH
R_GCONV = R_ACONV + 2 * CONV_WIDTH
R_U = R_GCONV + CONV_WIDTH
R_V = R_U + SG_WIDTH
R_GSG = R_V + SG_WIDTH

LANES = 128
SUBLANES = 8
HALO = 16
TM = 512
TM_IN = 1024
TQ = 512
TK = 256
ATT_UNROLL = 31
NQ = GQA * TQ
LANE_GROUP = 256
V_ROWS = HEAD_DIM + 16
SCORE_SCALE = HEAD_DIM ** -0.5 * math.log2(math.e)
VMEM_LIMIT = 56 * 1024 * 1024


def _bf16(x):
    return x.astype(jnp.bfloat16)


def _group_mean_sq(x, bd):
    x2 = x * x
    hi = _bf16(x2)
    lo = _bf16(x2 - hi.astype(jnp.float32))
    return (jnp.dot(hi, bd, preferred_element_type=jnp.float32)
            + jnp.dot(lo, bd, preferred_element_type=jnp.float32))


def _rope_slab(xn, cos, sin_signed, first_half):
    fwd = pltpu.roll(xn, LANES - ROPE_AXIS_DIM // 2, 1)
    bwd = pltpu.roll(xn, ROPE_AXIS_DIM // 2, 1)
    partner = jnp.where(first_half, fwd, bwd)
    return xn * cos + partner * sin_signed


def _in_proj_kernel(x_ref, g_ref, w_ref, gq_ref, gk_ref, bd_ref, cos_ref, sin_ref, cost_ref, sint_ref,
                    qt_ref, k_ref, vt_ref, rest_ref):
    x = x_ref[...]
    rstd = lax.rsqrt(jnp.mean(x * x, axis=-1, keepdims=True) + EPS)
    h = _bf16(x * g_ref[...])

    q = jnp.dot(h, w_ref[:, 0:ATT_WIDTH], preferred_element_type=jnp.float32) * rstd
    cost = cost_ref[...]
    sint = sint_ref[...]
    gq = jnp.concatenate([gq_ref[...]] * (TM_IN // LANES), axis=1)
    quarter = ROPE_AXIS_DIM // 2
    for j in range(ATT_WIDTH // LANES):
        qt = q[:, j * LANES:(j + 1) * LANES].T
        for hh in range(LANES // HEAD_DIM):
            xh = qt[hh * HEAD_DIM:(hh + 1) * HEAD_DIM, :]
            xn = xh * lax.rsqrt(jnp.mean(xh * xh, axis=0, keepdims=True) + EPS) * gq
            partner = jnp.concatenate([xn[quarter:2 * quarter], xn[0:quarter],
                                       xn[3 * quarter:4 * quarter], xn[2 * quarter:3 * quarter]], axis=0)
            row0 = j * LANES + hh * HEAD_DIM
            qt_ref[row0:row0 + HEAD_DIM, :] = _bf16((xn * cost + partner * sint) * SCORE_SCALE)

    lane = lax.broadcasted_iota(jnp.int32, (1, LANES), 1)
    first_half = (lane % ROPE_AXIS_DIM) < (ROPE_AXIS_DIM // 2)
    kv = jnp.dot(h, w_ref[:, ATT_WIDTH:QKV_WIDTH], preferred_element_type=jnp.float32) * rstd
    kk = kv[:, 0:KV_WIDTH]
    kn = kk * lax.rsqrt(_group_mean_sq(kk, bd_ref[...]) + EPS) * gk_ref[...]
    k_ref[...] = _bf16(_rope_slab(kn, cos_ref[...], sin_ref[...], first_half))

    vt = _bf16(kv[:, KV_WIDTH:2 * KV_WIDTH].T)
    for c in range(TM_IN // TK):
        for hd in range(ATT_KV_HEADS):
            vt_ref[c, hd, 0:HEAD_DIM, :] = vt[hd * HEAD_DIM:(hd + 1) * HEAD_DIM, c * TK:(c + 1) * TK]
            vt_ref[c, hd, HEAD_DIM:V_ROWS, :] = jnp.ones((V_ROWS - HEAD_DIM, TK), jnp.bfloat16)

    rest_ref[...] = jnp.dot(h, w_ref[:, QKV_WIDTH:D_IN], preferred_element_type=jnp.float32) * rstd


def _in_proj(x, g, w, gq, gk, bd, cos_t, sin_t, cos_tt, sin_tt):
    B, S, _ = x.shape
    ns = S // TM_IN
    const = lambda b, s: (0, 0)
    return pl.pallas_call(
        _in_proj_kernel,
        grid=(B, ns),
        in_specs=[
            pl.BlockSpec((None, TM_IN, D_MODEL), lambda b, s: (b, s, 0)),
            pl.BlockSpec((1, D_MODEL), const),
            pl.BlockSpec((D_MODEL, D_IN), const),
            pl.BlockSpec((HEAD_DIM, LANES), const),
            pl.BlockSpec((1, KV_WIDTH), const),
            pl.BlockSpec((KV_WIDTH, KV_WIDTH), const),
            pl.BlockSpec((TM_IN, LANES), lambda b, s: (s, 0)),
            pl.BlockSpec((TM_IN, LANES), lambda b, s: (s, 0)),
            pl.BlockSpec((HEAD_DIM, TM_IN), lambda b, s: (0, s)),
            pl.BlockSpec((HEAD_DIM, TM_IN), lambda b, s: (0, s)),
        ],
        out_specs=[
            pl.BlockSpec((None, ATT_WIDTH, TM_IN), lambda b, s: (b, 0, s)),
            pl.BlockSpec((None, TM_IN, KV_WIDTH), lambda b, s: (b, s, 0)),
            pl.BlockSpec((None, TM_IN // TK, ATT_KV_HEADS, V_ROWS, TK), lambda b, s: (b, s, 0, 0, 0)),
            pl.BlockSpec((None, TM_IN, REST_WIDTH), lambda b, s: (b, s, 0)),
        ],
        out_shape=[
            jax.ShapeDtypeStruct((B, ATT_WIDTH, S), jnp.bfloat16),
            jax.ShapeDtypeStruct((B, S, KV_WIDTH), jnp.bfloat16),
            jax.ShapeDtypeStruct((B, S // TK, ATT_KV_HEADS, V_ROWS, TK), jnp.bfloat16),
            jax.ShapeDtypeStruct((B, S, REST_WIDTH), jnp.float32),
        ],
        compiler_params=pltpu.CompilerParams(
            dimension_semantics=("arbitrary", "arbitrary"), vmem_limit_bytes=VMEM_LIMIT),
        name="in_proj",
    )(x, g, w, gq, gk, bd, cos_t, sin_t, cos_tt, sin_tt)


def _attention_kernel(qt_ref, k_ref, vt_ref, o_ref, qext_ref, s_ref, p_ref, acc_ref, m_ref, mx_ref):
    kv = pl.program_id(1)
    n = k_ref.shape[0] // TK
    for g in range(GQA):
        qg = qt_ref[g * HEAD_DIM:(g + 1) * HEAD_DIM, :]
        zero = jnp.zeros_like(qg)
        qext_ref[0:HEAD_DIM, g * TQ:(g + 1) * TQ] = jnp.where(kv == 0, qg, zero)
        qext_ref[HEAD_DIM:2 * HEAD_DIM, g * TQ:(g + 1) * TQ] = jnp.where(kv == 1, qg, zero)

    def scores(j, g, cols):
        kblk = k_ref[pl.ds(pl.multiple_of(j * TK, TK), TK), :]
        s = jnp.dot(kblk, qext_ref[:, cols], preferred_element_type=jnp.float32)
        s_ref[g] = s
        mx_ref[:, cols] = jnp.max(s, axis=0, keepdims=True)

    def values(j, g):
        return jnp.dot(vt_ref[j], p_ref[g], preferred_element_type=jnp.float32)

    groups = [(g, slice(g * LANE_GROUP, (g + 1) * LANE_GROUP)) for g in range(NQ // LANE_GROUP)]

    def stage(j, first=False, last=False):
        for g, cols in groups:
            if first:
                m_new = mx_ref[:, cols]
            else:
                pv = values(j - 1, g)
                m_old = m_ref[:, cols]
                m_new = jnp.maximum(m_old, mx_ref[:, cols])
                acc_ref[:, cols] = jnp.exp2(m_old - m_new) * (acc_ref[:, cols] + pv)
            m_ref[:, cols] = m_new
            p_ref[g] = _bf16(jnp.exp2(s_ref[g] - m_new))
            if not last:
                scores(j + 1, g, cols)

    acc_ref[...] = jnp.zeros_like(acc_ref)
    for g, cols in groups:
        scores(0, g, cols)
    stage(0, first=True)

    def step(j, carry):
        stage(j)
        return carry

    assert (n - 2) % ATT_UNROLL == 0
    lax.fori_loop(1, n - 1, step, 0, unroll=ATT_UNROLL)
    stage(n - 1, last=True)
    for g, cols in groups:
        acc_ref[:, cols] = acc_ref[:, cols] + values(n - 1, g)
    o = acc_ref[0:HEAD_DIM, :] / acc_ref[HEAD_DIM:HEAD_DIM + 1, :]
    o4 = jnp.concatenate([o[:, g * TQ:(g + 1) * TQ] for g in range(GQA)], axis=0)
    o_ref[...] = o4.T


def _attention(qt, k, vt):
    B, _, S = qt.shape
    nk = vt.shape[1]
    assert vt.shape[-1] == TK
    return pl.pallas_call(
        _attention_kernel,
        grid=(B, ATT_KV_HEADS, S // TQ),
        in_specs=[
            pl.BlockSpec((None, GQA * HEAD_DIM, TQ), lambda b, kv, qi: (b, kv, qi)),
            pl.BlockSpec((None, S, KV_WIDTH), lambda b, kv, qi: (b, 0, 0)),
            pl.BlockSpec((None, nk, None, V_ROWS, TK), lambda b, kv, qi: (b, 0, kv, 0, 0)),
        ],
        out_specs=pl.BlockSpec((None, TQ, GQA * HEAD_DIM), lambda b, kv, qi: (b, qi, kv)),
        out_shape=jax.ShapeDtypeStruct((B, S, ATT_WIDTH), jnp.float32),
        scratch_shapes=[
            pltpu.VMEM((KV_WIDTH, NQ), jnp.bfloat16),
            pltpu.VMEM((NQ // LANE_GROUP, TK, LANE_GROUP), jnp.float32),
            pltpu.VMEM((NQ // LANE_GROUP, TK, LANE_GROUP), jnp.bfloat16),
            pltpu.VMEM((V_ROWS, NQ), jnp.float32),
            pltpu.VMEM((1, NQ), jnp.float32),
            pltpu.VMEM((1, NQ), jnp.float32),
        ],
        compiler_params=pltpu.CompilerParams(
            dimension_semantics=("arbitrary", "arbitrary", "arbitrary"),
            vmem_limit_bytes=VMEM_LIMIT),
        name="attention",
    )(qt, k, vt)


def _silu(x):
    return x * jax.nn.sigmoid(x)


def _gelu(x):
    return 0.5 * x * (1.0 + lax.erf(x * (2.0 ** -0.5)))


def _layer_norm(x, g, b):
    mu = jnp.mean(x, axis=-1, keepdims=True)
    xc = x - mu
    var = jnp.mean(xc * xc, axis=-1, keepdims=True)
    return xc * lax.rsqrt(var + EPS) * g + b


def _glu(a):
    return a[:, :CONV_WIDTH] * jax.nn.sigmoid(a[:, CONV_WIDTH:])


CONV_ROWS = 64


def _post_kernel(x_ref, rest_ref, prev_ref, next_ref, att_ref, wout_ref, gpost_ref,
                 dw_ref, dwb_ref, cg_ref, cb_ref, sg_ref, sb_ref, sw_ref, sbias_ref,
                 o_ref, hext_ref, hsh_ref, cat_ref):
    si = pl.program_id(1)
    last = pl.num_programs(1) - 1

    cat_ref[:, 0:ATT_WIDTH] = _bf16(att_ref[...] * _silu(rest_ref[:, R_GATT:R_GATT + ATT_WIDTH]))

    hext_ref[0:HALO, :] = jnp.where(si > 0, _glu(prev_ref[...]), 0.0)
    hext_ref[HALO:HALO + TM, :] = _glu(rest_ref[:, R_ACONV:R_ACONV + 2 * CONV_WIDTH])
    hext_ref[HALO + TM:2 * HALO + TM, :] = jnp.where(si < last, _glu(next_ref[...]), 0.0)
    for r in range(1, SUBLANES):
        hsh_ref[r - 1] = hext_ref[r:r + TM + 2 * HALO - SUBLANES, :]
    for rb in range(TM // CONV_ROWS):
        acc = jnp.zeros((CONV_ROWS, CONV_WIDTH), jnp.float32) + dwb_ref[...]
        for j in range(CONV_KERNEL):
            r = (j + HALO - CONV_PAD) % SUBLANES
            start = rb * CONV_ROWS + j + HALO - CONV_PAD - r
            tap = (hext_ref[start:start + CONV_ROWS, :] if r == 0
                   else hsh_ref[r - 1, start:start + CONV_ROWS, :])
            acc = acc + tap * dw_ref[j:j + 1, :]
        rows = slice(rb * CONV_ROWS, (rb + 1) * CONV_ROWS)
        c = _silu(_layer_norm(acc, cg_ref[...], cb_ref[...]))
        c = c * _silu(rest_ref[rows, R_GCONV:R_GCONV + CONV_WIDTH])
        cat_ref[rows, ATT_WIDTH:ATT_WIDTH + CONV_WIDTH] = _bf16(c)

    lane = lax.broadcasted_iota(jnp.int32, (1, SG_WIDTH), 1)
    for c in range(TM // SG_CHUNK):
        rows = slice(c * SG_CHUNK, (c + 1) * SG_CHUNK)
        vln = _bf16(_layer_norm(_gelu(rest_ref[rows, R_V:R_V + SG_WIDTH]), sg_ref[...], sb_ref[...]))
        mixed = jnp.dot(sw_ref[SG_HEADS - 1], vln, preferred_element_type=jnp.float32)
        for hd in range(SG_HEADS - 1):
            mh = jnp.dot(sw_ref[hd], vln, preferred_element_type=jnp.float32)
            mixed = jnp.where(lane // HEAD_DIM == hd, mh, mixed)
        sgu = _gelu(rest_ref[rows, R_U:R_U + SG_WIDTH]) * (mixed + sbias_ref[...])
        sgu = sgu * _silu(rest_ref[rows, R_GSG:R_GSG + SG_WIDTH])
        cat_ref[rows, ATT_WIDTH + CONV_WIDTH:D_MIX] = _bf16(sgu)

    mix = jnp.dot(cat_ref[...], wout_ref[...], preferred_element_type=jnp.float32)
    ms = jnp.mean(mix * mix, axis=-1, keepdims=True)
    o_ref[...] = x_ref[...] + mix * lax.rsqrt(ms + EPS) * gpost_ref[...]


def _post(x, rest, att, wout, gpost, dw, dwb, cg, cb, sg, sb, sw, sbias):
    B, S, _ = x.shape
    ns = S // TM
    hb = TM // HALO
    const2 = lambda b, s: (0, 0)
    const3 = lambda b, s: (0, 0, 0)
    aconv_blk = R_ACONV // (2 * CONV_WIDTH)
    return pl.pallas_call(
        _post_kernel,
        grid=(B, ns),
        in_specs=[
            pl.BlockSpec((None, TM, D_MODEL), lambda b, s: (b, s, 0)),
            pl.BlockSpec((None, TM, REST_WIDTH), lambda b, s: (b, s, 0)),
            pl.BlockSpec((None, HALO, 2 * CONV_WIDTH),
                         lambda b, s: (b, jnp.maximum(s * hb - 1, 0), aconv_blk)),
            pl.BlockSpec((None, HALO, 2 * CONV_WIDTH),
                         lambda b, s: (b, jnp.minimum((s + 1) * hb, S // HALO - 1), aconv_blk)),
            pl.BlockSpec((None, TM, ATT_WIDTH), lambda b, s: (b, s, 0)),
            pl.BlockSpec((D_MIX, D_MODEL), const2),
            pl.BlockSpec((1, D_MODEL), const2),
            pl.BlockSpec((CONV_KERNEL, CONV_WIDTH), const2),
            pl.BlockSpec((1, CONV_WIDTH), const2),
            pl.BlockSpec((1, CONV_WIDTH), const2),
            pl.BlockSpec((1, CONV_WIDTH), const2),
            pl.BlockSpec((1, SG_WIDTH), const2),
            pl.BlockSpec((1, SG_WIDTH), const2),
            pl.BlockSpec((SG_HEADS, SG_CHUNK, SG_CHUNK), const3),
            pl.BlockSpec((SG_CHUNK, SG_WIDTH), const2),
        ],
        out_specs=pl.BlockSpec((None, TM, D_MODEL), lambda b, s: (b, s, 0)),
        out_shape=jax.ShapeDtypeStruct((B, S, D_MODEL), jnp.float32),
        scratch_shapes=[
            pltpu.VMEM((TM + 2 * HALO, CONV_WIDTH), jnp.float32),
            pltpu.VMEM((SUBLANES - 1, TM + 2 * HALO - SUBLANES, CONV_WIDTH), jnp.float32),
            pltpu.VMEM((TM, D_MIX), jnp.bfloat16),
        ],
        compiler_params=pltpu.CompilerParams(
            dimension_semantics=("arbitrary", "arbitrary"), vmem_limit_bytes=VMEM_LIMIT),
        name="post",
    )(x, rest, rest, rest, att, wout, gpost, dw, dwb, cg, cb, sg, sb, sw, sbias)


def _rope_tables(S):
    half = ROPE_AXIS_DIM // 2
    t = jnp.arange(S, dtype=jnp.int32)
    row = (t // GRID_W).astype(jnp.float32)
    col = (t % GRID_W).astype(jnp.float32)
    inv_freq = ROPE_THETA ** (-jnp.arange(half, dtype=jnp.float32) / half)
    ang_r = row[:, None] * inv_freq[None, :]
    ang_c = col[:, None] * inv_freq[None, :]
    cos_h = jnp.concatenate([jnp.cos(ang_r)] * 2 + [jnp.cos(ang_c)] * 2, axis=-1)
    sin_h = jnp.concatenate([-jnp.sin(ang_r), jnp.sin(ang_r), -jnp.sin(ang_c), jnp.sin(ang_c)], axis=-1)
    reps = LANES // HEAD_DIM
    return jnp.tile(cos_h, (1, reps)), jnp.tile(sin_h, (1, reps)), cos_h.T, sin_h.T


def kernel(x, pre_norm, post_norm, w_in, w_out, q_norm, k_norm, conv_dw, conv_dw_b,
           conv_ln_g, conv_ln_b, sg_ln_g, sg_ln_b, sg_w, sg_b):
    B, S, _ = x.shape
    depth = w_in.shape[0]
    cos_t, sin_t, cos_tt, sin_tt = _rope_tables(S)
    head_id = jnp.arange(KV_WIDTH, dtype=jnp.int32) // HEAD_DIM
    bd = _bf16(jnp.where(head_id[:, None] == head_id[None, :], 1.0 / HEAD_DIM, 0.0))
    row2 = lambda a: a.reshape(1, -1)
    for l in range(depth):
        qt, k, vt, rest = _in_proj(
            x, row2(pre_norm[l]), _bf16(w_in[l]),
            jnp.broadcast_to(q_norm[l][:, None], (HEAD_DIM, LANES)), row2(jnp.tile(k_norm[l], ATT_KV_HEADS)),
            bd, cos_t, sin_t, cos_tt, sin_tt)
        att = _attention(qt, k, vt)
        sbias = jnp.repeat(sg_b[l].T, HEAD_DIM, axis=1)
        x = _post(x, rest, att, _bf16(w_out[l]), row2(post_norm[l]),
                  conv_dw[l], row2(conv_dw_b[l]), row2(conv_ln_g[l]), row2(conv_ln_b[l]),
                  row2(sg_ln_g[l]), row2(sg_ln_b[l]), _bf16(sg_w[l]), sbias)
    return x
```

```python
import math

import jax
import jax.numpy as jnp
from jax import lax
from jax.experimental import pallas as pl
from jax.experimental.pallas import tpu as pltpu

D_MODEL = 1024
HEAD_DIM = 64
GRID_W = 64
EPS = 1e-6
ATT_HEADS = 8
ATT_KV_HEADS = 2
GQA = ATT_HEADS // ATT_KV_HEADS
ATT_WIDTH = ATT_HEADS * HEAD_DIM
KV_WIDTH = ATT_KV_HEADS * HEAD_DIM
ROPE_THETA = 10000.0
ROPE_AXIS_DIM = HEAD_DIM // 2
CONV_WIDTH = 256
CONV_KERNEL = 31
CONV_PAD = CONV_KERNEL // 2
SG_HEADS = 4
SG_WIDTH = SG_HEADS * HEAD_DIM
SG_CHUNK = 128
D_MIX = ATT_WIDTH + CONV_WIDTH + SG_WIDTH
QKV_WIDTH = ATT_WIDTH + 2 * KV_WIDTH
REST_WIDTH = ATT_WIDTH + 3 * CONV_WIDTH + 3 * SG_WIDTH
D_IN = QKV_WIDTH + REST_WIDTH

R_GATT = 0
R_ACONV = ATT_WIDTH
R_GCONV = R_ACONV + 2 * CONV_WIDTH
R_U = R_GCONV + CONV_WIDTH
R_V = R_U + SG_WIDTH
R_GSG = R_V + SG_WIDTH

LANES = 128
SUBLANES = 8
HALO = 16
TM = 512
TM_IN = 1024
TQ = 512
TK = 256
ATT_UNROLL = 31
NQ = GQA * TQ
LANE_GROUP = 256
V_ROWS = HEAD_DIM + 16
SCORE_SCALE = HEAD_DIM ** -0.5 * math.log2(math.e)
VMEM_LIMIT = 56 * 1024 * 1024


def _bf16(x):
    return x.astype(jnp.bfloat16)


def _group_mean_sq(x, bd):
    x2 = x * x
    hi = _bf16(x2)
    lo = _bf16(x2 - hi.astype(jnp.float32))
    return (jnp.dot(hi, bd, preferred_element_type=jnp.float32)
            + jnp.dot(lo, bd, preferred_element_type=jnp.float32))


def _rope_slab(xn, cos, sin_signed, first_half):
    fwd = pltpu.roll(xn, LANES - ROPE_AXIS_DIM // 2, 1)
    bwd = pltpu.roll(xn, ROPE_AXIS_DIM // 2, 1)
    partner = jnp.where(first_half, fwd, bwd)
    return xn * cos + partner * sin_signed


def _in_proj_kernel(x_ref, g_ref, w_ref, gq_ref, gk_ref, bd_ref, cos_ref, sin_ref, cost_ref, sint_ref,
                    qt_ref, k_ref, vt_ref, rest_ref):
    x = x_ref[...]
    rstd = lax.rsqrt(jnp.mean(x * x, axis=-1, keepdims=True) + EPS)
    h = _bf16(x * g_ref[...])

    q = jnp.dot(h, w_ref[:, 0:ATT_WIDTH], preferred_element_type=jnp.float32) * rstd
    cost = cost_ref[...]
    sint = sint_ref[...]
    gq = jnp.concatenate([gq_ref[...]] * (TM_IN // LANES), axis=1)
    quarter = ROPE_AXIS_DIM // 2
    for j in range(ATT_WIDTH // LANES):
        qt = q[:, j * LANES:(j + 1) * LANES].T
        for hh in range(LANES // HEAD_DIM):
            xh = qt[hh * HEAD_DIM:(hh + 1) * HEAD_DIM, :]
            xn = xh * lax.rsqrt(jnp.mean(xh * xh, axis=0, keepdims=True) + EPS) * gq
            partner = jnp.concatenate([xn[quarter:2 * quarter], xn[0:quarter],
                                       xn[3 * quarter:4 * quarter], xn[2 * quarter:3 * quarter]], axis=0)
            row0 = j * LANES + hh * HEAD_DIM
            qt_ref[row0:row0 + HEAD_DIM, :] = _bf16((xn * cost + partner * sint) * SCORE_SCALE)

    lane = lax.broadcasted_iota(jnp.int32, (1, LANES), 1)
    first_half = (lane % ROPE_AXIS_DIM) < (ROPE_AXIS_DIM // 2)
    kv = jnp.dot(h, w_ref[:, ATT_WIDTH:QKV_WIDTH], preferred_element_type=jnp.float32) * rstd
    kk = kv[:, 0:KV_WIDTH]
    kn = kk * lax.rsqrt(_group_mean_sq(kk, bd_ref[...]) + EPS) * gk_ref[...]
    k_ref[...] = _bf16(_rope_slab(kn, cos_ref[...], sin_ref[...], first_half))

    vt = _bf16(kv[:, KV_WIDTH:2 * KV_WIDTH].T)
    for c in range(TM_IN // TK):
        for hd in range(ATT_KV_HEADS):
            vt_ref[c, hd, 0:HEAD_DIM, :] = vt[hd * HEAD_DIM:(hd + 1) * HEAD_DIM, c * TK:(c + 1) * TK]
            vt_ref[c, hd, HEAD_DIM:V_ROWS, :] = jnp.ones((V_ROWS - HEAD_DIM, TK), jnp.bfloat16)

    rest_ref[...] = jnp.dot(h, w_ref[:, QKV_WIDTH:D_IN], preferred_element_type=jnp.float32) * rstd


def _in_proj(x, g, w, gq, gk, bd, cos_t, sin_t, cos_tt, sin_tt):
    B, S, _ = x.shape
    ns = S // TM_IN
    const = lambda b, s: (0, 0)
    return pl.pallas_call(
        _in_proj_kernel,
        grid=(B, ns),
        in_specs=[
            pl.BlockSpec((None, TM_IN, D_MODEL), lambda b, s: (b, s, 0)),
            pl.BlockSpec((1, D_MODEL), const),
            pl.BlockSpec((D_MODEL, D_IN), const),
            pl.BlockSpec((HEAD_DIM, LANES), const),
            pl.BlockSpec((1, KV_WIDTH), const),
            pl.BlockSpec((KV_WIDTH, KV_WIDTH), const),
            pl.BlockSpec((TM_IN, LANES), lambda b, s: (s, 0)),
            pl.BlockSpec((TM_IN, LANES), lambda b, s: (s, 0)),
            pl.BlockSpec((HEAD_DIM, TM_IN), lambda b, s: (0, s)),
            pl.BlockSpec((HEAD_DIM, TM_IN), lambda b, s: (0, s)),
        ],
        out_specs=[
            pl.BlockSpec((None, ATT_WIDTH, TM_IN), lambda b, s: (b, 0, s)),
            pl.BlockSpec((None, TM_IN, KV_WIDTH), lambda b, s: (b, s, 0)),
            pl.BlockSpec((None, TM_IN // TK, ATT_KV_HEADS, V_ROWS, TK), lambda b, s: (b, s, 0, 0, 0)),
            pl.BlockSpec((None, TM_IN, REST_WIDTH), lambda b, s: (b, s, 0)),
        ],
        out_shape=[
            jax.ShapeDtypeStruct((B, ATT_WIDTH, S), jnp.bfloat16),
            jax.ShapeDtypeStruct((B, S, KV_WIDTH), jnp.bfloat16),
            jax.ShapeDtypeStruct((B, S // TK, ATT_KV_HEADS, V_ROWS, TK), jnp.bfloat16),
            jax.ShapeDtypeStruct((B, S, REST_WIDTH), jnp.float32),
        ],
        compiler_params=pltpu.CompilerParams(
            dimension_semantics=("arbitrary", "arbitrary"), vmem_limit_bytes=VMEM_LIMIT),
        name="in_proj",
    )(x, g, w, gq, gk, bd, cos_t, sin_t, cos_tt, sin_tt)


def _attention_kernel(qt_ref, k_ref, vt_ref, o_ref, qext_ref, s_ref, p_ref, acc_ref, m_ref, mx_ref):
    kv = pl.program_id(1)
    n = k_ref.shape[0] // TK
    for g in range(GQA):
        qg = qt_ref[g * HEAD_DIM:(g + 1) * HEAD_DIM, :]
        zero = jnp.zeros_like(qg)
        qext_ref[0:HEAD_DIM, g * TQ:(g + 1) * TQ] = jnp.where(kv == 0, qg, zero)
        qext_ref[HEAD_DIM:2 * HEAD_DIM, g * TQ:(g + 1) * TQ] = jnp.where(kv == 1, qg, zero)

    def scores(j, g, cols):
        kblk = k_ref[pl.ds(pl.multiple_of(j * TK, TK), TK), :]
        s = jnp.dot(kblk, qext_ref[:, cols], preferred_element_type=jnp.float32)
        s_ref[g] = s
        mx_ref[:, cols] = jnp.max(s, axis=0, keepdims=True)

    def values(j, g):
        return jnp.dot(vt_ref[j], p_ref[g], preferred_element_type=jnp.float32)

    groups = [(g, slice(g * LANE_GROUP, (g + 1) * LANE_GROUP)) for g in range(NQ // LANE_GROUP)]

    def stage(j, first=False, last=False):
        for g, cols in groups:
            if first:
                m_new = mx_ref[:, cols]
            else:
                pv = values(j - 1, g)
                m_old = m_ref[:, cols]
                m_new = jnp.maximum(m_old, mx_ref[:, cols])
                acc_ref[:, cols] = jnp.exp2(m_old - m_new) * (acc_ref[:, cols] + pv)
            m_ref[:, cols] = m_new
            p_ref[g] = _bf16(jnp.exp2(s_ref[g] - m_new))
            if not last:
                scores(j + 1, g, cols)

    acc_ref[...] = jnp.zeros_like(acc_ref)
    for g, cols in groups:
        scores(0, g, cols)
    stage(0, first=True)

    def step(j, carry):
        stage(j)
        return carry

    assert (n - 2) % ATT_UNROLL == 0
    lax.fori_loop(1, n - 1, step, 0, unroll=ATT_UNROLL)
    stage(n - 1, last=True)
    for g, cols in groups:
        acc_ref[:, cols] = acc_ref[:, cols] + values(n - 1, g)
    o = acc_ref[0:HEAD_DIM, :] / acc_ref[HEAD_DIM:HEAD_DIM + 1, :]
    o4 = jnp.concatenate([o[:, g * TQ:(g + 1) * TQ] for g in range(GQA)], axis=0)
    o_ref[...] = o4.T


def _attention(qt, k, vt):
    B, _, S = qt.shape
    nk = vt.shape[1]
    assert vt.shape[-1] == TK
    return pl.pallas_call(
        _attention_kernel,
        grid=(B, ATT_KV_HEADS, S // TQ),
        in_specs=[
            pl.BlockSpec((None, GQA * HEAD_DIM, TQ), lambda b, kv, qi: (b, kv, qi)),
            pl.BlockSpec((None, S, KV_WIDTH), lambda b, kv, qi: (b, 0, 0)),
            pl.BlockSpec((None, nk, None, V_ROWS, TK), lambda b, kv, qi: (b, 0, kv, 0, 0)),
        ],
        out_specs=pl.BlockSpec((None, TQ, GQA * HEAD_DIM), lambda b, kv, qi: (b, qi, kv)),
        out_shape=jax.ShapeDtypeStruct((B, S, ATT_WIDTH), jnp.float32),
        scratch_shapes=[
            pltpu.VMEM((KV_WIDTH, NQ), jnp.bfloat16),
            pltpu.VMEM((NQ // LANE_GROUP, TK, LANE_GROUP), jnp.float32),
            pltpu.VMEM((NQ // LANE_GROUP, TK, LANE_GROUP), jnp.bfloat16),
            pltpu.VMEM((V_ROWS, NQ), jnp.float32),
            pltpu.VMEM((1, NQ), jnp.float32),
            pltpu.VMEM((1, NQ), jnp.float32),
        ],
        compiler_params=pltpu.CompilerParams(
            dimension_semantics=("arbitrary", "arbitrary", "arbitrary"),
            vmem_limit_bytes=VMEM_LIMIT),
        name="attention",
    )(qt, k, vt)


def _silu(x):
    return x * jax.nn.sigmoid(x)


def _gelu(x):
    return 0.5 * x * (1.0 + lax.erf(x * (2.0 ** -0.5)))


def _layer_norm(x, g, b):
    mu = jnp.mean(x, axis=-1, keepdims=True)
    xc = x - mu
    var = jnp.mean(xc * xc, axis=-1, keepdims=True)
    return xc * lax.rsqrt(var + EPS) * g + b


def _glu(a):
    return a[:, :CONV_WIDTH] * jax.nn.sigmoid(a[:, CONV_WIDTH:])


CONV_ROWS = 64


def _post_kernel(x_ref, rest_ref, prev_ref, next_ref, att_ref, wout_ref, gpost_ref,
                 dw_ref, dwb_ref, cg_ref, cb_ref, sg_ref, sb_ref, sw_ref, sbias_ref,
                 o_ref, hext_ref, hsh_ref, cat_ref):
    si = pl.program_id(1)
    last = pl.num_programs(1) - 1

    cat_ref[:, 0:ATT_WIDTH] = _bf16(att_ref[...] * _silu(rest_ref[:, R_GATT:R_GATT + ATT_WIDTH]))
    o_ref[...] = jnp.dot(cat_ref[:, 0:ATT_WIDTH], wout_ref[0:ATT_WIDTH, :], preferred_element_type=jnp.float32)

    hext_ref[0:HALO, :] = jnp.where(si > 0, _glu(prev_ref[...]), 0.0)
    hext_ref[HALO:HALO + TM, :] = _glu(rest_ref[:, R_ACONV:R_ACONV + 2 * CONV_WIDTH])
    hext_ref[HALO + TM:2 * HALO + TM, :] = jnp.where(si < last, _glu(next_ref[...]), 0.0)
    for r in range(1, SUBLANES):
        hsh_ref[r - 1] = hext_ref[r:r + TM + 2 * HALO - SUBLANES, :]
    for rb in range(TM // CONV_ROWS):
        acc = jnp.zeros((CONV_ROWS, CONV_WIDTH), jnp.float32) + dwb_ref[...]
        for j in range(CONV_KERNEL):
            r = (j + HALO - CONV_PAD) % SUBLANES
            start = rb * CONV_ROWS + j + HALO - CONV_PAD - r
            tap = (hext_ref[start:start + CONV_ROWS, :] if r == 0
                   else hsh_ref[r - 1, start:start + CONV_ROWS, :])
            acc = acc + tap * dw_ref[j:j + 1, :]
        rows = slice(rb * CONV_ROWS, (rb + 1) * CONV_ROWS)
        c = _silu(_layer_norm(acc, cg_ref[...], cb_ref[...]))
        c = c * _silu(rest_ref[rows, R_GCONV:R_GCONV + CONV_WIDTH])
        cat_ref[rows, ATT_WIDTH:ATT_WIDTH + CONV_WIDTH] = _bf16(c)
    o_ref[...] += jnp.dot(cat_ref[:, ATT_WIDTH:ATT_WIDTH + CONV_WIDTH],
                          wout_ref[ATT_WIDTH:ATT_WIDTH + CONV_WIDTH, :], preferred_element_type=jnp.float32)

    lane = lax.broadcasted_iota(jnp.int32, (1, SG_WIDTH), 1)
    for c in range(TM // SG_CHUNK):
        rows = slice(c * SG_CHUNK, (c + 1) * SG_CHUNK)
        vln = _bf16(_layer_norm(_gelu(rest_ref[rows, R_V:R_V + SG_WIDTH]), sg_ref[...], sb_ref[...]))
        mixed = jnp.dot(sw_ref[SG_HEADS - 1], vln, preferred_element_type=jnp.float32)
        for hd in range(SG_HEADS - 1):
            mh = jnp.dot(sw_ref[hd], vln, preferred_element_type=jnp.float32)
            mixed = jnp.where(lane // HEAD_DIM == hd, mh, mixed)
        sgu = _gelu(rest_ref[rows, R_U:R_U + SG_WIDTH]) * (mixed + sbias_ref[...])
        sgu = sgu * _silu(rest_ref[rows, R_GSG:R_GSG + SG_WIDTH])
        cat_ref[rows, ATT_WIDTH + CONV_WIDTH:D_MIX] = _bf16(sgu)

    mix = o_ref[...] + jnp.dot(cat_ref[:, ATT_WIDTH + CONV_WIDTH:D_MIX], wout_ref[ATT_WIDTH + CONV_WIDTH:D_MIX, :],
                               preferred_element_type=jnp.float32)
    ms = jnp.mean(mix * mix, axis=-1, keepdims=True)
    o_ref[...] = x_ref[...] + mix * lax.rsqrt(ms + EPS) * gpost_ref[...]


def _post(x, rest, att, wout, gpost, dw, dwb, cg, cb, sg, sb, sw, sbias):
    B, S, _ = x.shape
    ns = S // TM
    hb = TM // HALO
    const2 = lambda b, s: (0, 0)
    const3 = lambda b, s: (0, 0, 0)
    aconv_blk = R_ACONV // (2 * CONV_WIDTH)
    return pl.pallas_call(
        _post_kernel,
        grid=(B, ns),
        in_specs=[
            pl.BlockSpec((None, TM, D_MODEL), lambda b, s: (b, s, 0)),
            pl.BlockSpec((None, TM, REST_WIDTH), lambda b, s: (b, s, 0)),
            pl.BlockSpec((None, HALO, 2 * CONV_WIDTH),
                         lambda b, s: (b, jnp.maximum(s * hb - 1, 0), aconv_blk)),
            pl.BlockSpec((None, HALO, 2 * CONV_WIDTH),
                         lambda b, s: (b, jnp.minimum((s + 1) * hb, S // HALO - 1), aconv_blk)),
            pl.BlockSpec((None, TM, ATT_WIDTH), lambda b, s: (b, s, 0)),
            pl.BlockSpec((D_MIX, D_MODEL), const2),
            pl.BlockSpec((1, D_MODEL), const2),
            pl.BlockSpec((CONV_KERNEL, CONV_WIDTH), const2),
            pl.BlockSpec((1, CONV_WIDTH), const2),
            pl.BlockSpec((1, CONV_WIDTH), const2),
            pl.BlockSpec((1, CONV_WIDTH), const2),
            pl.BlockSpec((1, SG_WIDTH), const2),
            pl.BlockSpec((1, SG_WIDTH), const2),
            pl.BlockSpec((SG_HEADS, SG_CHUNK, SG_CHUNK), const3),
            pl.BlockSpec((SG_CHUNK, SG_WIDTH), const2),
        ],
        out_specs=pl.BlockSpec((None, TM, D_MODEL), lambda b, s: (b, s, 0)),
        out_shape=jax.ShapeDtypeStruct((B, S, D_MODEL), jnp.float32),
        scratch_shapes=[
            pltpu.VMEM((TM + 2 * HALO, CONV_WIDTH), jnp.float32),
            pltpu.VMEM((SUBLANES - 1, TM + 2 * HALO - SUBLANES, CONV_WIDTH), jnp.float32),
            pltpu.VMEM((TM, D_MIX), jnp.bfloat16),
        ],
        compiler_params=pltpu.CompilerParams(
            dimension_semantics=("arbitrary", "arbitrary"), vmem_limit_bytes=VMEM_LIMIT),
        name="post",
    )(x, rest, rest, rest, att, wout, gpost, dw, dwb, cg, cb, sg, sb, sw, sbias)


def _rope_tables(S):
    half = ROPE_AXIS_DIM // 2
    t = jnp.arange(S, dtype=jnp.int32)
    row = (t // GRID_W).astype(jnp.float32)
    col = (t % GRID_W).astype(jnp.float32)
    inv_freq = ROPE_THETA ** (-jnp.arange(half, dtype=jnp.float32) / half)
    ang_r = row[:, None] * inv_freq[None, :]
    ang_c = col[:, None] * inv_freq[None, :]
    cos_h = jnp.concatenate([jnp.cos(ang_r)] * 2 + [jnp.cos(ang_c)] * 2, axis=-1)
    sin_h = jnp.concatenate([-jnp.sin(ang_r), jnp.sin(ang_r), -jnp.sin(ang_c), jnp.sin(ang_c)], axis=-1)
    reps = LANES // HEAD_DIM
    return jnp.tile(cos_h, (1, reps)), jnp.tile(sin_h, (1, reps)), cos_h.T, sin_h.T


def kernel(x, pre_norm, post_norm, w_in, w_out, q_norm, k_norm, conv_dw, conv_dw_b,
           conv_ln_g, conv_ln_b, sg_ln_g, sg_ln_b, sg_w, sg_b):
    B, S, _ = x.shape
    depth = w_in.shape[0]
    cos_t, sin_t, cos_tt, sin_tt = _rope_tables(S)
    head_id = jnp.arange(KV_WIDTH, dtype=jnp.int32) // HEAD_DIM
    bd = _bf16(jnp.where(head_id[:, None] == head_id[None, :], 1.0 / HEAD_DIM, 0.0))
    row2 = lambda a: a.reshape(1, -1)
    for l in range(depth):
        qt, k, vt, rest = _in_proj(
            x, row2(pre_norm[l]), _bf16(w_in[l]),
            jnp.broadcast_to(q_norm[l][:, None], (HEAD_DIM, LANES)), row2(jnp.tile(k_norm[l], ATT_KV_HEADS)),
            bd, cos_t, sin_t, cos_tt, sin_tt)
        att = _attention(qt, k, vt)
        sbias = jnp.repeat(sg_b[l].T, HEAD_DIM, axis=1)
        x = _post(x, rest, att, _bf16(w_out[l]), row2(post_norm[l]),
                  conv_dw[l], row2(conv_dw_b[l]), row2(conv_ln_g[l]), row2(conv_ln_b[l]),
                  row2(sg_ln_g[l]), row2(sg_ln_b[l]), _bf16(sg_w[l]), sbias)
    return x
```

```python
import math

import jax
import jax.numpy as jnp
from jax import lax
from jax.experimental import pallas as pl
from jax.experimental.pallas import tpu as pltpu

D_MODEL = 1024
HEAD_DIM = 64
GRID_W = 64
EPS = 1e-6
ATT_HEADS = 8
ATT_KV_HEADS = 2
GQA = ATT_HEADS // ATT_KV_HEADS
ATT_WIDTH = ATT_HEADS * HEAD_DIM
KV_WIDTH = ATT_KV_HEADS * HEAD_DIM
ROPE_THETA = 10000.0
ROPE_AXIS_DIM = HEAD_DIM // 2
CONV_WIDTH = 256
CONV_KERNEL = 31
CONV_PAD = CONV_KERNEL // 2
SG_HEADS = 4
SG_WIDTH = SG_HEADS * HEAD_DIM
SG_CHUNK = 128
D_MIX = ATT_WIDTH + CONV_WIDTH + SG_WIDTH
QKV_WIDTH = ATT_WIDTH + 2 * KV_WIDTH
REST_WIDTH = ATT_WIDTH + 3 * CONV_WIDTH + 3 * SG_WIDTH
D_IN = QKV_WIDTH + REST_WIDTH

R_GATT = 0
R_ACONV = ATT_WIDTH
R_GCONV = R_ACONV + 2 * CONV_WIDTH
R_U = R_GCONV + CONV_WIDTH
R_V = R_U + SG_WIDTH
R_GSG = R_V + SG_WIDTH

LANES = 128
SUBLANES = 8
HALO = 16
TM = 512
TM_IN = 1024
TQ = 512
TK = 256
ATT_UNROLL = 31
NQ = GQA * TQ
LANE_GROUP = 256
V_ROWS = HEAD_DIM + 16
SCORE_SCALE = HEAD_DIM ** -0.5 * math.log2(math.e)
VMEM_LIMIT = 56 * 1024 * 1024


def _bf16(x):
    return x.astype(jnp.bfloat16)


def _group_mean_sq(x, bd):
    x2 = x * x
    hi = _bf16(x2)
    lo = _bf16(x2 - hi.astype(jnp.float32))
    return (jnp.dot(hi, bd, preferred_element_type=jnp.float32)
            + jnp.dot(lo, bd, preferred_element_type=jnp.float32))


def _rope_slab(xn, cos, sin_signed, first_half):
    fwd = pltpu.roll(xn, LANES - ROPE_AXIS_DIM // 2, 1)
    bwd = pltpu.roll(xn, ROPE_AXIS_DIM // 2, 1)
    partner = jnp.where(first_half, fwd, bwd)
    return xn * cos + partner * sin_signed


def _in_proj_kernel(x_ref, g_ref, w_ref, gq_ref, gk_ref, bd_ref, cos_ref, sin_ref, cost_ref, sint_ref,
                    qt_ref, k_ref, vt_ref, rest_ref):
    x = x_ref[...]
    rstd = lax.rsqrt(jnp.mean(x * x, axis=-1, keepdims=True) + EPS)
    h = _bf16(x * g_ref[...])

    q = jnp.dot(h, w_ref[:, 0:ATT_WIDTH], preferred_element_type=jnp.float32) * rstd
    cost = cost_ref[...]
    sint = sint_ref[...]
    gq = jnp.concatenate([gq_ref[...]] * (TM_IN // LANES), axis=1)
    quarter = ROPE_AXIS_DIM // 2
    for j in range(ATT_WIDTH // LANES):
        qt = q[:, j * LANES:(j + 1) * LANES].T
        for hh in range(LANES // HEAD_DIM):
            xh = qt[hh * HEAD_DIM:(hh + 1) * HEAD_DIM, :]
            xn = xh * lax.rsqrt(jnp.mean(xh * xh, axis=0, keepdims=True) + EPS) * gq
            partner = jnp.concatenate([xn[quarter:2 * quarter], xn[0:quarter],
                                       xn[3 * quarter:4 * quarter], xn[2 * quarter:3 * quarter]], axis=0)
            row0 = j * LANES + hh * HEAD_DIM
            qt_ref[row0:row0 + HEAD_DIM, :] = _bf16((xn * cost + partner * sint) * SCORE_SCALE)

    lane = lax.broadcasted_iota(jnp.int32, (1, LANES), 1)
    first_half = (lane % ROPE_AXIS_DIM) < (ROPE_AXIS_DIM // 2)
    kv = jnp.dot(h, w_ref[:, ATT_WIDTH:QKV_WIDTH], preferred_element_type=jnp.float32) * rstd
    kk = kv[:, 0:KV_WIDTH]
    kn = kk * lax.rsqrt(_group_mean_sq(kk, bd_ref[...]) + EPS) * gk_ref[...]
    k_ref[...] = _bf16(_rope_slab(kn, cos_ref[...], sin_ref[...], first_half))

    vt = _bf16(kv[:, KV_WIDTH:2 * KV_WIDTH].T)
    for c in range(TM_IN // TK):
        for hd in range(ATT_KV_HEADS):
            vt_ref[c, hd, 0:HEAD_DIM, :] = vt[hd * HEAD_DIM:(hd + 1) * HEAD_DIM, c * TK:(c + 1) * TK]
            vt_ref[c, hd, HEAD_DIM:V_ROWS, :] = jnp.ones((V_ROWS - HEAD_DIM, TK), jnp.bfloat16)

    rest_ref[...] = jnp.dot(h, w_ref[:, QKV_WIDTH:D_IN], preferred_element_type=jnp.float32) * rstd


def _in_proj(x, g, w, layer, gq, gk, bd, cos_t, sin_t, cos_tt, sin_tt):
    B, S, _ = x.shape
    ns = S // TM_IN
    const = lambda b, s: (0, 0)
    return pl.pallas_call(
        _in_proj_kernel,
        grid=(B, ns),
        in_specs=[
            pl.BlockSpec((None, TM_IN, D_MODEL), lambda b, s: (b, s, 0)),
            pl.BlockSpec((1, D_MODEL), const),
            pl.BlockSpec((None, D_MODEL, D_IN), lambda b, s: (layer, 0, 0)),
            pl.BlockSpec((HEAD_DIM, LANES), const),
            pl.BlockSpec((1, KV_WIDTH), const),
            pl.BlockSpec((KV_WIDTH, KV_WIDTH), const),
            pl.BlockSpec((TM_IN, LANES), lambda b, s: (s, 0)),
            pl.BlockSpec((TM_IN, LANES), lambda b, s: (s, 0)),
            pl.BlockSpec((HEAD_DIM, TM_IN), lambda b, s: (0, s)),
            pl.BlockSpec((HEAD_DIM, TM_IN), lambda b, s: (0, s)),
        ],
        out_specs=[
            pl.BlockSpec((None, ATT_WIDTH, TM_IN), lambda b, s: (b, 0, s)),
            pl.BlockSpec((None, TM_IN, KV_WIDTH), lambda b, s: (b, s, 0)),
            pl.BlockSpec((None, TM_IN // TK, ATT_KV_HEADS, V_ROWS, TK), lambda b, s: (b, s, 0, 0, 0)),
            pl.BlockSpec((None, TM_IN, REST_WIDTH), lambda b, s: (b, s, 0)),
        ],
        out_shape=[
            jax.ShapeDtypeStruct((B, ATT_WIDTH, S), jnp.bfloat16),
            jax.ShapeDtypeStruct((B, S, KV_WIDTH), jnp.bfloat16),
            jax.ShapeDtypeStruct((B, S // TK, ATT_KV_HEADS, V_ROWS, TK), jnp.bfloat16),
            jax.ShapeDtypeStruct((B, S, REST_WIDTH), jnp.float32),
        ],
        compiler_params=pltpu.CompilerParams(
            dimension_semantics=("arbitrary", "arbitrary"), vmem_limit_bytes=VMEM_LIMIT),
        name="in_proj",
    )(x, g, w, gq, gk, bd, cos_t, sin_t, cos_tt, sin_tt)


def _attention_kernel(qt_ref, k_ref, vt_ref, o_ref, qext_ref, s_ref, p_ref, acc_ref, m_ref, mx_ref):
    kv = pl.program_id(1)
    n = k_ref.shape[0] // TK
    for g in range(GQA):
        qg = qt_ref[g * HEAD_DIM:(g + 1) * HEAD_DIM, :]
        zero = jnp.zeros_like(qg)
        qext_ref[0:HEAD_DIM, g * TQ:(g + 1) * TQ] = jnp.where(kv == 0, qg, zero)
        qext_ref[HEAD_DIM:2 * HEAD_DIM, g * TQ:(g + 1) * TQ] = jnp.where(kv == 1, qg, zero)

    def scores(j, g, cols):
        kblk = k_ref[pl.ds(pl.multiple_of(j * TK, TK), TK), :]
        s = jnp.dot(kblk, qext_ref[:, cols], preferred_element_type=jnp.float32)
        s_ref[g] = s
        mx_ref[:, cols] = jnp.max(s, axis=0, keepdims=True)

    def values(j, g):
        return jnp.dot(vt_ref[j], p_ref[g], preferred_element_type=jnp.float32)

    groups = [(g, slice(g * LANE_GROUP, (g + 1) * LANE_GROUP)) for g in range(NQ // LANE_GROUP)]

    def stage(j, first=False, last=False):
        for g, cols in groups:
            if first:
                m_new = mx_ref[:, cols]
            else:
                pv = values(j - 1, g)
                m_old = m_ref[:, cols]
                m_new = jnp.maximum(m_old, mx_ref[:, cols])
                acc_ref[:, cols] = jnp.exp2(m_old - m_new) * (acc_ref[:, cols] + pv)
            m_ref[:, cols] = m_new
            p_ref[g] = _bf16(jnp.exp2(s_ref[g] - m_new))
            if not last:
                scores(j + 1, g, cols)

    acc_ref[...] = jnp.zeros_like(acc_ref)
    for g, cols in groups:
        scores(0, g, cols)
    stage(0, first=True)

    def step(j, carry):
        stage(j)
        return carry

    assert (n - 2) % ATT_UNROLL == 0
    lax.fori_loop(1, n - 1, step, 0, unroll=ATT_UNROLL)
    stage(n - 1, last=True)
    for g, cols in groups:
        acc_ref[:, cols] = acc_ref[:, cols] + values(n - 1, g)
    o = acc_ref[0:HEAD_DIM, :] / acc_ref[HEAD_DIM:HEAD_DIM + 1, :]
    o4 = jnp.concatenate([o[:, g * TQ:(g + 1) * TQ] for g in range(GQA)], axis=0)
    o_ref[...] = o4.T


def _attention(qt, k, vt):
    B, _, S = qt.shape
    nk = vt.shape[1]
    assert vt.shape[-1] == TK
    return pl.pallas_call(
        _attention_kernel,
        grid=(B, ATT_KV_HEADS, S // TQ),
        in_specs=[
            pl.BlockSpec((None, GQA * HEAD_DIM, TQ), lambda b, kv, qi: (b, kv, qi)),
            pl.BlockSpec((None, S, KV_WIDTH), lambda b, kv, qi: (b, 0, 0)),
            pl.BlockSpec((None, nk, None, V_ROWS, TK), lambda b, kv, qi: (b, 0, kv, 0, 0)),
        ],
        out_specs=pl.BlockSpec((None, TQ, GQA * HEAD_DIM), lambda b, kv, qi: (b, qi, kv)),
        out_shape=jax.ShapeDtypeStruct((B, S, ATT_WIDTH), jnp.float32),
        scratch_shapes=[
            pltpu.VMEM((KV_WIDTH, NQ), jnp.bfloat16),
            pltpu.VMEM((NQ // LANE_GROUP, TK, LANE_GROUP), jnp.float32),
            pltpu.VMEM((NQ // LANE_GROUP, TK, LANE_GROUP), jnp.bfloat16),
            pltpu.VMEM((V_ROWS, NQ), jnp.float32),
            pltpu.VMEM((1, NQ), jnp.float32),
            pltpu.VMEM((1, NQ), jnp.float32),
        ],
        compiler_params=pltpu.CompilerParams(
            dimension_semantics=("arbitrary", "arbitrary", "arbitrary"),
            vmem_limit_bytes=VMEM_LIMIT),
        name="attention",
    )(qt, k, vt)


def _silu(x):
    return x * jax.nn.sigmoid(x)


def _gelu(x):
    return 0.5 * x * (1.0 + lax.erf(x * (2.0 ** -0.5)))


def _layer_norm(x, g, b):
    mu = jnp.mean(x, axis=-1, keepdims=True)
    xc = x - mu
    var = jnp.mean(xc * xc, axis=-1, keepdims=True)
    return xc * lax.rsqrt(var + EPS) * g + b


def _glu(a):
    return a[:, :CONV_WIDTH] * jax.nn.sigmoid(a[:, CONV_WIDTH:])


CONV_ROWS = 64


def _post_kernel(x_ref, rest_ref, prev_ref, next_ref, att_ref, wout_ref, gpost_ref,
                 dw_ref, dwb_ref, cg_ref, cb_ref, sg_ref, sb_ref, sw_ref, sbias_ref,
                 o_ref, hext_ref, hsh_ref, cat_ref):
    si = pl.program_id(1)
    last = pl.num_programs(1) - 1

    cat_ref[:, 0:ATT_WIDTH] = _bf16(att_ref[...] * _silu(rest_ref[:, R_GATT:R_GATT + ATT_WIDTH]))
    o_ref[...] = jnp.dot(cat_ref[:, 0:ATT_WIDTH], wout_ref[0:ATT_WIDTH, :], preferred_element_type=jnp.float32)

    hext_ref[0:HALO, :] = jnp.where(si > 0, _glu(prev_ref[...]), 0.0)
    hext_ref[HALO:HALO + TM, :] = _glu(rest_ref[:, R_ACONV:R_ACONV + 2 * CONV_WIDTH])
    hext_ref[HALO + TM:2 * HALO + TM, :] = jnp.where(si < last, _glu(next_ref[...]), 0.0)
    for r in range(1, SUBLANES):
        hsh_ref[r - 1] = hext_ref[r:r + TM + 2 * HALO - SUBLANES, :]
    for rb in range(TM // CONV_ROWS):
        acc = jnp.zeros((CONV_ROWS, CONV_WIDTH), jnp.float32) + dwb_ref[...]
        for j in range(CONV_KERNEL):
            r = (j + HALO - CONV_PAD) % SUBLANES
            start = rb * CONV_ROWS + j + HALO - CONV_PAD - r
            tap = (hext_ref[start:start + CONV_ROWS, :] if r == 0
                   else hsh_ref[r - 1, start:start + CONV_ROWS, :])
            acc = acc + tap * dw_ref[j:j + 1, :]
        rows = slice(rb * CONV_ROWS, (rb + 1) * CONV_ROWS)
        c = _silu(_layer_norm(acc, cg_ref[...], cb_ref[...]))
        c = c * _silu(rest_ref[rows, R_GCONV:R_GCONV + CONV_WIDTH])
        cat_ref[rows, ATT_WIDTH:ATT_WIDTH + CONV_WIDTH] = _bf16(c)
    o_ref[...] += jnp.dot(cat_ref[:, ATT_WIDTH:ATT_WIDTH + CONV_WIDTH],
                          wout_ref[ATT_WIDTH:ATT_WIDTH + CONV_WIDTH, :], preferred_element_type=jnp.float32)

    lane = lax.broadcasted_iota(jnp.int32, (1, SG_WIDTH), 1)
    for c in range(TM // SG_CHUNK):
        rows = slice(c * SG_CHUNK, (c + 1) * SG_CHUNK)
        vln = _bf16(_layer_norm(_gelu(rest_ref[rows, R_V:R_V + SG_WIDTH]), sg_ref[...], sb_ref[...]))
        mixed = jnp.dot(sw_ref[SG_HEADS - 1], vln, preferred_element_type=jnp.float32)
        for hd in range(SG_HEADS - 1):
            mh = jnp.dot(sw_ref[hd], vln, preferred_element_type=jnp.float32)
            mixed = jnp.where(lane // HEAD_DIM == hd, mh, mixed)
        sgu = _gelu(rest_ref[rows, R_U:R_U + SG_WIDTH]) * (mixed + sbias_ref[...])
        sgu = sgu * _silu(rest_ref[rows, R_GSG:R_GSG + SG_WIDTH])
        cat_ref[rows, ATT_WIDTH + CONV_WIDTH:D_MIX] = _bf16(sgu)

    mix = o_ref[...] + jnp.dot(cat_ref[:, ATT_WIDTH + CONV_WIDTH:D_MIX], wout_ref[ATT_WIDTH + CONV_WIDTH:D_MIX, :],
                               preferred_element_type=jnp.float32)
    ms = jnp.mean(mix * mix, axis=-1, keepdims=True)
    o_ref[...] = x_ref[...] + mix * lax.rsqrt(ms + EPS) * gpost_ref[...]


def _post(x, rest, att, wout, layer, gpost, dw, dwb, cg, cb, sg, sb, sw, sbias):
    B, S, _ = x.shape
    ns = S // TM
    hb = TM // HALO
    const2 = lambda b, s: (0, 0)
    const3 = lambda b, s: (0, 0, 0)
    aconv_blk = R_ACONV // (2 * CONV_WIDTH)
    return pl.pallas_call(
        _post_kernel,
        grid=(B, ns),
        in_specs=[
            pl.BlockSpec((None, TM, D_MODEL), lambda b, s: (b, s, 0)),
            pl.BlockSpec((None, TM, REST_WIDTH), lambda b, s: (b, s, 0)),
            pl.BlockSpec((None, HALO, 2 * CONV_WIDTH),
                         lambda b, s: (b, jnp.maximum(s * hb - 1, 0), aconv_blk)),
            pl.BlockSpec((None, HALO, 2 * CONV_WIDTH),
                         lambda b, s: (b, jnp.minimum((s + 1) * hb, S // HALO - 1), aconv_blk)),
            pl.BlockSpec((None, TM, ATT_WIDTH), lambda b, s: (b, s, 0)),
            pl.BlockSpec((None, D_MIX, D_MODEL), lambda b, s: (layer, 0, 0)),
            pl.BlockSpec((1, D_MODEL), const2),
            pl.BlockSpec((CONV_KERNEL, CONV_WIDTH), const2),
            pl.BlockSpec((1, CONV_WIDTH), const2),
            pl.BlockSpec((1, CONV_WIDTH), const2),
            pl.BlockSpec((1, CONV_WIDTH), const2),
            pl.BlockSpec((1, SG_WIDTH), const2),
            pl.BlockSpec((1, SG_WIDTH), const2),
            pl.BlockSpec((SG_HEADS, SG_CHUNK, SG_CHUNK), const3),
            pl.BlockSpec((SG_CHUNK, SG_WIDTH), const2),
        ],
        out_specs=pl.BlockSpec((None, TM, D_MODEL), lambda b, s: (b, s, 0)),
        out_shape=jax.ShapeDtypeStruct((B, S, D_MODEL), jnp.float32),
        scratch_shapes=[
            pltpu.VMEM((TM + 2 * HALO, CONV_WIDTH), jnp.float32),
            pltpu.VMEM((SUBLANES - 1, TM + 2 * HALO - SUBLANES, CONV_WIDTH), jnp.float32),
            pltpu.VMEM((TM, D_MIX), jnp.bfloat16),
        ],
        compiler_params=pltpu.CompilerParams(
            dimension_semantics=("arbitrary", "arbitrary"), vmem_limit_bytes=VMEM_LIMIT),
        name="post",
    )(x, rest, rest, rest, att, wout, gpost, dw, dwb, cg, cb, sg, sb, sw, sbias)


def _rope_tables(S):
    half = ROPE_AXIS_DIM // 2
    t = jnp.arange(S, dtype=jnp.int32)
    row = (t // GRID_W).astype(jnp.float32)
    col = (t % GRID_W).astype(jnp.float32)
    inv_freq = ROPE_THETA ** (-jnp.arange(half, dtype=jnp.float32) / half)

    def tables(n_feat, feat_axis):
        d = jnp.arange(n_feat, dtype=jnp.int32) % HEAD_DIM
        expand = (lambda v: v[None, :]) if feat_axis == 1 else (lambda v: v[:, None])
        tok = (lambda v: v[:, None]) if feat_axis == 1 else (lambda v: v[None, :])
        ang = jnp.where(expand(d // ROPE_AXIS_DIM == 0), tok(row), tok(col)) * expand(inv_freq[d % half])
        sign = expand(jnp.where(d % ROPE_AXIS_DIM < half, -1.0, 1.0))
        return jnp.cos(ang), sign * jnp.sin(ang)

    cos_t, sin_t = tables(LANES, 1)
    cos_tt, sin_tt = tables(HEAD_DIM, 0)
    return cos_t, sin_t, cos_tt, sin_tt


def kernel(x, pre_norm, post_norm, w_in, w_out, q_norm, k_norm, conv_dw, conv_dw_b,
           conv_ln_g, conv_ln_b, sg_ln_g, sg_ln_b, sg_w, sg_b):
    B, S, _ = x.shape
    depth = w_in.shape[0]
    cos_t, sin_t, cos_tt, sin_tt = _rope_tables(S)
    head_id = jnp.arange(KV_WIDTH, dtype=jnp.int32) // HEAD_DIM
    bd = _bf16(jnp.where(head_id[:, None] == head_id[None, :], 1.0 / HEAD_DIM, 0.0))
    row2 = lambda a: a.reshape(1, -1)
    w_in_bf, w_out_bf = _bf16(w_in), _bf16(w_out)
    for l in range(depth):
        qt, k, vt, rest = _in_proj(
            x, row2(pre_norm[l]), w_in_bf, l,
            jnp.broadcast_to(q_norm[l][:, None], (HEAD_DIM, LANES)), row2(jnp.tile(k_norm[l], ATT_KV_HEADS)),
            bd, cos_t, sin_t, cos_tt, sin_tt)
        att = _attention(qt, k, vt)
        sbias = jnp.repeat(sg_b[l].T, HEAD_DIM, axis=1)
        x = _post(x, rest, att, w_out_bf, l, row2(post_norm[l]),
                  conv_dw[l], row2(conv_dw_b[l]), row2(conv_ln_g[l]), row2(conv_ln_b[l]),
                  row2(sg_ln_g[l]), row2(sg_ln_b[l]), _bf16(sg_w[l]), sbias)
    return x
```

```python
import math

import jax
import jax.numpy as jnp
from jax import lax
from jax.experimental import pallas as pl
from jax.experimental.pallas import tpu as pltpu

D_MODEL = 1024
HEAD_DIM = 64
GRID_W = 64
EPS = 1e-6
ATT_HEADS = 8
ATT_KV_HEADS = 2
GQA = ATT_HEADS // ATT_KV_HEADS
ATT_WIDTH = ATT_HEADS * HEAD_DIM
KV_WIDTH = ATT_KV_HEADS * HEAD_DIM
ROPE_THETA = 10000.0
ROPE_AXIS_DIM = HEAD_DIM // 2
CONV_WIDTH = 256
CONV_KERNEL = 31
CONV_PAD = CONV_KERNEL // 2
SG_HEADS = 4
SG_WIDTH = SG_HEADS * HEAD_DIM
SG_CHUNK = 128
D_MIX = ATT_WIDTH + CONV_WIDTH + SG_WIDTH
QKV_WIDTH = ATT_WIDTH + 2 * KV_WIDTH
REST_WIDTH = ATT_WIDTH + 3 * CONV_WIDTH + 3 * SG_WIDTH
D_IN = QKV_WIDTH + REST_WIDTH

R_GATT = 0
R_ACONV = ATT_WIDTH
R_GCONV = R_ACONV + 2 * CONV_WIDTH
R_U = R_GCONV + CONV_WIDTH
R_V = R_U + SG_WIDTH
R_GSG = R_V + SG_WIDTH

LANES = 128
SUBLANES = 8
HALO = 16
TM = 512
TM_IN = 1024
TQ = 512
TK = 256
ATT_UNROLL = 31
NQ = GQA * TQ
LANE_GROUP = 256
V_ROWS = HEAD_DIM + 16
SCORE_SCALE = HEAD_DIM ** -0.5 * math.log2(math.e)
VMEM_LIMIT = 56 * 1024 * 1024


def _bf16(x):
    return x.astype(jnp.bfloat16)


def _head_norm_rope(xh, gain, cost, sint):
    quarter = ROPE_AXIS_DIM // 2
    xn = xh * lax.rsqrt(jnp.mean(xh * xh, axis=0, keepdims=True) + EPS) * gain
    partner = jnp.concatenate([xn[quarter:2 * quarter], xn[0:quarter],
                               xn[3 * quarter:4 * quarter], xn[2 * quarter:3 * quarter]], axis=0)
    return xn * cost + partner * sint


def _in_proj_kernel(x_ref, g_ref, w_ref, gq_ref, gk_ref, cost_ref, sint_ref,
                    qt_ref, k_ref, vt_ref, rest_ref):
    x = x_ref[...]
    rstd = lax.rsqrt(jnp.mean(x * x, axis=-1, keepdims=True) + EPS)
    h = _bf16(x * g_ref[...])

    q = jnp.dot(h, w_ref[:, 0:ATT_WIDTH], preferred_element_type=jnp.float32) * rstd
    cost = cost_ref[...]
    sint = sint_ref[...]
    gq = jnp.concatenate([gq_ref[...]] * (TM_IN // LANES), axis=1)
    gk = jnp.concatenate([gk_ref[...]] * (TM_IN // LANES), axis=1)
    for j in range(ATT_WIDTH // LANES):
        qt = q[:, j * LANES:(j + 1) * LANES].T
        for hh in range(LANES // HEAD_DIM):
            row0 = j * LANES + hh * HEAD_DIM
            r = _head_norm_rope(qt[hh * HEAD_DIM:(hh + 1) * HEAD_DIM, :], gq, cost, sint)
            qt_ref[row0:row0 + HEAD_DIM, :] = _bf16(r * SCORE_SCALE)

    kv = jnp.dot(h, w_ref[:, ATT_WIDTH:QKV_WIDTH], preferred_element_type=jnp.float32) * rstd
    kt = kv[:, 0:KV_WIDTH].T
    kr = jnp.concatenate([_head_norm_rope(kt[hd * HEAD_DIM:(hd + 1) * HEAD_DIM, :], gk, cost, sint)
                          for hd in range(ATT_KV_HEADS)], axis=0)
    k_ref[...] = _bf16(kr.T)

    vt = _bf16(kv[:, KV_WIDTH:2 * KV_WIDTH].T)
    for c in range(TM_IN // TK):
        for hd in range(ATT_KV_HEADS):
            vt_ref[c, hd, 0:HEAD_DIM, :] = vt[hd * HEAD_DIM:(hd + 1) * HEAD_DIM, c * TK:(c + 1) * TK]
            vt_ref[c, hd, HEAD_DIM:V_ROWS, :] = jnp.ones((V_ROWS - HEAD_DIM, TK), jnp.bfloat16)

    rest_ref[...] = jnp.dot(h, w_ref[:, QKV_WIDTH:D_IN], preferred_element_type=jnp.float32) * rstd


def _in_proj(x, g, w, layer, gq, gk, cos_tt, sin_tt):
    B, S, _ = x.shape
    ns = S // TM_IN
    const = lambda b, s: (0, 0)
    return pl.pallas_call(
        _in_proj_kernel,
        grid=(B, ns),
        in_specs=[
            pl.BlockSpec((None, TM_IN, D_MODEL), lambda b, s: (b, s, 0)),
            pl.BlockSpec((1, D_MODEL), const),
            pl.BlockSpec((None, D_MODEL, D_IN), lambda b, s: (layer, 0, 0)),
            pl.BlockSpec((HEAD_DIM, LANES), const),
            pl.BlockSpec((HEAD_DIM, LANES), const),
            pl.BlockSpec((HEAD_DIM, TM_IN), lambda b, s: (0, s)),
            pl.BlockSpec((HEAD_DIM, TM_IN), lambda b, s: (0, s)),
        ],
        out_specs=[
            pl.BlockSpec((None, ATT_WIDTH, TM_IN), lambda b, s: (b, 0, s)),
            pl.BlockSpec((None, TM_IN, KV_WIDTH), lambda b, s: (b, s, 0)),
            pl.BlockSpec((None, TM_IN // TK, ATT_KV_HEADS, V_ROWS, TK), lambda b, s: (b, s, 0, 0, 0)),
            pl.BlockSpec((None, TM_IN, REST_WIDTH), lambda b, s: (b, s, 0)),
        ],
        out_shape=[
            jax.ShapeDtypeStruct((B, ATT_WIDTH, S), jnp.bfloat16),
            jax.ShapeDtypeStruct((B, S, KV_WIDTH), jnp.bfloat16),
            jax.ShapeDtypeStruct((B, S // TK, ATT_KV_HEADS, V_ROWS, TK), jnp.bfloat16),
            jax.ShapeDtypeStruct((B, S, REST_WIDTH), jnp.float32),
        ],
        compiler_params=pltpu.CompilerParams(
            dimension_semantics=("arbitrary", "arbitrary"), vmem_limit_bytes=VMEM_LIMIT),
        name="in_proj",
    )(x, g, w, gq, gk, cos_tt, sin_tt)


def _attention_kernel(qt_ref, k_ref, vt_ref, o_ref, qext_ref, s_ref, p_ref, acc_ref, m_ref, mx_ref):
    kv = pl.program_id(1)
    n = k_ref.shape[0] // TK
    for g in range(GQA):
        qg = qt_ref[g * HEAD_DIM:(g + 1) * HEAD_DIM, :]
        zero = jnp.zeros_like(qg)
        qext_ref[0:HEAD_DIM, g * TQ:(g + 1) * TQ] = jnp.where(kv == 0, qg, zero)
        qext_ref[HEAD_DIM:2 * HEAD_DIM, g * TQ:(g + 1) * TQ] = jnp.where(kv == 1, qg, zero)

    def scores(j, g, cols):
        kblk = k_ref[pl.ds(pl.multiple_of(j * TK, TK), TK), :]
        s = jnp.dot(kblk, qext_ref[:, cols], preferred_element_type=jnp.float32)
        s_ref[g] = s
        mx_ref[:, cols] = jnp.max(s, axis=0, keepdims=True)

    def values(j, g):
        return jnp.dot(vt_ref[j], p_ref[g], preferred_element_type=jnp.float32)

    groups = [(g, slice(g * LANE_GROUP, (g + 1) * LANE_GROUP)) for g in range(NQ // LANE_GROUP)]

    def stage(j, first=False, last=False):
        for g, cols in groups:
            if first:
                m_new = mx_ref[:, cols]
            else:
                pv = values(j - 1, g)
                m_old = m_ref[:, cols]
                m_new = jnp.maximum(m_old, mx_ref[:, cols])
                acc_ref[:, cols] = jnp.exp2(m_old - m_new) * (acc_ref[:, cols] + pv)
            m_ref[:, cols] = m_new
            p_ref[g] = _bf16(jnp.exp2(s_ref[g] - m_new))
            if not last:
                scores(j + 1, g, cols)

    acc_ref[...] = jnp.zeros_like(acc_ref)
    for g, cols in groups:
        scores(0, g, cols)
    stage(0, first=True)

    def step(j, carry):
        stage(j)
        return carry

    assert (n - 2) % ATT_UNROLL == 0
    lax.fori_loop(1, n - 1, step, 0, unroll=ATT_UNROLL)
    stage(n - 1, last=True)
    for g, cols in groups:
        acc_ref[:, cols] = acc_ref[:, cols] + values(n - 1, g)
    o = acc_ref[0:HEAD_DIM, :] / acc_ref[HEAD_DIM:HEAD_DIM + 1, :]
    o4 = jnp.concatenate([o[:, g * TQ:(g + 1) * TQ] for g in range(GQA)], axis=0)
    o_ref[...] = o4.T


def _attention(qt, k, vt):
    B, _, S = qt.shape
    nk = vt.shape[1]
    assert vt.shape[-1] == TK
    return pl.pallas_call(
        _attention_kernel,
        grid=(B, ATT_KV_HEADS, S // TQ),
        in_specs=[
            pl.BlockSpec((None, GQA * HEAD_DIM, TQ), lambda b, kv, qi: (b, kv, qi)),
            pl.BlockSpec((None, S, KV_WIDTH), lambda b, kv, qi: (b, 0, 0)),
            pl.BlockSpec((None, nk, None, V_ROWS, TK), lambda b, kv, qi: (b, 0, kv, 0, 0)),
        ],
        out_specs=pl.BlockSpec((None, TQ, GQA * HEAD_DIM), lambda b, kv, qi: (b, qi, kv)),
        out_shape=jax.ShapeDtypeStruct((B, S, ATT_WIDTH), jnp.float32),
        scratch_shapes=[
            pltpu.VMEM((KV_WIDTH, NQ), jnp.bfloat16),
            pltpu.VMEM((NQ // LANE_GROUP, TK, LANE_GROUP), jnp.float32),
            pltpu.VMEM((NQ // LANE_GROUP, TK, LANE_GROUP), jnp.bfloat16),
            pltpu.VMEM((V_ROWS, NQ), jnp.float32),
            pltpu.VMEM((1, NQ), jnp.float32),
            pltpu.VMEM((1, NQ), jnp.float32),
        ],
        compiler_params=pltpu.CompilerParams(
            dimension_semantics=("arbitrary", "arbitrary", "arbitrary"),
            vmem_limit_bytes=VMEM_LIMIT),
        name="attention",
    )(qt, k, vt)


def _silu(x):
    return x * jax.nn.sigmoid(x)


def _gelu(x):
    return 0.5 * x * (1.0 + lax.erf(x * (2.0 ** -0.5)))


def _layer_norm(x, g, b):
    mu = jnp.mean(x, axis=-1, keepdims=True)
    xc = x - mu
    var = jnp.mean(xc * xc, axis=-1, keepdims=True)
    return xc * lax.rsqrt(var + EPS) * g + b


def _glu(a):
    return a[:, :CONV_WIDTH] * jax.nn.sigmoid(a[:, CONV_WIDTH:])


CONV_ROWS = 64


def _post_kernel(x_ref, rest_ref, prev_ref, next_ref, att_ref, wout_ref, gpost_ref,
                 dw_ref, dwb_ref, cg_ref, cb_ref, sg_ref, sb_ref, sw_ref, sbias_ref,
                 o_ref, hext_ref, hsh_ref, cat_ref):
    si = pl.program_id(1)
    last = pl.num_programs(1) - 1

    cat_ref[:, 0:ATT_WIDTH] = _bf16(att_ref[...] * _silu(rest_ref[:, R_GATT:R_GATT + ATT_WIDTH]))
    o_ref[...] = jnp.dot(cat_ref[:, 0:ATT_WIDTH], wout_ref[0:ATT_WIDTH, :], preferred_element_type=jnp.float32)

    hext_ref[0:HALO, :] = jnp.where(si > 0, _glu(prev_ref[...]), 0.0)
    hext_ref[HALO:HALO + TM, :] = _glu(rest_ref[:, R_ACONV:R_ACONV + 2 * CONV_WIDTH])
    hext_ref[HALO + TM:2 * HALO + TM, :] = jnp.where(si < last, _glu(next_ref[...]), 0.0)
    for r in range(1, SUBLANES):
        hsh_ref[r - 1] = hext_ref[r:r + TM + 2 * HALO - SUBLANES, :]
    for rb in range(TM // CONV_ROWS):
        acc = jnp.zeros((CONV_ROWS, CONV_WIDTH), jnp.float32) + dwb_ref[...]
        for j in range(CONV_KERNEL):
            r = (j + HALO - CONV_PAD) % SUBLANES
            start = rb * CONV_ROWS + j + HALO - CONV_PAD - r
            tap = (hext_ref[start:start + CONV_ROWS, :] if r == 0
                   else hsh_ref[r - 1, start:start + CONV_ROWS, :])
            acc = acc + tap * dw_ref[j:j + 1, :]
        rows = slice(rb * CONV_ROWS, (rb + 1) * CONV_ROWS)
        c = _silu(_layer_norm(acc, cg_ref[...], cb_ref[...]))
        c = c * _silu(rest_ref[rows, R_GCONV:R_GCONV + CONV_WIDTH])
        cat_ref[rows, ATT_WIDTH:ATT_WIDTH + CONV_WIDTH] = _bf16(c)
    o_ref[...] += jnp.dot(cat_ref[:, ATT_WIDTH:ATT_WIDTH + CONV_WIDTH],
                          wout_ref[ATT_WIDTH:ATT_WIDTH + CONV_WIDTH, :], preferred_element_type=jnp.float32)

    lane = lax.broadcasted_iota(jnp.int32, (1, SG_WIDTH), 1)
    for c in range(TM // SG_CHUNK):
        rows = slice(c * SG_CHUNK, (c + 1) * SG_CHUNK)
        vln = _bf16(_layer_norm(_gelu(rest_ref[rows, R_V:R_V + SG_WIDTH]), sg_ref[...], sb_ref[...]))
        mixed = jnp.dot(sw_ref[SG_HEADS - 1], vln, preferred_element_type=jnp.float32)
        for hd in range(SG_HEADS - 1):
            mh = jnp.dot(sw_ref[hd], vln, preferred_element_type=jnp.float32)
            mixed = jnp.where(lane // HEAD_DIM == hd, mh, mixed)
        sgu = _gelu(rest_ref[rows, R_U:R_U + SG_WIDTH]) * (mixed + sbias_ref[...])
        sgu = sgu * _silu(rest_ref[rows, R_GSG:R_GSG + SG_WIDTH])
        cat_ref[rows, ATT_WIDTH + CONV_WIDTH:D_MIX] = _bf16(sgu)

    mix = o_ref[...] + jnp.dot(cat_ref[:, ATT_WIDTH + CONV_WIDTH:D_MIX], wout_ref[ATT_WIDTH + CONV_WIDTH:D_MIX, :],
                               preferred_element_type=jnp.float32)
    ms = jnp.mean(mix * mix, axis=-1, keepdims=True)
    o_ref[...] = x_ref[...] + mix * lax.rsqrt(ms + EPS) * gpost_ref[...]


def _post(x, rest, att, wout, layer, gpost, dw, dwb, cg, cb, sg, sb, sw, sbias):
    B, S, _ = x.shape
    ns = S // TM
    hb = TM // HALO
    const2 = lambda b, s: (0, 0)
    const3 = lambda b, s: (0, 0, 0)
    aconv_blk = R_ACONV // (2 * CONV_WIDTH)
    return pl.pallas_call(
        _post_kernel,
        grid=(B, ns),
        in_specs=[
            pl.BlockSpec((None, TM, D_MODEL), lambda b, s: (b, s, 0)),
            pl.BlockSpec((None, TM, REST_WIDTH), lambda b, s: (b, s, 0)),
            pl.BlockSpec((None, HALO, 2 * CONV_WIDTH),
                         lambda b, s: (b, jnp.maximum(s * hb - 1, 0), aconv_blk)),
            pl.BlockSpec((None, HALO, 2 * CONV_WIDTH),
                         lambda b, s: (b, jnp.minimum((s + 1) * hb, S // HALO - 1), aconv_blk)),
            pl.BlockSpec((None, TM, ATT_WIDTH), lambda b, s: (b, s, 0)),
            pl.BlockSpec((None, D_MIX, D_MODEL), lambda b, s: (layer, 0, 0)),
            pl.BlockSpec((1, D_MODEL), const2),
            pl.BlockSpec((CONV_KERNEL, CONV_WIDTH), const2),
            pl.BlockSpec((1, CONV_WIDTH), const2),
            pl.BlockSpec((1, CONV_WIDTH), const2),
            pl.BlockSpec((1, CONV_WIDTH), const2),
            pl.BlockSpec((1, SG_WIDTH), const2),
            pl.BlockSpec((1, SG_WIDTH), const2),
            pl.BlockSpec((SG_HEADS, SG_CHUNK, SG_CHUNK), const3),
            pl.BlockSpec((SG_CHUNK, SG_WIDTH), const2),
        ],
        out_specs=pl.BlockSpec((None, TM, D_MODEL), lambda b, s: (b, s, 0)),
        out_shape=jax.ShapeDtypeStruct((B, S, D_MODEL), jnp.float32),
        scratch_shapes=[
            pltpu.VMEM((TM + 2 * HALO, CONV_WIDTH), jnp.float32),
            pltpu.VMEM((SUBLANES - 1, TM + 2 * HALO - SUBLANES, CONV_WIDTH), jnp.float32),
            pltpu.VMEM((TM, D_MIX), jnp.bfloat16),
        ],
        compiler_params=pltpu.CompilerParams(
            dimension_semantics=("arbitrary", "arbitrary"), vmem_limit_bytes=VMEM_LIMIT),
        name="post",
    )(x, rest, rest, rest, att, wout, gpost, dw, dwb, cg, cb, sg, sb, sw, sbias)


def _rope_tables(S):
    half = ROPE_AXIS_DIM // 2
    t = jnp.arange(S, dtype=jnp.int32)
    row = (t // GRID_W).astype(jnp.float32)
    col = (t % GRID_W).astype(jnp.float32)
    inv_freq = ROPE_THETA ** (-jnp.arange(half, dtype=jnp.float32) / half)
    ang_r = inv_freq[:, None] * row[None, :]
    ang_c = inv_freq[:, None] * col[None, :]
    cos_tt = jnp.concatenate([jnp.cos(ang_r)] * 2 + [jnp.cos(ang_c)] * 2, axis=0)
    sin_tt = jnp.concatenate([-jnp.sin(ang_r), jnp.sin(ang_r), -jnp.sin(ang_c), jnp.sin(ang_c)], axis=0)
    return cos_tt, sin_tt


def kernel(x, pre_norm, post_norm, w_in, w_out, q_norm, k_norm, conv_dw, conv_dw_b,
           conv_ln_g, conv_ln_b, sg_ln_g, sg_ln_b, sg_w, sg_b):
    B, S, _ = x.shape
    depth = w_in.shape[0]
    cos_tt, sin_tt = _rope_tables(S)
    row2 = lambda a: a.reshape(1, -1)
    col_lanes = lambda a: jnp.broadcast_to(a[:, None], (HEAD_DIM, LANES))
    w_in_bf, w_out_bf = _bf16(w_in), _bf16(w_out)
    for l in range(depth):
        qt, k, vt, rest = _in_proj(
            x, row2(pre_norm[l]), w_in_bf, l, col_lanes(q_norm[l]), col_lanes(k_norm[l]), cos_tt, sin_tt)
        att = _attention(qt, k, vt)
        sbias = jnp.repeat(sg_b[l].T, HEAD_DIM, axis=1)
        x = _post(x, rest, att, w_out_bf, l, row2(post_norm[l]),
                  conv_dw[l], row2(conv_dw_b[l]), row2(conv_ln_g[l]), row2(conv_ln_b[l]),
                  row2(sg_ln_g[l]), row2(sg_ln_b[l]), _bf16(sg_w[l]), sbias)
    return x
```

```python
import math

import jax
import jax.numpy as jnp
from jax import lax
from jax.experimental import pallas as pl
from jax.experimental.pallas import tpu as pltpu

D_MODEL = 1024
HEAD_DIM = 64
GRID_W = 64
EPS = 1e-6
ATT_HEADS = 8
ATT_KV_HEADS = 2
GQA = ATT_HEADS // ATT_KV_HEADS
ATT_WIDTH = ATT_HEADS * HEAD_DIM
KV_WIDTH = ATT_KV_HEADS * HEAD_DIM
ROPE_THETA = 10000.0
ROPE_AXIS_DIM = HEAD_DIM // 2
CONV_WIDTH = 256
CONV_KERNEL = 31
CONV_PAD = CONV_KERNEL // 2
SG_HEADS = 4
SG_WIDTH = SG_HEADS * HEAD_DIM
SG_CHUNK = 128
D_MIX = ATT_WIDTH + CONV_WIDTH + SG_WIDTH
QKV_WIDTH = ATT_WIDTH + 2 * KV_WIDTH
REST_WIDTH = ATT_WIDTH + 3 * CONV_WIDTH + 3 * SG_WIDTH
D_IN = QKV_WIDTH + REST_WIDTH

R_GATT = 0
R_ACONV = ATT_WIDTH
R_GCONV = R_ACONV + 2 * CONV_WIDTH
R_U = R_GCONV + CONV_WIDTH
R_V = R_U + SG_WIDTH
R_GSG = R_V + SG_WIDTH

LANES = 128
SUBLANES = 8
HALO = 16
TM = 512
TM_IN = 1024
TQ = 512
TK = 256
ATT_UNROLL = 31
NQ = GQA * TQ
LANE_GROUP = 256
V_ROWS = HEAD_DIM + 16
SCORE_SCALE = HEAD_DIM ** -0.5 * math.log2(math.e)
VMEM_LIMIT = 56 * 1024 * 1024


def _bf16(x):
    return x.astype(jnp.bfloat16)


def _head_norm_rope(xh, gain, cost, sint):
    quarter = ROPE_AXIS_DIM // 2
    xn = xh * lax.rsqrt(jnp.mean(xh * xh, axis=0, keepdims=True) + EPS) * gain
    partner = jnp.concatenate([xn[quarter:2 * quarter], xn[0:quarter],
                               xn[3 * quarter:4 * quarter], xn[2 * quarter:3 * quarter]], axis=0)
    return xn * cost + partner * sint


def _in_proj_kernel(x_ref, g_ref, w_ref, gq_ref, gk_ref, cost_ref, sint_ref,
                    qt_ref, k_ref, vt_ref, rest_ref):
    x = x_ref[...]
    rstd = lax.rsqrt(jnp.mean(x * x, axis=-1, keepdims=True) + EPS)
    h = _bf16(x * g_ref[...])

    q = jnp.dot(h, w_ref[:, 0:ATT_WIDTH], preferred_element_type=jnp.float32) * rstd
    cost = cost_ref[...]
    sint = sint_ref[...]
    gq = jnp.concatenate([gq_ref[...]] * (TM_IN // LANES), axis=1)
    gk = jnp.concatenate([gk_ref[...]] * (TM_IN // LANES), axis=1)
    for j in range(ATT_WIDTH // LANES):
        qt = q[:, j * LANES:(j + 1) * LANES].T
        for hh in range(LANES // HEAD_DIM):
            row0 = j * LANES + hh * HEAD_DIM
            r = _head_norm_rope(qt[hh * HEAD_DIM:(hh + 1) * HEAD_DIM, :], gq, cost, sint)
            qt_ref[row0:row0 + HEAD_DIM, :] = _bf16(r * SCORE_SCALE)

    kv = jnp.dot(h, w_ref[:, ATT_WIDTH:QKV_WIDTH], preferred_element_type=jnp.float32) * rstd
    kt = kv[:, 0:KV_WIDTH].T
    kr = jnp.concatenate([_head_norm_rope(kt[hd * HEAD_DIM:(hd + 1) * HEAD_DIM, :], gk, cost, sint)
                          for hd in range(ATT_KV_HEADS)], axis=0)
    k_ref[...] = _bf16(kr.T)

    vt = _bf16(kv[:, KV_WIDTH:2 * KV_WIDTH].T)
    for c in range(TM_IN // TK):
        for hd in range(ATT_KV_HEADS):
            vt_ref[c, hd, 0:HEAD_DIM, :] = vt[hd * HEAD_DIM:(hd + 1) * HEAD_DIM, c * TK:(c + 1) * TK]
            vt_ref[c, hd, HEAD_DIM:V_ROWS, :] = jnp.ones((V_ROWS - HEAD_DIM, TK), jnp.bfloat16)

    rest_ref[...] = jnp.dot(h, w_ref[:, QKV_WIDTH:D_IN], preferred_element_type=jnp.float32) * rstd


def _in_proj(x, g, w, layer, gq, gk, cos_tt, sin_tt):
    B, S, _ = x.shape
    ns = S // TM_IN
    const = lambda b, s: (0, 0)
    return pl.pallas_call(
        _in_proj_kernel,
        grid=(B, ns),
        in_specs=[
            pl.BlockSpec((None, TM_IN, D_MODEL), lambda b, s: (b, s, 0)),
            pl.BlockSpec((1, D_MODEL), const),
            pl.BlockSpec((None, D_MODEL, D_IN), lambda b, s: (layer, 0, 0)),
            pl.BlockSpec((HEAD_DIM, LANES), const),
            pl.BlockSpec((HEAD_DIM, LANES), const),
            pl.BlockSpec((HEAD_DIM, TM_IN), lambda b, s: (0, s)),
            pl.BlockSpec((HEAD_DIM, TM_IN), lambda b, s: (0, s)),
        ],
        out_specs=[
            pl.BlockSpec((None, ATT_WIDTH, TM_IN), lambda b, s: (b, 0, s)),
            pl.BlockSpec((None, TM_IN, KV_WIDTH), lambda b, s: (b, s, 0)),
            pl.BlockSpec((None, TM_IN // TK, ATT_KV_HEADS, V_ROWS, TK), lambda b, s: (b, s, 0, 0, 0)),
            pl.BlockSpec((None, TM_IN, REST_WIDTH), lambda b, s: (b, s, 0)),
        ],
        out_shape=[
            jax.ShapeDtypeStruct((B, ATT_WIDTH, S), jnp.bfloat16),
            jax.ShapeDtypeStruct((B, S, KV_WIDTH), jnp.bfloat16),
            jax.ShapeDtypeStruct((B, S // TK, ATT_KV_HEADS, V_ROWS, TK), jnp.bfloat16),
            jax.ShapeDtypeStruct((B, S, REST_WIDTH), jnp.float32),
        ],
        compiler_params=pltpu.CompilerParams(
            dimension_semantics=("arbitrary", "arbitrary"), vmem_limit_bytes=VMEM_LIMIT),
        name="in_proj",
    )(x, g, w, gq, gk, cos_tt, sin_tt)


def _attention_kernel(qt_ref, k_ref, vt_ref, o_ref, qext_ref, s_ref, p_ref, acc_ref, m_ref, mx_ref):
    kv = pl.program_id(1)
    n = k_ref.shape[0] // TK
    for g in range(GQA):
        qg = qt_ref[g * HEAD_DIM:(g + 1) * HEAD_DIM, :]
        zero = jnp.zeros_like(qg)
        qext_ref[0:HEAD_DIM, g * TQ:(g + 1) * TQ] = jnp.where(kv == 0, qg, zero)
        qext_ref[HEAD_DIM:2 * HEAD_DIM, g * TQ:(g + 1) * TQ] = jnp.where(kv == 1, qg, zero)

    def scores(j, g, cols):
        kblk = k_ref[pl.ds(pl.multiple_of(j * TK, TK), TK), :]
        s = jnp.dot(kblk, qext_ref[:, cols], preferred_element_type=jnp.float32)
        s_ref[g] = s
        mx_ref[:, cols] = jnp.max(s, axis=0, keepdims=True)

    def values(j, g):
        return jnp.dot(vt_ref[j], p_ref[g], preferred_element_type=jnp.float32)

    groups = [(g, slice(g * LANE_GROUP, (g + 1) * LANE_GROUP)) for g in range(NQ // LANE_GROUP)]

    def stage(j, first=False, last=False):
        for g, cols in groups:
            if first:
                m_new = mx_ref[:, cols]
            else:
                pv = values(j - 1, g)
                m_old = m_ref[:, cols]
                m_new = jnp.maximum(m_old, mx_ref[:, cols])
                acc_ref[:, cols] = jnp.exp2(m_old - m_new) * (acc_ref[:, cols] + pv)
            m_ref[:, cols] = m_new
            p_ref[g] = _bf16(jnp.exp2(s_ref[g] - m_new))
            if not last:
                scores(j + 1, g, cols)

    acc_ref[...] = jnp.zeros_like(acc_ref)
    for g, cols in groups:
        scores(0, g, cols)
    stage(0, first=True)

    def step(j, carry):
        stage(j)
        return carry

    assert (n - 2) % ATT_UNROLL == 0
    lax.fori_loop(1, n - 1, step, 0, unroll=ATT_UNROLL)
    stage(n - 1, last=True)
    for g, cols in groups:
        acc_ref[:, cols] = acc_ref[:, cols] + values(n - 1, g)
    o = acc_ref[0:HEAD_DIM, :] / acc_ref[HEAD_DIM:HEAD_DIM + 1, :]
    o4 = jnp.concatenate([o[:, g * TQ:(g + 1) * TQ] for g in range(GQA)], axis=0)
    o_ref[...] = o4.T


def _attention(qt, k, vt):
    B, _, S = qt.shape
    nk = vt.shape[1]
    assert vt.shape[-1] == TK
    return pl.pallas_call(
        _attention_kernel,
        grid=(B, ATT_KV_HEADS, S // TQ),
        in_specs=[
            pl.BlockSpec((None, GQA * HEAD_DIM, TQ), lambda b, kv, qi: (b, kv, qi)),
            pl.BlockSpec((None, S, KV_WIDTH), lambda b, kv, qi: (b, 0, 0)),
            pl.BlockSpec((None, nk, None, V_ROWS, TK), lambda b, kv, qi: (b, 0, kv, 0, 0)),
        ],
        out_specs=pl.BlockSpec((None, TQ, GQA * HEAD_DIM), lambda b, kv, qi: (b, qi, kv)),
        out_shape=jax.ShapeDtypeStruct((B, S, ATT_WIDTH), jnp.float32),
        scratch_shapes=[
            pltpu.VMEM((KV_WIDTH, NQ), jnp.bfloat16),
            pltpu.VMEM((NQ // LANE_GROUP, TK, LANE_GROUP), jnp.float32),
            pltpu.VMEM((NQ // LANE_GROUP, TK, LANE_GROUP), jnp.bfloat16),
            pltpu.VMEM((V_ROWS, NQ), jnp.float32),
            pltpu.VMEM((1, NQ), jnp.float32),
            pltpu.VMEM((1, NQ), jnp.float32),
        ],
        compiler_params=pltpu.CompilerParams(
            dimension_semantics=("arbitrary", "arbitrary", "arbitrary"),
            vmem_limit_bytes=VMEM_LIMIT),
        name="attention",
    )(qt, k, vt)


def _silu(x):
    h = 0.5 * x
    return h + h * jnp.tanh(h)


def _gelu(x):
    return 0.5 * x * (1.0 + lax.erf(x * (2.0 ** -0.5)))


def _layer_norm(x, g, b):
    mu = jnp.mean(x, axis=-1, keepdims=True)
    xc = x - mu
    var = jnp.mean(xc * xc, axis=-1, keepdims=True)
    return xc * lax.rsqrt(var + EPS) * g + b


def _glu(a):
    return a[:, :CONV_WIDTH] * jax.nn.sigmoid(a[:, CONV_WIDTH:])


CONV_ROWS = 64


def _post_kernel(x_ref, rest_ref, prev_ref, next_ref, att_ref, wout_ref, gpost_ref,
                 dw_ref, dwb_ref, cg_ref, cb_ref, sg_ref, sb_ref, sw_ref, sbias_ref,
                 o_ref, hext_ref, hsh_ref, cat_ref):
    si = pl.program_id(1)
    last = pl.num_programs(1) - 1

    cat_ref[:, 0:ATT_WIDTH] = _bf16(att_ref[...] * _silu(rest_ref[:, R_GATT:R_GATT + ATT_WIDTH]))
    o_ref[...] = jnp.dot(cat_ref[:, 0:ATT_WIDTH], wout_ref[0:ATT_WIDTH, :], preferred_element_type=jnp.float32)

    hext_ref[0:HALO, :] = jnp.where(si > 0, _glu(prev_ref[...]), 0.0)
    hext_ref[HALO:HALO + TM, :] = _glu(rest_ref[:, R_ACONV:R_ACONV + 2 * CONV_WIDTH])
    hext_ref[HALO + TM:2 * HALO + TM, :] = jnp.where(si < last, _glu(next_ref[...]), 0.0)
    for r in range(1, SUBLANES):
        hsh_ref[r - 1] = hext_ref[r:r + TM + 2 * HALO - SUBLANES, :]
    for rb in range(TM // CONV_ROWS):
        acc = jnp.zeros((CONV_ROWS, CONV_WIDTH), jnp.float32) + dwb_ref[...]
        for j in range(CONV_KERNEL):
            r = (j + HALO - CONV_PAD) % SUBLANES
            start = rb * CONV_ROWS + j + HALO - CONV_PAD - r
            tap = (hext_ref[start:start + CONV_ROWS, :] if r == 0
                   else hsh_ref[r - 1, start:start + CONV_ROWS, :])
            acc = acc + tap * dw_ref[j:j + 1, :]
        rows = slice(rb * CONV_ROWS, (rb + 1) * CONV_ROWS)
        c = _silu(_layer_norm(acc, cg_ref[...], cb_ref[...]))
        c = c * _silu(rest_ref[rows, R_GCONV:R_GCONV + CONV_WIDTH])
        cat_ref[rows, ATT_WIDTH:ATT_WIDTH + CONV_WIDTH] = _bf16(c)
    o_ref[...] += jnp.dot(cat_ref[:, ATT_WIDTH:ATT_WIDTH + CONV_WIDTH],
                          wout_ref[ATT_WIDTH:ATT_WIDTH + CONV_WIDTH, :], preferred_element_type=jnp.float32)

    lane = lax.broadcasted_iota(jnp.int32, (1, SG_WIDTH), 1)
    for c in range(TM // SG_CHUNK):
        rows = slice(c * SG_CHUNK, (c + 1) * SG_CHUNK)
        vln = _bf16(_layer_norm(_gelu(rest_ref[rows, R_V:R_V + SG_WIDTH]), sg_ref[...], sb_ref[...]))
        mixed = jnp.dot(sw_ref[SG_HEADS - 1], vln, preferred_element_type=jnp.float32)
        for hd in range(SG_HEADS - 1):
            mh = jnp.dot(sw_ref[hd], vln, preferred_element_type=jnp.float32)
            mixed = jnp.where(lane // HEAD_DIM == hd, mh, mixed)
        sgu = _gelu(rest_ref[rows, R_U:R_U + SG_WIDTH]) * (mixed + sbias_ref[...])
        sgu = sgu * _silu(rest_ref[rows, R_GSG:R_GSG + SG_WIDTH])
        cat_ref[rows, ATT_WIDTH + CONV_WIDTH:D_MIX] = _bf16(sgu)

    mix = o_ref[...] + jnp.dot(cat_ref[:, ATT_WIDTH + CONV_WIDTH:D_MIX], wout_ref[ATT_WIDTH + CONV_WIDTH:D_MIX, :],
                               preferred_element_type=jnp.float32)
    ms = jnp.mean(mix * mix, axis=-1, keepdims=True)
    o_ref[...] = x_ref[...] + mix * lax.rsqrt(ms + EPS) * gpost_ref[...]


def _post(x, rest, att, wout, layer, gpost, dw, dwb, cg, cb, sg, sb, sw, sbias):
    B, S, _ = x.shape
    ns = S // TM
    hb = TM // HALO
    const2 = lambda b, s: (0, 0)
    const3 = lambda b, s: (0, 0, 0)
    aconv_blk = R_ACONV // (2 * CONV_WIDTH)
    return pl.pallas_call(
        _post_kernel,
        grid=(B, ns),
        in_specs=[
            pl.BlockSpec((None, TM, D_MODEL), lambda b, s: (b, s, 0)),
            pl.BlockSpec((None, TM, REST_WIDTH), lambda b, s: (b, s, 0)),
            pl.BlockSpec((None, HALO, 2 * CONV_WIDTH),
                         lambda b, s: (b, jnp.maximum(s * hb - 1, 0), aconv_blk)),
            pl.BlockSpec((None, HALO, 2 * CONV_WIDTH),
                         lambda b, s: (b, jnp.minimum((s + 1) * hb, S // HALO - 1), aconv_blk)),
            pl.BlockSpec((None, TM, ATT_WIDTH), lambda b, s: (b, s, 0)),
            pl.BlockSpec((None, D_MIX, D_MODEL), lambda b, s: (layer, 0, 0)),
            pl.BlockSpec((1, D_MODEL), const2),
            pl.BlockSpec((CONV_KERNEL, CONV_WIDTH), const2),
            pl.BlockSpec((1, CONV_WIDTH), const2),
            pl.BlockSpec((1, CONV_WIDTH), const2),
            pl.BlockSpec((1, CONV_WIDTH), const2),
            pl.BlockSpec((1, SG_WIDTH), const2),
            pl.BlockSpec((1, SG_WIDTH), const2),
            pl.BlockSpec((SG_HEADS, SG_CHUNK, SG_CHUNK), const3),
            pl.BlockSpec((SG_CHUNK, SG_WIDTH), const2),
        ],
        out_specs=pl.BlockSpec((None, TM, D_MODEL), lambda b, s: (b, s, 0)),
        out_shape=jax.ShapeDtypeStruct((B, S, D_MODEL), jnp.float32),
        scratch_shapes=[
            pltpu.VMEM((TM + 2 * HALO, CONV_WIDTH), jnp.float32),
            pltpu.VMEM((SUBLANES - 1, TM + 2 * HALO - SUBLANES, CONV_WIDTH), jnp.float32),
            pltpu.VMEM((TM, D_MIX), jnp.bfloat16),
        ],
        compiler_params=pltpu.CompilerParams(
            dimension_semantics=("arbitrary", "arbitrary"), vmem_limit_bytes=VMEM_LIMIT),
        name="post",
    )(x, rest, rest, rest, att, wout, gpost, dw, dwb, cg, cb, sg, sb, sw, sbias)


def _rope_tables(S):
    half = ROPE_AXIS_DIM // 2
    t = jnp.arange(S, dtype=jnp.int32)
    row = (t // GRID_W).astype(jnp.float32)
    col = (t % GRID_W).astype(jnp.float32)
    inv_freq = ROPE_THETA ** (-jnp.arange(half, dtype=jnp.float32) / half)
    ang_r = inv_freq[:, None] * row[None, :]
    ang_c = inv_freq[:, None] * col[None, :]
    cos_tt = jnp.concatenate([jnp.cos(ang_r)] * 2 + [jnp.cos(ang_c)] * 2, axis=0)
    sin_tt = jnp.concatenate([-jnp.sin(ang_r), jnp.sin(ang_r), -jnp.sin(ang_c), jnp.sin(ang_c)], axis=0)
    return cos_tt, sin_tt


def kernel(x, pre_norm, post_norm, w_in, w_out, q_norm, k_norm, conv_dw, conv_dw_b,
           conv_ln_g, conv_ln_b, sg_ln_g, sg_ln_b, sg_w, sg_b):
    B, S, _ = x.shape
    depth = w_in.shape[0]
    cos_tt, sin_tt = _rope_tables(S)
    row2 = lambda a: a.reshape(1, -1)
    col_lanes = lambda a: jnp.broadcast_to(a[:, None], (HEAD_DIM, LANES))
    w_in_bf, w_out_bf = _bf16(w_in), _bf16(w_out)
    for l in range(depth):
        qt, k, vt, rest = _in_proj(
            x, row2(pre_norm[l]), w_in_bf, l, col_lanes(q_norm[l]), col_lanes(k_norm[l]), cos_tt, sin_tt)
        att = _attention(qt, k, vt)
        sbias = jnp.repeat(sg_b[l].T, HEAD_DIM, axis=1)
        x = _post(x, rest, att, w_out_bf, l, row2(post_norm[l]),
                  conv_dw[l], row2(conv_dw_b[l]), row2(conv_ln_g[l]), row2(conv_ln_b[l]),
                  row2(sg_ln_g[l]), row2(sg_ln_b[l]), _bf16(sg_w[l]), sbias)
    return x
```

```python
import math

import jax
import jax.numpy as jnp
from jax import lax
from jax.experimental import pallas as pl
from jax.experimental.pallas import tpu as pltpu

D_MODEL = 1024
HEAD_DIM = 64
GRID_W = 64
EPS = 1e-6
ATT_HEADS = 8
ATT_KV_HEADS = 2
GQA = ATT_HEADS // ATT_KV_HEADS
ATT_WIDTH = ATT_HEADS * HEAD_DIM
KV_WIDTH = ATT_KV_HEADS * HEAD_DIM
ROPE_THETA = 10000.0
ROPE_AXIS_DIM = HEAD_DIM // 2
CONV_WIDTH = 256
CONV_KERNEL = 31
CONV_PAD = CONV_KERNEL // 2
SG_HEADS = 4
SG_WIDTH = SG_HEADS * HEAD_DIM
SG_CHUNK = 128
D_MIX = ATT_WIDTH + CONV_WIDTH + SG_WIDTH
QKV_WIDTH = ATT_WIDTH + 2 * KV_WIDTH
REST_WIDTH = ATT_WIDTH + 3 * CONV_WIDTH + 3 * SG_WIDTH
D_IN = QKV_WIDTH + REST_WIDTH

R_GATT = 0
R_ACONV = ATT_WIDTH
R_GCONV = R_ACONV + 2 * CONV_WIDTH
R_U = R_GCONV + CONV_WIDTH
R_V = R_U + SG_WIDTH
R_GSG = R_V + SG_WIDTH

LANES = 128
SUBLANES = 8
HALO = 16
TM = 512
TM_IN = 1024
TQ = 1024
TK = 256
ATT_UNROLL = 31
NQ = GQA * TQ
LANE_GROUP = 256
V_ROWS = HEAD_DIM + 16
SCORE_SCALE = HEAD_DIM ** -0.5 * math.log2(math.e)
VMEM_LIMIT = 56 * 1024 * 1024


def _bf16(x):
    return x.astype(jnp.bfloat16)


def _head_norm_rope(xh, gain, cost, sint):
    quarter = ROPE_AXIS_DIM // 2
    xn = xh * lax.rsqrt(jnp.mean(xh * xh, axis=0, keepdims=True) + EPS) * gain
    partner = jnp.concatenate([xn[quarter:2 * quarter], xn[0:quarter],
                               xn[3 * quarter:4 * quarter], xn[2 * quarter:3 * quarter]], axis=0)
    return xn * cost + partner * sint


def _in_proj_kernel(x_ref, g_ref, w_ref, gq_ref, gk_ref, cost_ref, sint_ref,
                    qt_ref, k_ref, vt_ref, rest_ref):
    x = x_ref[...]
    rstd = lax.rsqrt(jnp.mean(x * x, axis=-1, keepdims=True) + EPS)
    h = _bf16(x * g_ref[...])

    q = jnp.dot(h, w_ref[:, 0:ATT_WIDTH], preferred_element_type=jnp.float32) * rstd
    cost = cost_ref[...]
    sint = sint_ref[...]
    gq = jnp.concatenate([gq_ref[...]] * (TM_IN // LANES), axis=1)
    gk = jnp.concatenate([gk_ref[...]] * (TM_IN // LANES), axis=1)
    for j in range(ATT_WIDTH // LANES):
        qt = q[:, j * LANES:(j + 1) * LANES].T
        for hh in range(LANES // HEAD_DIM):
            row0 = j * LANES + hh * HEAD_DIM
            r = _head_norm_rope(qt[hh * HEAD_DIM:(hh + 1) * HEAD_DIM, :], gq, cost, sint)
            qt_ref[row0:row0 + HEAD_DIM, :] = _bf16(r * SCORE_SCALE)

    kv = jnp.dot(h, w_ref[:, ATT_WIDTH:QKV_WIDTH], preferred_element_type=jnp.float32) * rstd
    kt = kv[:, 0:KV_WIDTH].T
    kr = jnp.concatenate([_head_norm_rope(kt[hd * HEAD_DIM:(hd + 1) * HEAD_DIM, :], gk, cost, sint)
                          for hd in range(ATT_KV_HEADS)], axis=0)
    k_ref[...] = _bf16(kr.T)

    vt = _bf16(kv[:, KV_WIDTH:2 * KV_WIDTH].T)
    for c in range(TM_IN // TK):
        for hd in range(ATT_KV_HEADS):
            vt_ref[c, hd, 0:HEAD_DIM, :] = vt[hd * HEAD_DIM:(hd + 1) * HEAD_DIM, c * TK:(c + 1) * TK]
            vt_ref[c, hd, HEAD_DIM:V_ROWS, :] = jnp.ones((V_ROWS - HEAD_DIM, TK), jnp.bfloat16)

    rest_ref[...] = jnp.dot(h, w_ref[:, QKV_WIDTH:D_IN], preferred_element_type=jnp.float32) * rstd


def _in_proj(x, g, w, layer, gq, gk, cos_tt, sin_tt):
    B, S, _ = x.shape
    ns = S // TM_IN
    const = lambda b, s: (0, 0)
    return pl.pallas_call(
        _in_proj_kernel,
        grid=(B, ns),
        in_specs=[
            pl.BlockSpec((None, TM_IN, D_MODEL), lambda b, s: (b, s, 0)),
            pl.BlockSpec((1, D_MODEL), const),
            pl.BlockSpec((None, D_MODEL, D_IN), lambda b, s: (layer, 0, 0)),
            pl.BlockSpec((HEAD_DIM, LANES), const),
            pl.BlockSpec((HEAD_DIM, LANES), const),
            pl.BlockSpec((HEAD_DIM, TM_IN), lambda b, s: (0, s)),
            pl.BlockSpec((HEAD_DIM, TM_IN), lambda b, s: (0, s)),
        ],
        out_specs=[
            pl.BlockSpec((None, ATT_WIDTH, TM_IN), lambda b, s: (b, 0, s)),
            pl.BlockSpec((None, TM_IN, KV_WIDTH), lambda b, s: (b, s, 0)),
            pl.BlockSpec((None, TM_IN // TK, ATT_KV_HEADS, V_ROWS, TK), lambda b, s: (b, s, 0, 0, 0)),
            pl.BlockSpec((None, TM_IN, REST_WIDTH), lambda b, s: (b, s, 0)),
        ],
        out_shape=[
            jax.ShapeDtypeStruct((B, ATT_WIDTH, S), jnp.bfloat16),
            jax.ShapeDtypeStruct((B, S, KV_WIDTH), jnp.bfloat16),
            jax.ShapeDtypeStruct((B, S // TK, ATT_KV_HEADS, V_ROWS, TK), jnp.bfloat16),
            jax.ShapeDtypeStruct((B, S, REST_WIDTH), jnp.float32),
        ],
        compiler_params=pltpu.CompilerParams(
            dimension_semantics=("arbitrary", "arbitrary"), vmem_limit_bytes=VMEM_LIMIT),
        name="in_proj",
    )(x, g, w, gq, gk, cos_tt, sin_tt)


def _attention_kernel(qt_ref, k_ref, vt_ref, o_ref, qext_ref, s_ref, p_ref, acc_ref, m_ref, mx_ref):
    kv = pl.program_id(1)
    n = k_ref.shape[0] // TK
    for g in range(GQA):
        qg = qt_ref[g * HEAD_DIM:(g + 1) * HEAD_DIM, :]
        zero = jnp.zeros_like(qg)
        qext_ref[0:HEAD_DIM, g * TQ:(g + 1) * TQ] = jnp.where(kv == 0, qg, zero)
        qext_ref[HEAD_DIM:2 * HEAD_DIM, g * TQ:(g + 1) * TQ] = jnp.where(kv == 1, qg, zero)

    def scores(j, g, cols):
        kblk = k_ref[pl.ds(pl.multiple_of(j * TK, TK), TK), :]
        s = jnp.dot(kblk, qext_ref[:, cols], preferred_element_type=jnp.float32)
        s_ref[g] = s
        mx_ref[:, cols] = jnp.max(s, axis=0, keepdims=True)

    def values(j, g):
        return jnp.dot(vt_ref[j], p_ref[g], preferred_element_type=jnp.float32)

    groups = [(g, slice(g * LANE_GROUP, (g + 1) * LANE_GROUP)) for g in range(NQ // LANE_GROUP)]

    def stage(j, first=False, last=False):
        for g, cols in groups:
            if first:
                m_new = mx_ref[:, cols]
            else:
                pv = values(j - 1, g)
                m_old = m_ref[:, cols]
                m_new = jnp.maximum(m_old, mx_ref[:, cols])
                acc_ref[:, cols] = jnp.exp2(m_old - m_new) * (acc_ref[:, cols] + pv)
            m_ref[:, cols] = m_new
            p_ref[g] = _bf16(jnp.exp2(s_ref[g] - m_new))
            if not last:
                scores(j + 1, g, cols)

    acc_ref[...] = jnp.zeros_like(acc_ref)
    for g, cols in groups:
        scores(0, g, cols)
    stage(0, first=True)

    def step(j, carry):
        stage(j)
        return carry

    assert (n - 2) % ATT_UNROLL == 0
    lax.fori_loop(1, n - 1, step, 0, unroll=ATT_UNROLL)
    stage(n - 1, last=True)
    for g, cols in groups:
        acc_ref[:, cols] = acc_ref[:, cols] + values(n - 1, g)
    o = acc_ref[0:HEAD_DIM, :] / acc_ref[HEAD_DIM:HEAD_DIM + 1, :]
    o4 = jnp.concatenate([o[:, g * TQ:(g + 1) * TQ] for g in range(GQA)], axis=0)
    o_ref[...] = o4.T


def _attention(qt, k, vt):
    B, _, S = qt.shape
    nk = vt.shape[1]
    assert vt.shape[-1] == TK
    return pl.pallas_call(
        _attention_kernel,
        grid=(B, ATT_KV_HEADS, S // TQ),
        in_specs=[
            pl.BlockSpec((None, GQA * HEAD_DIM, TQ), lambda b, kv, qi: (b, kv, qi)),
            pl.BlockSpec((None, S, KV_WIDTH), lambda b, kv, qi: (b, 0, 0)),
            pl.BlockSpec((None, nk, None, V_ROWS, TK), lambda b, kv, qi: (b, 0, kv, 0, 0)),
        ],
        out_specs=pl.BlockSpec((None, TQ, GQA * HEAD_DIM), lambda b, kv, qi: (b, qi, kv)),
        out_shape=jax.ShapeDtypeStruct((B, S, ATT_WIDTH), jnp.float32),
        scratch_shapes=[
            pltpu.VMEM((KV_WIDTH, NQ), jnp.bfloat16),
            pltpu.VMEM((NQ // LANE_GROUP, TK, LANE_GROUP), jnp.float32),
            pltpu.VMEM((NQ // LANE_GROUP, TK, LANE_GROUP), jnp.bfloat16),
            pltpu.VMEM((V_ROWS, NQ), jnp.float32),
            pltpu.VMEM((1, NQ), jnp.float32),
            pltpu.VMEM((1, NQ), jnp.float32),
        ],
        compiler_params=pltpu.CompilerParams(
            dimension_semantics=("arbitrary", "arbitrary", "arbitrary"),
            vmem_limit_bytes=VMEM_LIMIT),
        name="attention",
    )(qt, k, vt)


def _silu(x):
    h = 0.5 * x
    return h + h * jnp.tanh(h)


def _gelu(x):
    return 0.5 * x * (1.0 + lax.erf(x * (2.0 ** -0.5)))


def _layer_norm(x, g, b):
    mu = jnp.mean(x, axis=-1, keepdims=True)
    xc = x - mu
    var = jnp.mean(xc * xc, axis=-1, keepdims=True)
    return xc * lax.rsqrt(var + EPS) * g + b


def _glu(a):
    return a[:, :CONV_WIDTH] * jax.nn.sigmoid(a[:, CONV_WIDTH:])


CONV_ROWS = 64


def _post_kernel(x_ref, rest_ref, prev_ref, next_ref, att_ref, wout_ref, gpost_ref,
                 dw_ref, dwb_ref, cg_ref, cb_ref, sg_ref, sb_ref, sw_ref, sbias_ref,
                 o_ref, hext_ref, hsh_ref, cat_ref):
    si = pl.program_id(1)
    last = pl.num_programs(1) - 1

    cat_ref[:, 0:ATT_WIDTH] = _bf16(att_ref[...] * _silu(rest_ref[:, R_GATT:R_GATT + ATT_WIDTH]))
    o_ref[...] = jnp.dot(cat_ref[:, 0:ATT_WIDTH], wout_ref[0:ATT_WIDTH, :], preferred_element_type=jnp.float32)

    hext_ref[0:HALO, :] = jnp.where(si > 0, _glu(prev_ref[...]), 0.0)
    hext_ref[HALO:HALO + TM, :] = _glu(rest_ref[:, R_ACONV:R_ACONV + 2 * CONV_WIDTH])
    hext_ref[HALO + TM:2 * HALO + TM, :] = jnp.where(si < last, _glu(next_ref[...]), 0.0)
    for r in range(1, SUBLANES):
        hsh_ref[r - 1] = hext_ref[r:r + TM + 2 * HALO - SUBLANES, :]
    for rb in range(TM // CONV_ROWS):
        acc = jnp.zeros((CONV_ROWS, CONV_WIDTH), jnp.float32) + dwb_ref[...]
        for j in range(CONV_KERNEL):
            r = (j + HALO - CONV_PAD) % SUBLANES
            start = rb * CONV_ROWS + j + HALO - CONV_PAD - r
            tap = (hext_ref[start:start + CONV_ROWS, :] if r == 0
                   else hsh_ref[r - 1, start:start + CONV_ROWS, :])
            acc = acc + tap * dw_ref[j:j + 1, :]
        rows = slice(rb * CONV_ROWS, (rb + 1) * CONV_ROWS)
        c = _silu(_layer_norm(acc, cg_ref[...], cb_ref[...]))
        c = c * _silu(rest_ref[rows, R_GCONV:R_GCONV + CONV_WIDTH])
        cat_ref[rows, ATT_WIDTH:ATT_WIDTH + CONV_WIDTH] = _bf16(c)
    o_ref[...] += jnp.dot(cat_ref[:, ATT_WIDTH:ATT_WIDTH + CONV_WIDTH],
                          wout_ref[ATT_WIDTH:ATT_WIDTH + CONV_WIDTH, :], preferred_element_type=jnp.float32)

    lane = lax.broadcasted_iota(jnp.int32, (1, SG_WIDTH), 1)
    for c in range(TM // SG_CHUNK):
        rows = slice(c * SG_CHUNK, (c + 1) * SG_CHUNK)
        vln = _bf16(_layer_norm(_gelu(rest_ref[rows, R_V:R_V + SG_WIDTH]), sg_ref[...], sb_ref[...]))
        mixed = jnp.dot(sw_ref[SG_HEADS - 1], vln, preferred_element_type=jnp.float32)
        for hd in range(SG_HEADS - 1):
            mh = jnp.dot(sw_ref[hd], vln, preferred_element_type=jnp.float32)
            mixed = jnp.where(lane // HEAD_DIM == hd, mh, mixed)
        sgu = _gelu(rest_ref[rows, R_U:R_U + SG_WIDTH]) * (mixed + sbias_ref[...])
        sgu = sgu * _silu(rest_ref[rows, R_GSG:R_GSG + SG_WIDTH])
        cat_ref[rows, ATT_WIDTH + CONV_WIDTH:D_MIX] = _bf16(sgu)

    mix = o_ref[...] + jnp.dot(cat_ref[:, ATT_WIDTH + CONV_WIDTH:D_MIX], wout_ref[ATT_WIDTH + CONV_WIDTH:D_MIX, :],
                               preferred_element_type=jnp.float32)
    ms = jnp.mean(mix * mix, axis=-1, keepdims=True)
    o_ref[...] = x_ref[...] + mix * lax.rsqrt(ms + EPS) * gpost_ref[...]


def _post(x, rest, att, wout, layer, gpost, dw, dwb, cg, cb, sg, sb, sw, sbias):
    B, S, _ = x.shape
    ns = S // TM
    hb = TM // HALO
    const2 = lambda b, s: (0, 0)
    const3 = lambda b, s: (0, 0, 0)
    aconv_blk = R_ACONV // (2 * CONV_WIDTH)
    return pl.pallas_call(
        _post_kernel,
        grid=(B, ns),
        in_specs=[
            pl.BlockSpec((None, TM, D_MODEL), lambda b, s: (b, s, 0)),
            pl.BlockSpec((None, TM, REST_WIDTH), lambda b, s: (b, s, 0)),
            pl.BlockSpec((None, HALO, 2 * CONV_WIDTH),
                         lambda b, s: (b, jnp.maximum(s * hb - 1, 0), aconv_blk)),
            pl.BlockSpec((None, HALO, 2 * CONV_WIDTH),
                         lambda b, s: (b, jnp.minimum((s + 1) * hb, S // HALO - 1), aconv_blk)),
            pl.BlockSpec((None, TM, ATT_WIDTH), lambda b, s: (b, s, 0)),
            pl.BlockSpec((None, D_MIX, D_MODEL), lambda b, s: (layer, 0, 0)),
            pl.BlockSpec((1, D_MODEL), const2),
            pl.BlockSpec((CONV_KERNEL, CONV_WIDTH), const2),
            pl.BlockSpec((1, CONV_WIDTH), const2),
            pl.BlockSpec((1, CONV_WIDTH), const2),
            pl.BlockSpec((1, CONV_WIDTH), const2),
            pl.BlockSpec((1, SG_WIDTH), const2),
            pl.BlockSpec((1, SG_WIDTH), const2),
            pl.BlockSpec((SG_HEADS, SG_CHUNK, SG_CHUNK), const3),
            pl.BlockSpec((SG_CHUNK, SG_WIDTH), const2),
        ],
        out_specs=pl.BlockSpec((None, TM, D_MODEL), lambda b, s: (b, s, 0)),
        out_shape=jax.ShapeDtypeStruct((B, S, D_MODEL), jnp.float32),
        scratch_shapes=[
            pltpu.VMEM((TM + 2 * HALO, CONV_WIDTH), jnp.float32),
            pltpu.VMEM((SUBLANES - 1, TM + 2 * HALO - SUBLANES, CONV_WIDTH), jnp.float32),
            pltpu.VMEM((TM, D_MIX), jnp.bfloat16),
        ],
        compiler_params=pltpu.CompilerParams(
            dimension_semantics=("arbitrary", "arbitrary"), vmem_limit_bytes=VMEM_LIMIT),
        name="post",
    )(x, rest, rest, rest, att, wout, gpost, dw, dwb, cg, cb, sg, sb, sw, sbias)


def _rope_tables(S):
    half = ROPE_AXIS_DIM // 2
    t = jnp.arange(S, dtype=jnp.int32)
    row = (t // GRID_W).astype(jnp.float32)
    col = (t % GRID_W).astype(jnp.float32)
    inv_freq = ROPE_THETA ** (-jnp.arange(half, dtype=jnp.float32) / half)
    ang_r = inv_freq[:, None] * row[None, :]
    ang_c = inv_freq[:, None] * col[None, :]
    cos_tt = jnp.concatenate([jnp.cos(ang_r)] * 2 + [jnp.cos(ang_c)] * 2, axis=0)
    sin_tt = jnp.concatenate([-jnp.sin(ang_r), jnp.sin(ang_r), -jnp.sin(ang_c), jnp.sin(ang_c)], axis=0)
    return cos_tt, sin_tt


def kernel(x, pre_norm, post_norm, w_in, w_out, q_norm, k_norm, conv_dw, conv_dw_b,
           conv_ln_g, conv_ln_b, sg_ln_g, sg_ln_b, sg_w, sg_b):
    B, S, _ = x.shape
    depth = w_in.shape[0]
    cos_tt, sin_tt = _rope_tables(S)
    row2 = lambda a: a.reshape(1, -1)
    col_lanes = lambda a: jnp.broadcast_to(a[:, None], (HEAD_DIM, LANES))
    w_in_bf, w_out_bf = _bf16(w_in), _bf16(w_out)
    for l in range(depth):
        qt, k, vt, rest = _in_proj(
            x, row2(pre_norm[l]), w_in_bf, l, col_lanes(q_norm[l]), col_lanes(k_norm[l]), cos_tt, sin_tt)
        att = _attention(qt, k, vt)
        sbias = jnp.repeat(sg_b[l].T, HEAD_DIM, axis=1)
        x = _post(x, rest, att, w_out_bf, l, row2(post_norm[l]),
                  conv_dw[l], row2(conv_dw_b[l]), row2(conv_ln_g[l]), row2(conv_ln_b[l]),
                  row2(sg_ln_g[l]), row2(sg_ln_b[l]), _bf16(sg_w[l]), sbias)
    return x
```

```python
import math

import jax
import jax.numpy as jnp
from jax import lax
from jax.experimental import pallas as pl
from jax.experimental.pallas import tpu as pltpu

D_MODEL = 1024
HEAD_DIM = 64
GRID_W = 64
EPS = 1e-6
ATT_HEADS = 8
ATT_KV_HEADS = 2
GQA = ATT_HEADS // ATT_KV_HEADS
ATT_WIDTH = ATT_HEADS * HEAD_DIM
KV_WIDTH = ATT_KV_HEADS * HEAD_DIM
ROPE_THETA = 10000.0
ROPE_AXIS_DIM = HEAD_DIM // 2
CONV_WIDTH = 256
CONV_KERNEL = 31
CONV_PAD = CONV_KERNEL // 2
SG_HEADS = 4
SG_WIDTH = SG_HEADS * HEAD_DIM
SG_CHUNK = 128
D_MIX = ATT_WIDTH + CONV_WIDTH + SG_WIDTH
QKV_WIDTH = ATT_WIDTH + 2 * KV_WIDTH
REST_WIDTH = ATT_WIDTH + 3 * CONV_WIDTH + 3 * SG_WIDTH
D_IN = QKV_WIDTH + REST_WIDTH

R_GATT = 0
R_ACONV = ATT_WIDTH
R_GCONV = R_ACONV + 2 * CONV_WIDTH
R_U = R_GCONV + CONV_WIDTH
R_V = R_U + SG_WIDTH
R_GSG = R_V + SG_WIDTH

LANES = 128
SUBLANES = 8
HALO = 16
TM = 512
TM_IN = 1024
TQ = 2048
TK = 256
ATT_UNROLL = 31
NQ = GQA * TQ
LANE_GROUP = 256
V_ROWS = HEAD_DIM + 16
SCORE_SCALE = HEAD_DIM ** -0.5 * math.log2(math.e)
VMEM_LIMIT = 56 * 1024 * 1024


def _bf16(x):
    return x.astype(jnp.bfloat16)


def _head_norm_rope(xh, gain, cost, sint):
    quarter = ROPE_AXIS_DIM // 2
    xn = xh * lax.rsqrt(jnp.mean(xh * xh, axis=0, keepdims=True) + EPS) * gain
    partner = jnp.concatenate([xn[quarter:2 * quarter], xn[0:quarter],
                               xn[3 * quarter:4 * quarter], xn[2 * quarter:3 * quarter]], axis=0)
    return xn * cost + partner * sint


def _in_proj_kernel(x_ref, g_ref, w_ref, gq_ref, gk_ref, cost_ref, sint_ref,
                    qt_ref, k_ref, vt_ref, rest_ref):
    x = x_ref[...]
    rstd = lax.rsqrt(jnp.mean(x * x, axis=-1, keepdims=True) + EPS)
    h = _bf16(x * g_ref[...])

    q = jnp.dot(h, w_ref[:, 0:ATT_WIDTH], preferred_element_type=jnp.float32) * rstd
    cost = cost_ref[...]
    sint = sint_ref[...]
    gq = jnp.concatenate([gq_ref[...]] * (TM_IN // LANES), axis=1)
    gk = jnp.concatenate([gk_ref[...]] * (TM_IN // LANES), axis=1)
    for j in range(ATT_WIDTH // LANES):
        qt = q[:, j * LANES:(j + 1) * LANES].T
        for hh in range(LANES // HEAD_DIM):
            row0 = j * LANES + hh * HEAD_DIM
            r = _head_norm_rope(qt[hh * HEAD_DIM:(hh + 1) * HEAD_DIM, :], gq, cost, sint)
            qt_ref[row0:row0 + HEAD_DIM, :] = _bf16(r * SCORE_SCALE)

    kv = jnp.dot(h, w_ref[:, ATT_WIDTH:QKV_WIDTH], preferred_element_type=jnp.float32) * rstd
    kt = kv[:, 0:KV_WIDTH].T
    kr = jnp.concatenate([_head_norm_rope(kt[hd * HEAD_DIM:(hd + 1) * HEAD_DIM, :], gk, cost, sint)
                          for hd in range(ATT_KV_HEADS)], axis=0)
    k_ref[...] = _bf16(kr.T)

    vt = _bf16(kv[:, KV_WIDTH:2 * KV_WIDTH].T)
    for c in range(TM_IN // TK):
        for hd in range(ATT_KV_HEADS):
            vt_ref[c, hd, 0:HEAD_DIM, :] = vt[hd * HEAD_DIM:(hd + 1) * HEAD_DIM, c * TK:(c + 1) * TK]
            vt_ref[c, hd, HEAD_DIM:V_ROWS, :] = jnp.ones((V_ROWS - HEAD_DIM, TK), jnp.bfloat16)

    rest_ref[...] = jnp.dot(h, w_ref[:, QKV_WIDTH:D_IN], preferred_element_type=jnp.float32) * rstd


def _in_proj(x, g, w, layer, gq, gk, cos_tt, sin_tt):
    B, S, _ = x.shape
    ns = S // TM_IN
    const = lambda b, s: (0, 0)
    return pl.pallas_call(
        _in_proj_kernel,
        grid=(B, ns),
        in_specs=[
            pl.BlockSpec((None, TM_IN, D_MODEL), lambda b, s: (b, s, 0)),
            pl.BlockSpec((1, D_MODEL), const),
            pl.BlockSpec((None, D_MODEL, D_IN), lambda b, s: (layer, 0, 0)),
            pl.BlockSpec((HEAD_DIM, LANES), const),
            pl.BlockSpec((HEAD_DIM, LANES), const),
            pl.BlockSpec((HEAD_DIM, TM_IN), lambda b, s: (0, s)),
            pl.BlockSpec((HEAD_DIM, TM_IN), lambda b, s: (0, s)),
        ],
        out_specs=[
            pl.BlockSpec((None, ATT_WIDTH, TM_IN), lambda b, s: (b, 0, s)),
            pl.BlockSpec((None, TM_IN, KV_WIDTH), lambda b, s: (b, s, 0)),
            pl.BlockSpec((None, TM_IN // TK, ATT_KV_HEADS, V_ROWS, TK), lambda b, s: (b, s, 0, 0, 0)),
            pl.BlockSpec((None, TM_IN, REST_WIDTH), lambda b, s: (b, s, 0)),
        ],
        out_shape=[
            jax.ShapeDtypeStruct((B, ATT_WIDTH, S), jnp.bfloat16),
            jax.ShapeDtypeStruct((B, S, KV_WIDTH), jnp.bfloat16),
            jax.ShapeDtypeStruct((B, S // TK, ATT_KV_HEADS, V_ROWS, TK), jnp.bfloat16),
            jax.ShapeDtypeStruct((B, S, REST_WIDTH), jnp.float32),
        ],
        compiler_params=pltpu.CompilerParams(
            dimension_semantics=("arbitrary", "arbitrary"), vmem_limit_bytes=VMEM_LIMIT),
        name="in_proj",
    )(x, g, w, gq, gk, cos_tt, sin_tt)


def _attention_kernel(qt_ref, k_ref, vt_ref, o_ref, qext_ref, s_ref, p_ref, acc_ref, m_ref, mx_ref):
    kv = pl.program_id(1)
    n = k_ref.shape[0] // TK
    for g in range(GQA):
        qg = qt_ref[g * HEAD_DIM:(g + 1) * HEAD_DIM, :]
        zero = jnp.zeros_like(qg)
        qext_ref[0:HEAD_DIM, g * TQ:(g + 1) * TQ] = jnp.where(kv == 0, qg, zero)
        qext_ref[HEAD_DIM:2 * HEAD_DIM, g * TQ:(g + 1) * TQ] = jnp.where(kv == 1, qg, zero)

    def scores(j, g, cols):
        kblk = k_ref[pl.ds(pl.multiple_of(j * TK, TK), TK), :]
        s = jnp.dot(kblk, qext_ref[:, cols], preferred_element_type=jnp.float32)
        s_ref[g] = s
        mx_ref[:, cols] = jnp.max(s, axis=0, keepdims=True)

    def values(j, g):
        return jnp.dot(vt_ref[j], p_ref[g], preferred_element_type=jnp.float32)

    groups = [(g, slice(g * LANE_GROUP, (g + 1) * LANE_GROUP)) for g in range(NQ // LANE_GROUP)]

    def stage(j, first=False, last=False):
        for g, cols in groups:
            if first:
                m_new = mx_ref[:, cols]
            else:
                pv = values(j - 1, g)
                m_old = m_ref[:, cols]
                m_new = jnp.maximum(m_old, mx_ref[:, cols])
                acc_ref[:, cols] = jnp.exp2(m_old - m_new) * (acc_ref[:, cols] + pv)
            m_ref[:, cols] = m_new
            p_ref[g] = _bf16(jnp.exp2(s_ref[g] - m_new))
            if not last:
                scores(j + 1, g, cols)

    acc_ref[...] = jnp.zeros_like(acc_ref)
    for g, cols in groups:
        scores(0, g, cols)
    stage(0, first=True)

    def step(j, carry):
        stage(j)
        return carry

    assert (n - 2) % ATT_UNROLL == 0
    lax.fori_loop(1, n - 1, step, 0, unroll=ATT_UNROLL)
    stage(n - 1, last=True)
    for g, cols in groups:
        acc_ref[:, cols] = acc_ref[:, cols] + values(n - 1, g)
    o = acc_ref[0:HEAD_DIM, :] / acc_ref[HEAD_DIM:HEAD_DIM + 1, :]
    o4 = jnp.concatenate([o[:, g * TQ:(g + 1) * TQ] for g in range(GQA)], axis=0)
    o_ref[...] = o4.T


def _attention(qt, k, vt):
    B, _, S = qt.shape
    nk = vt.shape[1]
    assert vt.shape[-1] == TK
    return pl.pallas_call(
        _attention_kernel,
        grid=(B, ATT_KV_HEADS, S // TQ),
        in_specs=[
            pl.BlockSpec((None, GQA * HEAD_DIM, TQ), lambda b, kv, qi: (b, kv, qi)),
            pl.BlockSpec((None, S, KV_WIDTH), lambda b, kv, qi: (b, 0, 0)),
            pl.BlockSpec((None, nk, None, V_ROWS, TK), lambda b, kv, qi: (b, 0, kv, 0, 0)),
        ],
        out_specs=pl.BlockSpec((None, TQ, GQA * HEAD_DIM), lambda b, kv, qi: (b, qi, kv)),
        out_shape=jax.ShapeDtypeStruct((B, S, ATT_WIDTH), jnp.float32),
        scratch_shapes=[
            pltpu.VMEM((KV_WIDTH, NQ), jnp.bfloat16),
            pltpu.VMEM((NQ // LANE_GROUP, TK, LANE_GROUP), jnp.float32),
            pltpu.VMEM((NQ // LANE_GROUP, TK, LANE_GROUP), jnp.bfloat16),
            pltpu.VMEM((V_ROWS, NQ), jnp.float32),
            pltpu.VMEM((1, NQ), jnp.float32),
            pltpu.VMEM((1, NQ), jnp.float32),
        ],
        compiler_params=pltpu.CompilerParams(
            dimension_semantics=("arbitrary", "arbitrary", "arbitrary"),
            vmem_limit_bytes=VMEM_LIMIT),
        name="attention",
    )(qt, k, vt)


def _silu(x):
    h = 0.5 * x
    return h + h * jnp.tanh(h)


def _gelu(x):
    return 0.5 * x * (1.0 + lax.erf(x * (2.0 ** -0.5)))


def _layer_norm(x, g, b):
    mu = jnp.mean(x, axis=-1, keepdims=True)
    xc = x - mu
    var = jnp.mean(xc * xc, axis=-1, keepdims=True)
    return xc * lax.rsqrt(var + EPS) * g + b


def _glu(a):
    return a[:, :CONV_WIDTH] * jax.nn.sigmoid(a[:, CONV_WIDTH:])


CONV_ROWS = 64


def _post_kernel(x_ref, rest_ref, prev_ref, next_ref, att_ref, wout_ref, gpost_ref,
                 dw_ref, dwb_ref, cg_ref, cb_ref, sg_ref, sb_ref, sw_ref, sbias_ref,
                 o_ref, hext_ref, hsh_ref, cat_ref):
    si = pl.program_id(1)
    last = pl.num_programs(1) - 1

    cat_ref[:, 0:ATT_WIDTH] = _bf16(att_ref[...] * _silu(rest_ref[:, R_GATT:R_GATT + ATT_WIDTH]))
    o_ref[...] = jnp.dot(cat_ref[:, 0:ATT_WIDTH], wout_ref[0:ATT_WIDTH, :], preferred_element_type=jnp.float32)

    hext_ref[0:HALO, :] = jnp.where(si > 0, _glu(prev_ref[...]), 0.0)
    hext_ref[HALO:HALO + TM, :] = _glu(rest_ref[:, R_ACONV:R_ACONV + 2 * CONV_WIDTH])
    hext_ref[HALO + TM:2 * HALO + TM, :] = jnp.where(si < last, _glu(next_ref[...]), 0.0)
    for r in range(1, SUBLANES):
        hsh_ref[r - 1] = hext_ref[r:r + TM + 2 * HALO - SUBLANES, :]
    for rb in range(TM // CONV_ROWS):
        acc = jnp.zeros((CONV_ROWS, CONV_WIDTH), jnp.float32) + dwb_ref[...]
        for j in range(CONV_KERNEL):
            r = (j + HALO - CONV_PAD) % SUBLANES
            start = rb * CONV_ROWS + j + HALO - CONV_PAD - r
            tap = (hext_ref[start:start + CONV_ROWS, :] if r == 0
                   else hsh_ref[r - 1, start:start + CONV_ROWS, :])
            acc = acc + tap * dw_ref[j:j + 1, :]
        rows = slice(rb * CONV_ROWS, (rb + 1) * CONV_ROWS)
        c = _silu(_layer_norm(acc, cg_ref[...], cb_ref[...]))
        c = c * _silu(rest_ref[rows, R_GCONV:R_GCONV + CONV_WIDTH])
        cat_ref[rows, ATT_WIDTH:ATT_WIDTH + CONV_WIDTH] = _bf16(c)
    o_ref[...] += jnp.dot(cat_ref[:, ATT_WIDTH:ATT_WIDTH + CONV_WIDTH],
                          wout_ref[ATT_WIDTH:ATT_WIDTH + CONV_WIDTH, :], preferred_element_type=jnp.float32)

    lane = lax.broadcasted_iota(jnp.int32, (1, SG_WIDTH), 1)
    for c in range(TM // SG_CHUNK):
        rows = slice(c * SG_CHUNK, (c + 1) * SG_CHUNK)
        vln = _bf16(_layer_norm(_gelu(rest_ref[rows, R_V:R_V + SG_WIDTH]), sg_ref[...], sb_ref[...]))
        mixed = jnp.dot(sw_ref[SG_HEADS - 1], vln, preferred_element_type=jnp.float32)
        for hd in range(SG_HEADS - 1):
            mh = jnp.dot(sw_ref[hd], vln, preferred_element_type=jnp.float32)
            mixed = jnp.where(lane // HEAD_DIM == hd, mh, mixed)
        sgu = _gelu(rest_ref[rows, R_U:R_U + SG_WIDTH]) * (mixed + sbias_ref[...])
        sgu = sgu * _silu(rest_ref[rows, R_GSG:R_GSG + SG_WIDTH])
        cat_ref[rows, ATT_WIDTH + CONV_WIDTH:D_MIX] = _bf16(sgu)

    mix = o_ref[...] + jnp.dot(cat_ref[:, ATT_WIDTH + CONV_WIDTH:D_MIX], wout_ref[ATT_WIDTH + CONV_WIDTH:D_MIX, :],
                               preferred_element_type=jnp.float32)
    ms = jnp.mean(mix * mix, axis=-1, keepdims=True)
    o_ref[...] = x_ref[...] + mix * lax.rsqrt(ms + EPS) * gpost_ref[...]


def _post(x, rest, att, wout, layer, gpost, dw, dwb, cg, cb, sg, sb, sw, sbias):
    B, S, _ = x.shape
    ns = S // TM
    hb = TM // HALO
    const2 = lambda b, s: (0, 0)
    const3 = lambda b, s: (0, 0, 0)
    aconv_blk = R_ACONV // (2 * CONV_WIDTH)
    return pl.pallas_call(
        _post_kernel,
        grid=(B, ns),
        in_specs=[
            pl.BlockSpec((None, TM, D_MODEL), lambda b, s: (b, s, 0)),
            pl.BlockSpec((None, TM, REST_WIDTH), lambda b, s: (b, s, 0)),
            pl.BlockSpec((None, HALO, 2 * CONV_WIDTH),
                         lambda b, s: (b, jnp.maximum(s * hb - 1, 0), aconv_blk)),
            pl.BlockSpec((None, HALO, 2 * CONV_WIDTH),
                         lambda b, s: (b, jnp.minimum((s + 1) * hb, S // HALO - 1), aconv_blk)),
            pl.BlockSpec((None, TM, ATT_WIDTH), lambda b, s: (b, s, 0)),
            pl.BlockSpec((None, D_MIX, D_MODEL), lambda b, s: (layer, 0, 0)),
            pl.BlockSpec((1, D_MODEL), const2),
            pl.BlockSpec((CONV_KERNEL, CONV_WIDTH), const2),
            pl.BlockSpec((1, CONV_WIDTH), const2),
            pl.BlockSpec((1, CONV_WIDTH), const2),
            pl.BlockSpec((1, CONV_WIDTH), const2),
            pl.BlockSpec((1, SG_WIDTH), const2),
            pl.BlockSpec((1, SG_WIDTH), const2),
            pl.BlockSpec((SG_HEADS, SG_CHUNK, SG_CHUNK), const3),
            pl.BlockSpec((SG_CHUNK, SG_WIDTH), const2),
        ],
        out_specs=pl.BlockSpec((None, TM, D_MODEL), lambda b, s: (b, s, 0)),
        out_shape=jax.ShapeDtypeStruct((B, S, D_MODEL), jnp.float32),
        scratch_shapes=[
            pltpu.VMEM((TM + 2 * HALO, CONV_WIDTH), jnp.float32),
            pltpu.VMEM((SUBLANES - 1, TM + 2 * HALO - SUBLANES, CONV_WIDTH), jnp.float32),
            pltpu.VMEM((TM, D_MIX), jnp.bfloat16),
        ],
        compiler_params=pltpu.CompilerParams(
            dimension_semantics=("arbitrary", "arbitrary"), vmem_limit_bytes=VMEM_LIMIT),
        name="post",
    )(x, rest, rest, rest, att, wout, gpost, dw, dwb, cg, cb, sg, sb, sw, sbias)


def _rope_tables(S):
    half = ROPE_AXIS_DIM // 2
    t = jnp.arange(S, dtype=jnp.int32)
    row = (t // GRID_W).astype(jnp.float32)
    col = (t % GRID_W).astype(jnp.float32)
    inv_freq = ROPE_THETA ** (-jnp.arange(half, dtype=jnp.float32) / half)
    ang_r = inv_freq[:, None] * row[None, :]
    ang_c = inv_freq[:, None] * col[None, :]
    cos_tt = jnp.concatenate([jnp.cos(ang_r)] * 2 + [jnp.cos(ang_c)] * 2, axis=0)
    sin_tt = jnp.concatenate([-jnp.sin(ang_r), jnp.sin(ang_r), -jnp.sin(ang_c), jnp.sin(ang_c)], axis=0)
    return cos_tt, sin_tt


def kernel(x, pre_norm, post_norm, w_in, w_out, q_norm, k_norm, conv_dw, conv_dw_b,
           conv_ln_g, conv_ln_b, sg_ln_g, sg_ln_b, sg_w, sg_b):
    B, S, _ = x.shape
    depth = w_in.shape[0]
    cos_tt, sin_tt = _rope_tables(S)
    row2 = lambda a: a.reshape(1, -1)
    col_lanes = lambda a: jnp.broadcast_to(a[:, None], (HEAD_DIM, LANES))
    w_in_bf, w_out_bf = _bf16(w_in), _bf16(w_out)
    for l in range(depth):
        qt, k, vt, rest = _in_proj(
            x, row2(pre_norm[l]), w_in_bf, l, col_lanes(q_norm[l]), col_lanes(k_norm[l]), cos_tt, sin_tt)
        att = _attention(qt, k, vt)
        sbias = jnp.repeat(sg_b[l].T, HEAD_DIM, axis=1)
        x = _post(x, rest, att, w_out_bf, l, row2(post_norm[l]),
                  conv_dw[l], row2(conv_dw_b[l]), row2(conv_ln_g[l]), row2(conv_ln_b[l]),
                  row2(sg_ln_g[l]), row2(sg_ln_b[l]), _bf16(sg_w[l]), sbias)
    return x
```

```python
import math

import jax
import jax.numpy as jnp
from jax import lax
from jax.experimental import pallas as pl
from jax.experimental.pallas import tpu as pltpu

D_MODEL = 1024
HEAD_DIM = 64
GRID_W = 64
EPS = 1e-6
ATT_HEADS = 8
ATT_KV_HEADS = 2
GQA = ATT_HEADS // ATT_KV_HEADS
ATT_WIDTH = ATT_HEADS * HEAD_DIM
KV_WIDTH = ATT_KV_HEADS * HEAD_DIM
ROPE_THETA = 10000.0
ROPE_AXIS_DIM = HEAD_DIM // 2
CONV_WIDTH = 256
CONV_KERNEL = 31
CONV_PAD = CONV_KERNEL // 2
SG_HEADS = 4
SG_WIDTH = SG_HEADS * HEAD_DIM
SG_CHUNK = 128
D_MIX = ATT_WIDTH + CONV_WIDTH + SG_WIDTH
QKV_WIDTH = ATT_WIDTH + 2 * KV_WIDTH
REST_WIDTH = ATT_WIDTH + 3 * CONV_WIDTH + 3 * SG_WIDTH
D_IN = QKV_WIDTH + REST_WIDTH

R_GATT = 0
R_ACONV = ATT_WIDTH
R_GCONV = R_ACONV + 2 * CONV_WIDTH
R_U = R_GCONV + CONV_WIDTH
R_V = R_U + SG_WIDTH
R_GSG = R_V + SG_WIDTH

LANES = 128
SUBLANES = 8
HALO = 16
TM = 512
TM_IN = 1024
TQ = 1024
TK = 256
ATT_UNROLL = 31
NQ = GQA * TQ
LANE_GROUP = 512
V_ROWS = HEAD_DIM + 16
SCORE_SCALE = HEAD_DIM ** -0.5 * math.log2(math.e)
VMEM_LIMIT = 56 * 1024 * 1024


def _bf16(x):
    return x.astype(jnp.bfloat16)


def _head_norm_rope(xh, gain, cost, sint):
    quarter = ROPE_AXIS_DIM // 2
    xn = xh * lax.rsqrt(jnp.mean(xh * xh, axis=0, keepdims=True) + EPS) * gain
    partner = jnp.concatenate([xn[quarter:2 * quarter], xn[0:quarter],
                               xn[3 * quarter:4 * quarter], xn[2 * quarter:3 * quarter]], axis=0)
    return xn * cost + partner * sint


def _in_proj_kernel(x_ref, g_ref, w_ref, gq_ref, gk_ref, cost_ref, sint_ref,
                    qt_ref, k_ref, vt_ref, rest_ref):
    x = x_ref[...]
    rstd = lax.rsqrt(jnp.mean(x * x, axis=-1, keepdims=True) + EPS)
    h = _bf16(x * g_ref[...])

    q = jnp.dot(h, w_ref[:, 0:ATT_WIDTH], preferred_element_type=jnp.float32) * rstd
    cost = cost_ref[...]
    sint = sint_ref[...]
    gq = jnp.concatenate([gq_ref[...]] * (TM_IN // LANES), axis=1)
    gk = jnp.concatenate([gk_ref[...]] * (TM_IN // LANES), axis=1)
    for j in range(ATT_WIDTH // LANES):
        qt = q[:, j * LANES:(j + 1) * LANES].T
        for hh in range(LANES // HEAD_DIM):
            row0 = j * LANES + hh * HEAD_DIM
            r = _head_norm_rope(qt[hh * HEAD_DIM:(hh + 1) * HEAD_DIM, :], gq, cost, sint)
            qt_ref[row0:row0 + HEAD_DIM, :] = _bf16(r * SCORE_SCALE)

    kv = jnp.dot(h, w_ref[:, ATT_WIDTH:QKV_WIDTH], preferred_element_type=jnp.float32) * rstd
    kt = kv[:, 0:KV_WIDTH].T
    kr = jnp.concatenate([_head_norm_rope(kt[hd * HEAD_DIM:(hd + 1) * HEAD_DIM, :], gk, cost, sint)
                          for hd in range(ATT_KV_HEADS)], axis=0)
    k_ref[...] = _bf16(kr.T)

    vt = _bf16(kv[:, KV_WIDTH:2 * KV_WIDTH].T)
    for c in range(TM_IN // TK):
        for hd in range(ATT_KV_HEADS):
            vt_ref[c, hd, 0:HEAD_DIM, :] = vt[hd * HEAD_DIM:(hd + 1) * HEAD_DIM, c * TK:(c + 1) * TK]
            vt_ref[c, hd, HEAD_DIM:V_ROWS, :] = jnp.ones((V_ROWS - HEAD_DIM, TK), jnp.bfloat16)

    rest_ref[...] = jnp.dot(h, w_ref[:, QKV_WIDTH:D_IN], preferred_element_type=jnp.float32) * rstd


def _in_proj(x, g, w, layer, gq, gk, cos_tt, sin_tt):
    B, S, _ = x.shape
    ns = S // TM_IN
    const = lambda b, s: (0, 0)
    return pl.pallas_call(
        _in_proj_kernel,
        grid=(B, ns),
        in_specs=[
            pl.BlockSpec((None, TM_IN, D_MODEL), lambda b, s: (b, s, 0)),
            pl.BlockSpec((1, D_MODEL), const),
            pl.BlockSpec((None, D_MODEL, D_IN), lambda b, s: (layer, 0, 0)),
            pl.BlockSpec((HEAD_DIM, LANES), const),
            pl.BlockSpec((HEAD_DIM, LANES), const),
            pl.BlockSpec((HEAD_DIM, TM_IN), lambda b, s: (0, s)),
            pl.BlockSpec((HEAD_DIM, TM_IN), lambda b, s: (0, s)),
        ],
        out_specs=[
            pl.BlockSpec((None, ATT_WIDTH, TM_IN), lambda b, s: (b, 0, s)),
            pl.BlockSpec((None, TM_IN, KV_WIDTH), lambda b, s: (b, s, 0)),
            pl.BlockSpec((None, TM_IN // TK, ATT_KV_HEADS, V_ROWS, TK), lambda b, s: (b, s, 0, 0, 0)),
            pl.BlockSpec((None, TM_IN, REST_WIDTH), lambda b, s: (b, s, 0)),
        ],
        out_shape=[
            jax.ShapeDtypeStruct((B, ATT_WIDTH, S), jnp.bfloat16),
            jax.ShapeDtypeStruct((B, S, KV_WIDTH), jnp.bfloat16),
            jax.ShapeDtypeStruct((B, S // TK, ATT_KV_HEADS, V_ROWS, TK), jnp.bfloat16),
            jax.ShapeDtypeStruct((B, S, REST_WIDTH), jnp.float32),
        ],
        compiler_params=pltpu.CompilerParams(
            dimension_semantics=("arbitrary", "arbitrary"), vmem_limit_bytes=VMEM_LIMIT),
        name="in_proj",
    )(x, g, w, gq, gk, cos_tt, sin_tt)


def _attention_kernel(qt_ref, k_ref, vt_ref, o_ref, qext_ref, s_ref, p_ref, acc_ref, m_ref, mx_ref):
    kv = pl.program_id(1)
    n = k_ref.shape[0] // TK
    for g in range(GQA):
        qg = qt_ref[g * HEAD_DIM:(g + 1) * HEAD_DIM, :]
        zero = jnp.zeros_like(qg)
        qext_ref[0:HEAD_DIM, g * TQ:(g + 1) * TQ] = jnp.where(kv == 0, qg, zero)
        qext_ref[HEAD_DIM:2 * HEAD_DIM, g * TQ:(g + 1) * TQ] = jnp.where(kv == 1, qg, zero)

    def scores(j, g, cols):
        kblk = k_ref[pl.ds(pl.multiple_of(j * TK, TK), TK), :]
        s = jnp.dot(kblk, qext_ref[:, cols], preferred_element_type=jnp.float32)
        s_ref[g] = s
        mx_ref[:, cols] = jnp.max(s, axis=0, keepdims=True)

    def values(j, g):
        return jnp.dot(vt_ref[j], p_ref[g], preferred_element_type=jnp.float32)

    groups = [(g, slice(g * LANE_GROUP, (g + 1) * LANE_GROUP)) for g in range(NQ // LANE_GROUP)]

    def stage(j, first=False, last=False):
        for g, cols in groups:
            if first:
                m_new = mx_ref[:, cols]
            else:
                pv = values(j - 1, g)
                m_old = m_ref[:, cols]
                m_new = jnp.maximum(m_old, mx_ref[:, cols])
                acc_ref[:, cols] = jnp.exp2(m_old - m_new) * (acc_ref[:, cols] + pv)
            m_ref[:, cols] = m_new
            p_ref[g] = _bf16(jnp.exp2(s_ref[g] - m_new))
            if not last:
                scores(j + 1, g, cols)

    acc_ref[...] = jnp.zeros_like(acc_ref)
    for g, cols in groups:
        scores(0, g, cols)
    stage(0, first=True)

    def step(j, carry):
        stage(j)
        return carry

    assert (n - 2) % ATT_UNROLL == 0
    lax.fori_loop(1, n - 1, step, 0, unroll=ATT_UNROLL)
    stage(n - 1, last=True)
    for g, cols in groups:
        acc_ref[:, cols] = acc_ref[:, cols] + values(n - 1, g)
    o = acc_ref[0:HEAD_DIM, :] / acc_ref[HEAD_DIM:HEAD_DIM + 1, :]
    o4 = jnp.concatenate([o[:, g * TQ:(g + 1) * TQ] for g in range(GQA)], axis=0)
    o_ref[...] = o4.T


def _attention(qt, k, vt):
    B, _, S = qt.shape
    nk = vt.shape[1]
    assert vt.shape[-1] == TK
    return pl.pallas_call(
        _attention_kernel,
        grid=(B, ATT_KV_HEADS, S // TQ),
        in_specs=[
            pl.BlockSpec((None, GQA * HEAD_DIM, TQ), lambda b, kv, qi: (b, kv, qi)),
            pl.BlockSpec((None, S, KV_WIDTH), lambda b, kv, qi: (b, 0, 0)),
            pl.BlockSpec((None, nk, None, V_ROWS, TK), lambda b, kv, qi: (b, 0, kv, 0, 0)),
        ],
        out_specs=pl.BlockSpec((None, TQ, GQA * HEAD_DIM), lambda b, kv, qi: (b, qi, kv)),
        out_shape=jax.ShapeDtypeStruct((B, S, ATT_WIDTH), jnp.float32),
        scratch_shapes=[
            pltpu.VMEM((KV_WIDTH, NQ), jnp.bfloat16),
            pltpu.VMEM((NQ // LANE_GROUP, TK, LANE_GROUP), jnp.float32),
            pltpu.VMEM((NQ // LANE_GROUP, TK, LANE_GROUP), jnp.bfloat16),
            pltpu.VMEM((V_ROWS, NQ), jnp.float32),
            pltpu.VMEM((1, NQ), jnp.float32),
            pltpu.VMEM((1, NQ), jnp.float32),
        ],
        compiler_params=pltpu.CompilerParams(
            dimension_semantics=("arbitrary", "arbitrary", "arbitrary"),
            vmem_limit_bytes=VMEM_LIMIT),
        name="attention",
    )(qt, k, vt)


def _silu(x):
    h = 0.5 * x
    return h + h * jnp.tanh(h)


def _gelu(x):
    return 0.5 * x * (1.0 + lax.erf(x * (2.0 ** -0.5)))


def _layer_norm(x, g, b):
    mu = jnp.mean(x, axis=-1, keepdims=True)
    xc = x - mu
    var = jnp.mean(xc * xc, axis=-1, keepdims=True)
    return xc * lax.rsqrt(var + EPS) * g + b


def _glu(a):
    return a[:, :CONV_WIDTH] * jax.nn.sigmoid(a[:, CONV_WIDTH:])


CONV_ROWS = 64


def _post_kernel(x_ref, rest_ref, prev_ref, next_ref, att_ref, wout_ref, gpost_ref,
                 dw_ref, dwb_ref, cg_ref, cb_ref, sg_ref, sb_ref, sw_ref, sbias_ref,
                 o_ref, hext_ref, hsh_ref, cat_ref):
    si = pl.program_id(1)
    last = pl.num_programs(1) - 1

    cat_ref[:, 0:ATT_WIDTH] = _bf16(att_ref[...] * _silu(rest_ref[:, R_GATT:R_GATT + ATT_WIDTH]))
    o_ref[...] = jnp.dot(cat_ref[:, 0:ATT_WIDTH], wout_ref[0:ATT_WIDTH, :], preferred_element_type=jnp.float32)

    hext_ref[0:HALO, :] = jnp.where(si > 0, _glu(prev_ref[...]), 0.0)
    hext_ref[HALO:HALO + TM, :] = _glu(rest_ref[:, R_ACONV:R_ACONV + 2 * CONV_WIDTH])
    hext_ref[HALO + TM:2 * HALO + TM, :] = jnp.where(si < last, _glu(next_ref[...]), 0.0)
    for r in range(1, SUBLANES):
        hsh_ref[r - 1] = hext_ref[r:r + TM + 2 * HALO - SUBLANES, :]
    for rb in range(TM // CONV_ROWS):
        acc = jnp.zeros((CONV_ROWS, CONV_WIDTH), jnp.float32) + dwb_ref[...]
        for j in range(CONV_KERNEL):
            r = (j + HALO - CONV_PAD) % SUBLANES
            start = rb * CONV_ROWS + j + HALO - CONV_PAD - r
            tap = (hext_ref[start:start + CONV_ROWS, :] if r == 0
                   else hsh_ref[r - 1, start:start + CONV_ROWS, :])
            acc = acc + tap * dw_ref[j:j + 1, :]
        rows = slice(rb * CONV_ROWS, (rb + 1) * CONV_ROWS)
        c = _silu(_layer_norm(acc, cg_ref[...], cb_ref[...]))
        c = c * _silu(rest_ref[rows, R_GCONV:R_GCONV + CONV_WIDTH])
        cat_ref[rows, ATT_WIDTH:ATT_WIDTH + CONV_WIDTH] = _bf16(c)
    o_ref[...] += jnp.dot(cat_ref[:, ATT_WIDTH:ATT_WIDTH + CONV_WIDTH],
                          wout_ref[ATT_WIDTH:ATT_WIDTH + CONV_WIDTH, :], preferred_element_type=jnp.float32)

    lane = lax.broadcasted_iota(jnp.int32, (1, SG_WIDTH), 1)
    for c in range(TM // SG_CHUNK):
        rows = slice(c * SG_CHUNK, (c + 1) * SG_CHUNK)
        vln = _bf16(_layer_norm(_gelu(rest_ref[rows, R_V:R_V + SG_WIDTH]), sg_ref[...], sb_ref[...]))
        mixed = jnp.dot(sw_ref[SG_HEADS - 1], vln, preferred_element_type=jnp.float32)
        for hd in range(SG_HEADS - 1):
            mh = jnp.dot(sw_ref[hd], vln, preferred_element_type=jnp.float32)
            mixed = jnp.where(lane // HEAD_DIM == hd, mh, mixed)
        sgu = _gelu(rest_ref[rows, R_U:R_U + SG_WIDTH]) * (mixed + sbias_ref[...])
        sgu = sgu * _silu(rest_ref[rows, R_GSG:R_GSG + SG_WIDTH])
        cat_ref[rows, ATT_WIDTH + CONV_WIDTH:D_MIX] = _bf16(sgu)

    mix = o_ref[...] + jnp.dot(cat_ref[:, ATT_WIDTH + CONV_WIDTH:D_MIX], wout_ref[ATT_WIDTH + CONV_WIDTH:D_MIX, :],
                               preferred_element_type=jnp.float32)
    ms = jnp.mean(mix * mix, axis=-1, keepdims=True)
    o_ref[...] = x_ref[...] + mix * lax.rsqrt(ms + EPS) * gpost_ref[...]


def _post(x, rest, att, wout, layer, gpost, dw, dwb, cg, cb, sg, sb, sw, sbias):
    B, S, _ = x.shape
    ns = S // TM
    hb = TM // HALO
    const2 = lambda b, s: (0, 0)
    const3 = lambda b, s: (0, 0, 0)
    aconv_blk = R_ACONV // (2 * CONV_WIDTH)
    return pl.pallas_call(
        _post_kernel,
        grid=(B, ns),
        in_specs=[
            pl.BlockSpec((None, TM, D_MODEL), lambda b, s: (b, s, 0)),
            pl.BlockSpec((None, TM, REST_WIDTH), lambda b, s: (b, s, 0)),
            pl.BlockSpec((None, HALO, 2 * CONV_WIDTH),
                         lambda b, s: (b, jnp.maximum(s * hb - 1, 0), aconv_blk)),
            pl.BlockSpec((None, HALO, 2 * CONV_WIDTH),
                         lambda b, s: (b, jnp.minimum((s + 1) * hb, S // HALO - 1), aconv_blk)),
            pl.BlockSpec((None, TM, ATT_WIDTH), lambda b, s: (b, s, 0)),
            pl.BlockSpec((None, D_MIX, D_MODEL), lambda b, s: (layer, 0, 0)),
            pl.BlockSpec((1, D_MODEL), const2),
            pl.BlockSpec((CONV_KERNEL, CONV_WIDTH), const2),
            pl.BlockSpec((1, CONV_WIDTH), const2),
            pl.BlockSpec((1, CONV_WIDTH), const2),
            pl.BlockSpec((1, CONV_WIDTH), const2),
            pl.BlockSpec((1, SG_WIDTH), const2),
            pl.BlockSpec((1, SG_WIDTH), const2),
            pl.BlockSpec((SG_HEADS, SG_CHUNK, SG_CHUNK), const3),
            pl.BlockSpec((SG_CHUNK, SG_WIDTH), const2),
        ],
        out_specs=pl.BlockSpec((None, TM, D_MODEL), lambda b, s: (b, s, 0)),
        out_shape=jax.ShapeDtypeStruct((B, S, D_MODEL), jnp.float32),
        scratch_shapes=[
            pltpu.VMEM((TM + 2 * HALO, CONV_WIDTH), jnp.float32),
            pltpu.VMEM((SUBLANES - 1, TM + 2 * HALO - SUBLANES, CONV_WIDTH), jnp.float32),
            pltpu.VMEM((TM, D_MIX), jnp.bfloat16),
        ],
        compiler_params=pltpu.CompilerParams(
            dimension_semantics=("arbitrary", "arbitrary"), vmem_limit_bytes=VMEM_LIMIT),
        name="post",
    )(x, rest, rest, rest, att, wout, gpost, dw, dwb, cg, cb, sg, sb, sw, sbias)


def _rope_tables(S):
    half = ROPE_AXIS_DIM // 2
    t = jnp.arange(S, dtype=jnp.int32)
    row = (t // GRID_W).astype(jnp.float32)
    col = (t % GRID_W).astype(jnp.float32)
    inv_freq = ROPE_THETA ** (-jnp.arange(half, dtype=jnp.float32) / half)
    ang_r = inv_freq[:, None] * row[None, :]
    ang_c = inv_freq[:, None] * col[None, :]
    cos_tt = jnp.concatenate([jnp.cos(ang_r)] * 2 + [jnp.cos(ang_c)] * 2, axis=0)
    sin_tt = jnp.concatenate([-jnp.sin(ang_r), jnp.sin(ang_r), -jnp.sin(ang_c), jnp.sin(ang_c)], axis=0)
    return cos_tt, sin_tt


def kernel(x, pre_norm, post_norm, w_in, w_out, q_norm, k_norm, conv_dw, conv_dw_b,
           conv_ln_g, conv_ln_b, sg_ln_g, sg_ln_b, sg_w, sg_b):
    B, S, _ = x.shape
    depth = w_in.shape[0]
    cos_tt, sin_tt = _rope_tables(S)
    row2 = lambda a: a.reshape(1, -1)
    col_lanes = lambda a: jnp.broadcast_to(a[:, None], (HEAD_DIM, LANES))
    w_in_bf, w_out_bf = _bf16(w_in), _bf16(w_out)
    for l in range(depth):
        qt, k, vt, rest = _in_proj(
            x, row2(pre_norm[l]), w_in_bf, l, col_lanes(q_norm[l]), col_lanes(k_norm[l]), cos_tt, sin_tt)
        att = _attention(qt, k, vt)
        sbias = jnp.repeat(sg_b[l].T, HEAD_DIM, axis=1)
        x = _post(x, rest, att, w_out_bf, l, row2(post_norm[l]),
                  conv_dw[l], row2(conv_dw_b[l]), row2(conv_ln_g[l]), row2(conv_ln_b[l]),
                  row2(sg_ln_g[l]), row2(sg_ln_b[l]), _bf16(sg_w[l]), sbias)
    return x
```

```python
import math

import jax
import jax.numpy as jnp
from jax import lax
from jax.experimental import pallas as pl
from jax.experimental.pallas import tpu as pltpu

D_MODEL = 1024
HEAD_DIM = 64
GRID_W = 64
EPS = 1e-6
ATT_HEADS = 8
ATT_KV_HEADS = 2
GQA = ATT_HEADS // ATT_KV_HEADS
ATT_WIDTH = ATT_HEADS * HEAD_DIM
KV_WIDTH = ATT_KV_HEADS * HEAD_DIM
ROPE_THETA = 10000.0
ROPE_AXIS_DIM = HEAD_DIM // 2
CONV_WIDTH = 256
CONV_KERNEL = 31
CONV_PAD = CONV_KERNEL // 2
SG_HEADS = 4
SG_WIDTH = SG_HEADS * HEAD_DIM
SG_CHUNK = 128
D_MIX = ATT_WIDTH + CONV_WIDTH + SG_WIDTH
QKV_WIDTH = ATT_WIDTH + 2 * KV_WIDTH
REST_WIDTH = ATT_WIDTH + 3 * CONV_WIDTH + 3 * SG_WIDTH
D_IN = QKV_WIDTH + REST_WIDTH

R_GATT = 0
R_ACONV = ATT_WIDTH
R_GCONV = R_ACONV + 2 * CONV_WIDTH
R_U = R_GCONV + CONV_WIDTH
R_V = R_U + SG_WIDTH
R_GSG = R_V + SG_WIDTH

LANES = 128
SUBLANES = 8
HALO = 16
TM = 512
TM_IN = 1024
TQ = 1024
TK = 256
ATT_UNROLL = 31
NQ = GQA * TQ
LANE_GROUP = 256
V_ROWS = HEAD_DIM + 16
SCORE_SCALE = HEAD_DIM ** -0.5 * math.log2(math.e)
VMEM_LIMIT = 56 * 1024 * 1024


def _bf16(x):
    return x.astype(jnp.bfloat16)


def _head_norm_rope(xh, gain, cost, sint):
    quarter = ROPE_AXIS_DIM // 2
    xn = xh * lax.rsqrt(jnp.mean(xh * xh, axis=0, keepdims=True) + EPS) * gain
    partner = jnp.concatenate([xn[quarter:2 * quarter], xn[0:quarter],
                               xn[3 * quarter:4 * quarter], xn[2 * quarter:3 * quarter]], axis=0)
    return xn * cost + partner * sint


def _in_proj_kernel(x_ref, g_ref, w_ref, gq_ref, gk_ref, cost_ref, sint_ref,
                    qt_ref, k_ref, vt_ref, rest_ref):
    x = x_ref[...]
    rstd = lax.rsqrt(jnp.mean(x * x, axis=-1, keepdims=True) + EPS)
    h = _bf16(x * g_ref[...])

    q = jnp.dot(h, w_ref[:, 0:ATT_WIDTH], preferred_element_type=jnp.float32) * rstd
    cost = cost_ref[...]
    sint = sint_ref[...]
    gq = jnp.concatenate([gq_ref[...]] * (TM_IN // LANES), axis=1)
    gk = jnp.concatenate([gk_ref[...]] * (TM_IN // LANES), axis=1)
    for j in range(ATT_WIDTH // LANES):
        qt = q[:, j * LANES:(j + 1) * LANES].T
        for hh in range(LANES // HEAD_DIM):
            row0 = j * LANES + hh * HEAD_DIM
            r = _head_norm_rope(qt[hh * HEAD_DIM:(hh + 1) * HEAD_DIM, :], gq, cost, sint)
            qt_ref[row0:row0 + HEAD_DIM, :] = _bf16(r * SCORE_SCALE)

    kv = jnp.dot(h, w_ref[:, ATT_WIDTH:QKV_WIDTH], preferred_element_type=jnp.float32) * rstd
    kt = kv[:, 0:KV_WIDTH].T
    kr = jnp.concatenate([_head_norm_rope(kt[hd * HEAD_DIM:(hd + 1) * HEAD_DIM, :], gk, cost, sint)
                          for hd in range(ATT_KV_HEADS)], axis=0)
    k_ref[...] = _bf16(kr.T)

    vt = _bf16(kv[:, KV_WIDTH:2 * KV_WIDTH].T)
    for c in range(TM_IN // TK):
        for hd in range(ATT_KV_HEADS):
            vt_ref[c, hd, 0:HEAD_DIM, :] = vt[hd * HEAD_DIM:(hd + 1) * HEAD_DIM, c * TK:(c + 1) * TK]
            vt_ref[c, hd, HEAD_DIM:V_ROWS, :] = jnp.ones((V_ROWS - HEAD_DIM, TK), jnp.bfloat16)

    rest_ref[...] = jnp.dot(h, w_ref[:, QKV_WIDTH:D_IN], preferred_element_type=jnp.float32) * rstd


def _in_proj(x, g, w, layer, gq, gk, cos_tt, sin_tt):
    B, S, _ = x.shape
    ns = S // TM_IN
    const = lambda b, s: (0, 0)
    return pl.pallas_call(
        _in_proj_kernel,
        grid=(B, ns),
        in_specs=[
            pl.BlockSpec((None, TM_IN, D_MODEL), lambda b, s: (b, s, 0)),
            pl.BlockSpec((1, D_MODEL), const),
            pl.BlockSpec((None, D_MODEL, D_IN), lambda b, s: (layer, 0, 0)),
            pl.BlockSpec((HEAD_DIM, LANES), const),
            pl.BlockSpec((HEAD_DIM, LANES), const),
            pl.BlockSpec((HEAD_DIM, TM_IN), lambda b, s: (0, s)),
            pl.BlockSpec((HEAD_DIM, TM_IN), lambda b, s: (0, s)),
        ],
        out_specs=[
            pl.BlockSpec((None, ATT_WIDTH, TM_IN), lambda b, s: (b, 0, s)),
            pl.BlockSpec((None, TM_IN, KV_WIDTH), lambda b, s: (b, s, 0)),
            pl.BlockSpec((None, TM_IN // TK, ATT_KV_HEADS, V_ROWS, TK), lambda b, s: (b, s, 0, 0, 0)),
            pl.BlockSpec((None, TM_IN, REST_WIDTH), lambda b, s: (b, s, 0)),
        ],
        out_shape=[
            jax.ShapeDtypeStruct((B, ATT_WIDTH, S), jnp.bfloat16),
            jax.ShapeDtypeStruct((B, S, KV_WIDTH), jnp.bfloat16),
            jax.ShapeDtypeStruct((B, S // TK, ATT_KV_HEADS, V_ROWS, TK), jnp.bfloat16),
            jax.ShapeDtypeStruct((B, S, REST_WIDTH), jnp.float32),
        ],
        compiler_params=pltpu.CompilerParams(
            dimension_semantics=("arbitrary", "arbitrary"), vmem_limit_bytes=VMEM_LIMIT),
        name="in_proj",
    )(x, g, w, gq, gk, cos_tt, sin_tt)


def _attention_kernel(qt_ref, k_ref, vt_ref, o_ref, qext_ref, s_ref, p_ref, acc_ref, m_ref, mx_ref):
    kv = pl.program_id(1)
    n = k_ref.shape[0] // TK
    for g in range(GQA):
        qg = qt_ref[g * HEAD_DIM:(g + 1) * HEAD_DIM, :]
        zero = jnp.zeros_like(qg)
        qext_ref[0:HEAD_DIM, g * TQ:(g + 1) * TQ] = jnp.where(kv == 0, qg, zero)
        qext_ref[HEAD_DIM:2 * HEAD_DIM, g * TQ:(g + 1) * TQ] = jnp.where(kv == 1, qg, zero)

    def scores(j, g, cols):
        kblk = k_ref[pl.ds(pl.multiple_of(j * TK, TK), TK), :]
        s = jnp.dot(kblk, qext_ref[:, cols], preferred_element_type=jnp.float32)
        s_ref[g] = s
        mx_ref[:, cols] = jnp.max(s, axis=0, keepdims=True)

    def values(j, g):
        return jnp.dot(vt_ref[j], p_ref[g], preferred_element_type=jnp.float32)

    groups = [(g, slice(g * LANE_GROUP, (g + 1) * LANE_GROUP)) for g in range(NQ // LANE_GROUP)]

    def stage(j, first=False, last=False):
        for g, cols in groups:
            if first:
                m_new = mx_ref[:, cols]
            else:
                pv = values(j - 1, g)
                m_old = m_ref[:, cols]
                m_new = jnp.maximum(m_old, mx_ref[:, cols])
                acc_ref[:, cols] = jnp.exp2(m_old - m_new) * (acc_ref[:, cols] + pv)
            m_ref[:, cols] = m_new
            p_ref[g] = _bf16(jnp.exp2(s_ref[g] - m_new))
            if not last:
                scores(j + 1, g, cols)

    acc_ref[...] = jnp.zeros_like(acc_ref)
    for g, cols in groups:
        scores(0, g, cols)
    stage(0, first=True)

    def step(j, carry):
        stage(j)
        return carry

    assert (n - 2) % ATT_UNROLL == 0
    lax.fori_loop(1, n - 1, step, 0, unroll=ATT_UNROLL)
    stage(n - 1, last=True)
    for g, cols in groups:
        acc_ref[:, cols] = acc_ref[:, cols] + values(n - 1, g)
    o = acc_ref[0:HEAD_DIM, :] / acc_ref[HEAD_DIM:HEAD_DIM + 1, :]
    o4 = jnp.concatenate([o[:, g * TQ:(g + 1) * TQ] for g in range(GQA)], axis=0)
    o_ref[...] = _bf16(o4.T)


def _attention(qt, k, vt):
    B, _, S = qt.shape
    nk = vt.shape[1]
    assert vt.shape[-1] == TK
    return pl.pallas_call(
        _attention_kernel,
        grid=(B, ATT_KV_HEADS, S // TQ),
        in_specs=[
            pl.BlockSpec((None, GQA * HEAD_DIM, TQ), lambda b, kv, qi: (b, kv, qi)),
            pl.BlockSpec((None, S, KV_WIDTH), lambda b, kv, qi: (b, 0, 0)),
            pl.BlockSpec((None, nk, None, V_ROWS, TK), lambda b, kv, qi: (b, 0, kv, 0, 0)),
        ],
        out_specs=pl.BlockSpec((None, TQ, GQA * HEAD_DIM), lambda b, kv, qi: (b, qi, kv)),
        out_shape=jax.ShapeDtypeStruct((B, S, ATT_WIDTH), jnp.bfloat16),
        scratch_shapes=[
            pltpu.VMEM((KV_WIDTH, NQ), jnp.bfloat16),
            pltpu.VMEM((NQ // LANE_GROUP, TK, LANE_GROUP), jnp.float32),
            pltpu.VMEM((NQ // LANE_GROUP, TK, LANE_GROUP), jnp.bfloat16),
            pltpu.VMEM((V_ROWS, NQ), jnp.float32),
            pltpu.VMEM((1, NQ), jnp.float32),
            pltpu.VMEM((1, NQ), jnp.float32),
        ],
        compiler_params=pltpu.CompilerParams(
            dimension_semantics=("arbitrary", "arbitrary", "arbitrary"),
            vmem_limit_bytes=VMEM_LIMIT),
        name="attention",
    )(qt, k, vt)


def _silu(x):
    h = 0.5 * x
    return h + h * jnp.tanh(h)


def _gelu(x):
    return 0.5 * x * (1.0 + lax.erf(x * (2.0 ** -0.5)))


def _layer_norm(x, g, b):
    mu = jnp.mean(x, axis=-1, keepdims=True)
    xc = x - mu
    var = jnp.mean(xc * xc, axis=-1, keepdims=True)
    return xc * lax.rsqrt(var + EPS) * g + b


def _glu(a):
    return a[:, :CONV_WIDTH] * jax.nn.sigmoid(a[:, CONV_WIDTH:])


CONV_ROWS = 64


def _post_kernel(x_ref, rest_ref, prev_ref, next_ref, att_ref, wout_ref, gpost_ref,
                 dw_ref, dwb_ref, cg_ref, cb_ref, sg_ref, sb_ref, sw_ref, sbias_ref,
                 o_ref, hext_ref, hsh_ref, cat_ref):
    si = pl.program_id(1)
    last = pl.num_programs(1) - 1

    cat_ref[:, 0:ATT_WIDTH] = _bf16(att_ref[...] * _silu(rest_ref[:, R_GATT:R_GATT + ATT_WIDTH]))
    o_ref[...] = jnp.dot(cat_ref[:, 0:ATT_WIDTH], wout_ref[0:ATT_WIDTH, :], preferred_element_type=jnp.float32)

    hext_ref[0:HALO, :] = jnp.where(si > 0, _glu(prev_ref[...]), 0.0)
    hext_ref[HALO:HALO + TM, :] = _glu(rest_ref[:, R_ACONV:R_ACONV + 2 * CONV_WIDTH])
    hext_ref[HALO + TM:2 * HALO + TM, :] = jnp.where(si < last, _glu(next_ref[...]), 0.0)
    for r in range(1, SUBLANES):
        hsh_ref[r - 1] = hext_ref[r:r + TM + 2 * HALO - SUBLANES, :]
    for rb in range(TM // CONV_ROWS):
        acc = jnp.zeros((CONV_ROWS, CONV_WIDTH), jnp.float32) + dwb_ref[...]
        for j in range(CONV_KERNEL):
            r = (j + HALO - CONV_PAD) % SUBLANES
            start = rb * CONV_ROWS + j + HALO - CONV_PAD - r
            tap = (hext_ref[start:start + CONV_ROWS, :] if r == 0
                   else hsh_ref[r - 1, start:start + CONV_ROWS, :])
            acc = acc + tap * dw_ref[j:j + 1, :]
        rows = slice(rb * CONV_ROWS, (rb + 1) * CONV_ROWS)
        c = _silu(_layer_norm(acc, cg_ref[...], cb_ref[...]))
        c = c * _silu(rest_ref[rows, R_GCONV:R_GCONV + CONV_WIDTH])
        cat_ref[rows, ATT_WIDTH:ATT_WIDTH + CONV_WIDTH] = _bf16(c)
    o_ref[...] += jnp.dot(cat_ref[:, ATT_WIDTH:ATT_WIDTH + CONV_WIDTH],
                          wout_ref[ATT_WIDTH:ATT_WIDTH + CONV_WIDTH, :], preferred_element_type=jnp.float32)

    lane = lax.broadcasted_iota(jnp.int32, (1, SG_WIDTH), 1)
    for c in range(TM // SG_CHUNK):
        rows = slice(c * SG_CHUNK, (c + 1) * SG_CHUNK)
        vln = _bf16(_layer_norm(_gelu(rest_ref[rows, R_V:R_V + SG_WIDTH]), sg_ref[...], sb_ref[...]))
        mixed = jnp.dot(sw_ref[SG_HEADS - 1], vln, preferred_element_type=jnp.float32)
        for hd in range(SG_HEADS - 1):
            mh = jnp.dot(sw_ref[hd], vln, preferred_element_type=jnp.float32)
            mixed = jnp.where(lane // HEAD_DIM == hd, mh, mixed)
        sgu = _gelu(rest_ref[rows, R_U:R_U + SG_WIDTH]) * (mixed + sbias_ref[...])
        sgu = sgu * _silu(rest_ref[rows, R_GSG:R_GSG + SG_WIDTH])
        cat_ref[rows, ATT_WIDTH + CONV_WIDTH:D_MIX] = _bf16(sgu)

    mix = o_ref[...] + jnp.dot(cat_ref[:, ATT_WIDTH + CONV_WIDTH:D_MIX], wout_ref[ATT_WIDTH + CONV_WIDTH:D_MIX, :],
                               preferred_element_type=jnp.float32)
    ms = jnp.mean(mix * mix, axis=-1, keepdims=True)
    o_ref[...] = x_ref[...] + mix * lax.rsqrt(ms + EPS) * gpost_ref[...]


def _post(x, rest, att, wout, layer, gpost, dw, dwb, cg, cb, sg, sb, sw, sbias):
    B, S, _ = x.shape
    ns = S // TM
    hb = TM // HALO
    const2 = lambda b, s: (0, 0)
    const3 = lambda b, s: (0, 0, 0)
    aconv_blk = R_ACONV // (2 * CONV_WIDTH)
    return pl.pallas_call(
        _post_kernel,
        grid=(B, ns),
        in_specs=[
            pl.BlockSpec((None, TM, D_MODEL), lambda b, s: (b, s, 0)),
            pl.BlockSpec((None, TM, REST_WIDTH), lambda b, s: (b, s, 0)),
            pl.BlockSpec((None, HALO, 2 * CONV_WIDTH),
                         lambda b, s: (b, jnp.maximum(s * hb - 1, 0), aconv_blk)),
            pl.BlockSpec((None, HALO, 2 * CONV_WIDTH),
                         lambda b, s: (b, jnp.minimum((s + 1) * hb, S // HALO - 1), aconv_blk)),
            pl.BlockSpec((None, TM, ATT_WIDTH), lambda b, s: (b, s, 0)),
            pl.BlockSpec((None, D_MIX, D_MODEL), lambda b, s: (layer, 0, 0)),
            pl.BlockSpec((1, D_MODEL), const2),
            pl.BlockSpec((CONV_KERNEL, CONV_WIDTH), const2),
            pl.BlockSpec((1, CONV_WIDTH), const2),
            pl.BlockSpec((1, CONV_WIDTH), const2),
            pl.BlockSpec((1, CONV_WIDTH), const2),
            pl.BlockSpec((1, SG_WIDTH), const2),
            pl.BlockSpec((1, SG_WIDTH), const2),
            pl.BlockSpec((SG_HEADS, SG_CHUNK, SG_CHUNK), const3),
            pl.BlockSpec((SG_CHUNK, SG_WIDTH), const2),
        ],
        out_specs=pl.BlockSpec((None, TM, D_MODEL), lambda b, s: (b, s, 0)),
        out_shape=jax.ShapeDtypeStruct((B, S, D_MODEL), jnp.float32),
        scratch_shapes=[
            pltpu.VMEM((TM + 2 * HALO, CONV_WIDTH), jnp.float32),
            pltpu.VMEM((SUBLANES - 1, TM + 2 * HALO - SUBLANES, CONV_WIDTH), jnp.float32),
            pltpu.VMEM((TM, D_MIX), jnp.bfloat16),
        ],
        compiler_params=pltpu.CompilerParams(
            dimension_semantics=("arbitrary", "arbitrary"), vmem_limit_bytes=VMEM_LIMIT),
        name="post",
    )(x, rest, rest, rest, att, wout, gpost, dw, dwb, cg, cb, sg, sb, sw, sbias)


def _rope_tables(S):
    half = ROPE_AXIS_DIM // 2
    t = jnp.arange(S, dtype=jnp.int32)
    row = (t // GRID_W).astype(jnp.float32)
    col = (t % GRID_W).astype(jnp.float32)
    inv_freq = ROPE_THETA ** (-jnp.arange(half, dtype=jnp.float32) / half)
    ang_r = inv_freq[:, None] * row[None, :]
    ang_c = inv_freq[:, None] * col[None, :]
    cos_tt = jnp.concatenate([jnp.cos(ang_r)] * 2 + [jnp.cos(ang_c)] * 2, axis=0)
    sin_tt = jnp.concatenate([-jnp.sin(ang_r), jnp.sin(ang_r), -jnp.sin(ang_c), jnp.sin(ang_c)], axis=0)
    return cos_tt, sin_tt


def kernel(x, pre_norm, post_norm, w_in, w_out, q_norm, k_norm, conv_dw, conv_dw_b,
           conv_ln_g, conv_ln_b, sg_ln_g, sg_ln_b, sg_w, sg_b):
    B, S, _ = x.shape
    depth = w_in.shape[0]
    cos_tt, sin_tt = _rope_tables(S)
    row2 = lambda a: a.reshape(1, -1)
    col_lanes = lambda a: jnp.broadcast_to(a[:, None], (HEAD_DIM, LANES))
    w_in_bf, w_out_bf = _bf16(w_in), _bf16(w_out)
    for l in range(depth):
        qt, k, vt, rest = _in_proj(
            x, row2(pre_norm[l]), w_in_bf, l, col_lanes(q_norm[l]), col_lanes(k_norm[l]), cos_tt, sin_tt)
        att = _attention(qt, k, vt)
        sbias = jnp.repeat(sg_b[l].T, HEAD_DIM, axis=1)
        x = _post(x, rest, att, w_out_bf, l, row2(post_norm[l]),
                  conv_dw[l], row2(conv_dw_b[l]), row2(conv_ln_g[l]), row2(conv_ln_b[l]),
                  row2(sg_ln_g[l]), row2(sg_ln_b[l]), _bf16(sg_w[l]), sbias)
    return x
```

```python
import math

import jax
import jax.numpy as jnp
from jax import lax
from jax.experimental import pallas as pl
from jax.experimental.pallas import tpu as pltpu

D_MODEL = 1024
HEAD_DIM = 64
GRID_W = 64
EPS = 1e-6
ATT_HEADS = 8
ATT_KV_HEADS = 2
GQA = ATT_HEADS // ATT_KV_HEADS
ATT_WIDTH = ATT_HEADS * HEAD_DIM
KV_WIDTH = ATT_KV_HEADS * HEAD_DIM
ROPE_THETA = 10000.0
ROPE_AXIS_DIM = HEAD_DIM // 2
CONV_WIDTH = 256
CONV_KERNEL = 31
CONV_PAD = CONV_KERNEL // 2
SG_HEADS = 4
SG_WIDTH = SG_HEADS * HEAD_DIM
SG_CHUNK = 128
D_MIX = ATT_WIDTH + CONV_WIDTH + SG_WIDTH
QKV_WIDTH = ATT_WIDTH + 2 * KV_WIDTH
REST_WIDTH = ATT_WIDTH + 3 * CONV_WIDTH + 3 * SG_WIDTH
D_IN = QKV_WIDTH + REST_WIDTH

R_GATT = 0
R_ACONV = ATT_WIDTH
R_GCONV = R_ACONV + 2 * CONV_WIDTH
R_U = R_GCONV + CONV_WIDTH
R_V = R_U + SG_WIDTH
R_GSG = R_V + SG_WIDTH

LANES = 128
SUBLANES = 8
HALO = 16
TM = 512
TM_IN = 1024
TQ = 1024
TK = 512
ATT_UNROLL = 15
NQ = GQA * TQ
LANE_GROUP = 256
V_ROWS = HEAD_DIM + 16
SCORE_SCALE = HEAD_DIM ** -0.5 * math.log2(math.e)
VMEM_LIMIT = 56 * 1024 * 1024


def _bf16(x):
    return x.astype(jnp.bfloat16)


def _head_norm_rope(xh, gain, cost, sint):
    quarter = ROPE_AXIS_DIM // 2
    xn = xh * lax.rsqrt(jnp.mean(xh * xh, axis=0, keepdims=True) + EPS) * gain
    partner = jnp.concatenate([xn[quarter:2 * quarter], xn[0:quarter],
                               xn[3 * quarter:4 * quarter], xn[2 * quarter:3 * quarter]], axis=0)
    return xn * cost + partner * sint


def _in_proj_kernel(x_ref, g_ref, w_ref, gq_ref, gk_ref, cost_ref, sint_ref,
                    qt_ref, k_ref, vt_ref, rest_ref):
    x = x_ref[...]
    rstd = lax.rsqrt(jnp.mean(x * x, axis=-1, keepdims=True) + EPS)
    h = _bf16(x * g_ref[...])

    q = jnp.dot(h, w_ref[:, 0:ATT_WIDTH], preferred_element_type=jnp.float32) * rstd
    cost = cost_ref[...]
    sint = sint_ref[...]
    gq = jnp.concatenate([gq_ref[...]] * (TM_IN // LANES), axis=1)
    gk = jnp.concatenate([gk_ref[...]] * (TM_IN // LANES), axis=1)
    for j in range(ATT_WIDTH // LANES):
        qt = q[:, j * LANES:(j + 1) * LANES].T
        for hh in range(LANES // HEAD_DIM):
            row0 = j * LANES + hh * HEAD_DIM
            r = _head_norm_rope(qt[hh * HEAD_DIM:(hh + 1) * HEAD_DIM, :], gq, cost, sint)
            qt_ref[row0:row0 + HEAD_DIM, :] = _bf16(r * SCORE_SCALE)

    kv = jnp.dot(h, w_ref[:, ATT_WIDTH:QKV_WIDTH], preferred_element_type=jnp.float32) * rstd
    kt = kv[:, 0:KV_WIDTH].T
    kr = jnp.concatenate([_head_norm_rope(kt[hd * HEAD_DIM:(hd + 1) * HEAD_DIM, :], gk, cost, sint)
                          for hd in range(ATT_KV_HEADS)], axis=0)
    k_ref[...] = _bf16(kr.T)

    vt = _bf16(kv[:, KV_WIDTH:2 * KV_WIDTH].T)
    for c in range(TM_IN // TK):
        for hd in range(ATT_KV_HEADS):
            vt_ref[c, hd, 0:HEAD_DIM, :] = vt[hd * HEAD_DIM:(hd + 1) * HEAD_DIM, c * TK:(c + 1) * TK]
            vt_ref[c, hd, HEAD_DIM:V_ROWS, :] = jnp.ones((V_ROWS - HEAD_DIM, TK), jnp.bfloat16)

    rest_ref[...] = jnp.dot(h, w_ref[:, QKV_WIDTH:D_IN], preferred_element_type=jnp.float32) * rstd


def _in_proj(x, g, w, layer, gq, gk, cos_tt, sin_tt):
    B, S, _ = x.shape
    ns = S // TM_IN
    const = lambda b, s: (0, 0)
    return pl.pallas_call(
        _in_proj_kernel,
        grid=(B, ns),
        in_specs=[
            pl.BlockSpec((None, TM_IN, D_MODEL), lambda b, s: (b, s, 0)),
            pl.BlockSpec((1, D_MODEL), const),
            pl.BlockSpec((None, D_MODEL, D_IN), lambda b, s: (layer, 0, 0)),
            pl.BlockSpec((HEAD_DIM, LANES), const),
            pl.BlockSpec((HEAD_DIM, LANES), const),
            pl.BlockSpec((HEAD_DIM, TM_IN), lambda b, s: (0, s)),
            pl.BlockSpec((HEAD_DIM, TM_IN), lambda b, s: (0, s)),
        ],
        out_specs=[
            pl.BlockSpec((None, ATT_WIDTH, TM_IN), lambda b, s: (b, 0, s)),
            pl.BlockSpec((None, TM_IN, KV_WIDTH), lambda b, s: (b, s, 0)),
            pl.BlockSpec((None, TM_IN // TK, ATT_KV_HEADS, V_ROWS, TK), lambda b, s: (b, s, 0, 0, 0)),
            pl.BlockSpec((None, TM_IN, REST_WIDTH), lambda b, s: (b, s, 0)),
        ],
        out_shape=[
            jax.ShapeDtypeStruct((B, ATT_WIDTH, S), jnp.bfloat16),
            jax.ShapeDtypeStruct((B, S, KV_WIDTH), jnp.bfloat16),
            jax.ShapeDtypeStruct((B, S // TK, ATT_KV_HEADS, V_ROWS, TK), jnp.bfloat16),
            jax.ShapeDtypeStruct((B, S, REST_WIDTH), jnp.float32),
        ],
        compiler_params=pltpu.CompilerParams(
            dimension_semantics=("arbitrary", "arbitrary"), vmem_limit_bytes=VMEM_LIMIT),
        name="in_proj",
    )(x, g, w, gq, gk, cos_tt, sin_tt)


def _attention_kernel(qt_ref, k_ref, vt_ref, o_ref, qext_ref, s_ref, p_ref, acc_ref, m_ref, mx_ref):
    kv = pl.program_id(1)
    n = k_ref.shape[0] // TK
    for g in range(GQA):
        qg = qt_ref[g * HEAD_DIM:(g + 1) * HEAD_DIM, :]
        zero = jnp.zeros_like(qg)
        qext_ref[0:HEAD_DIM, g * TQ:(g + 1) * TQ] = jnp.where(kv == 0, qg, zero)
        qext_ref[HEAD_DIM:2 * HEAD_DIM, g * TQ:(g + 1) * TQ] = jnp.where(kv == 1, qg, zero)

    def scores(j, g, cols):
        kblk = k_ref[pl.ds(pl.multiple_of(j * TK, TK), TK), :]
        s = jnp.dot(kblk, qext_ref[:, cols], preferred_element_type=jnp.float32)
        s_ref[g] = s
        mx_ref[:, cols] = jnp.max(s, axis=0, keepdims=True)

    def values(j, g):
        return jnp.dot(vt_ref[j], p_ref[g], preferred_element_type=jnp.float32)

    groups = [(g, slice(g * LANE_GROUP, (g + 1) * LANE_GROUP)) for g in range(NQ // LANE_GROUP)]

    def stage(j, first=False, last=False):
        for g, cols in groups:
            if first:
                m_new = mx_ref[:, cols]
            else:
                pv = values(j - 1, g)
                m_old = m_ref[:, cols]
                m_new = jnp.maximum(m_old, mx_ref[:, cols])
                acc_ref[:, cols] = jnp.exp2(m_old - m_new) * (acc_ref[:, cols] + pv)
            m_ref[:, cols] = m_new
            p_ref[g] = _bf16(jnp.exp2(s_ref[g] - m_new))
            if not last:
                scores(j + 1, g, cols)

    acc_ref[...] = jnp.zeros_like(acc_ref)
    for g, cols in groups:
        scores(0, g, cols)
    stage(0, first=True)

    def step(j, carry):
        stage(j)
        return carry

    assert (n - 2) % ATT_UNROLL == 0
    lax.fori_loop(1, n - 1, step, 0, unroll=ATT_UNROLL)
    stage(n - 1, last=True)
    for g, cols in groups:
        acc_ref[:, cols] = acc_ref[:, cols] + values(n - 1, g)
    o = acc_ref[0:HEAD_DIM, :] / acc_ref[HEAD_DIM:HEAD_DIM + 1, :]
    o4 = jnp.concatenate([o[:, g * TQ:(g + 1) * TQ] for g in range(GQA)], axis=0)
    o_ref[...] = o4.T


def _attention(qt, k, vt):
    B, _, S = qt.shape
    nk = vt.shape[1]
    assert vt.shape[-1] == TK
    return pl.pallas_call(
        _attention_kernel,
        grid=(B, ATT_KV_HEADS, S // TQ),
        in_specs=[
            pl.BlockSpec((None, GQA * HEAD_DIM, TQ), lambda b, kv, qi: (b, kv, qi)),
            pl.BlockSpec((None, S, KV_WIDTH), lambda b, kv, qi: (b, 0, 0)),
            pl.BlockSpec((None, nk, None, V_ROWS, TK), lambda b, kv, qi: (b, 0, kv, 0, 0)),
        ],
        out_specs=pl.BlockSpec((None, TQ, GQA * HEAD_DIM), lambda b, kv, qi: (b, qi, kv)),
        out_shape=jax.ShapeDtypeStruct((B, S, ATT_WIDTH), jnp.float32),
        scratch_shapes=[
            pltpu.VMEM((KV_WIDTH, NQ), jnp.bfloat16),
            pltpu.VMEM((NQ // LANE_GROUP, TK, LANE_GROUP), jnp.float32),
            pltpu.VMEM((NQ // LANE_GROUP, TK, LANE_GROUP), jnp.bfloat16),
            pltpu.VMEM((V_ROWS, NQ), jnp.float32),
            pltpu.VMEM((1, NQ), jnp.float32),
            pltpu.VMEM((1, NQ), jnp.float32),
        ],
        compiler_params=pltpu.CompilerParams(
            dimension_semantics=("arbitrary", "arbitrary", "arbitrary"),
            vmem_limit_bytes=VMEM_LIMIT),
        name="attention",
    )(qt, k, vt)


def _silu(x):
    h = 0.5 * x
    return h + h * jnp.tanh(h)


def _gelu(x):
    return 0.5 * x * (1.0 + lax.erf(x * (2.0 ** -0.5)))


def _layer_norm(x, g, b):
    mu = jnp.mean(x, axis=-1, keepdims=True)
    xc = x - mu
    var = jnp.mean(xc * xc, axis=-1, keepdims=True)
    return xc * lax.rsqrt(var + EPS) * g + b


def _glu(a):
    return a[:, :CONV_WIDTH] * jax.nn.sigmoid(a[:, CONV_WIDTH:])


CONV_ROWS = 64


def _post_kernel(x_ref, rest_ref, prev_ref, next_ref, att_ref, wout_ref, gpost_ref,
                 dw_ref, dwb_ref, cg_ref, cb_ref, sg_ref, sb_ref, sw_ref, sbias_ref,
                 o_ref, hext_ref, hsh_ref, cat_ref):
    si = pl.program_id(1)
    last = pl.num_programs(1) - 1

    cat_ref[:, 0:ATT_WIDTH] = _bf16(att_ref[...] * _silu(rest_ref[:, R_GATT:R_GATT + ATT_WIDTH]))
    o_ref[...] = jnp.dot(cat_ref[:, 0:ATT_WIDTH], wout_ref[0:ATT_WIDTH, :], preferred_element_type=jnp.float32)

    hext_ref[0:HALO, :] = jnp.where(si > 0, _glu(prev_ref[...]), 0.0)
    hext_ref[HALO:HALO + TM, :] = _glu(rest_ref[:, R_ACONV:R_ACONV + 2 * CONV_WIDTH])
    hext_ref[HALO + TM:2 * HALO + TM, :] = jnp.where(si < last, _glu(next_ref[...]), 0.0)
    for r in range(1, SUBLANES):
        hsh_ref[r - 1] = hext_ref[r:r + TM + 2 * HALO - SUBLANES, :]
    for rb in range(TM // CONV_ROWS):
        acc = jnp.zeros((CONV_ROWS, CONV_WIDTH), jnp.float32) + dwb_ref[...]
        for j in range(CONV_KERNEL):
            r = (j + HALO - CONV_PAD) % SUBLANES
            start = rb * CONV_ROWS + j + HALO - CONV_PAD - r
            tap = (hext_ref[start:start + CONV_ROWS, :] if r == 0
                   else hsh_ref[r - 1, start:start + CONV_ROWS, :])
            acc = acc + tap * dw_ref[j:j + 1, :]
        rows = slice(rb * CONV_ROWS, (rb + 1) * CONV_ROWS)
        c = _silu(_layer_norm(acc, cg_ref[...], cb_ref[...]))
        c = c * _silu(rest_ref[rows, R_GCONV:R_GCONV + CONV_WIDTH])
        cat_ref[rows, ATT_WIDTH:ATT_WIDTH + CONV_WIDTH] = _bf16(c)
    o_ref[...] += jnp.dot(cat_ref[:, ATT_WIDTH:ATT_WIDTH + CONV_WIDTH],
                          wout_ref[ATT_WIDTH:ATT_WIDTH + CONV_WIDTH, :], preferred_element_type=jnp.float32)

    lane = lax.broadcasted_iota(jnp.int32, (1, SG_WIDTH), 1)
    for c in range(TM // SG_CHUNK):
        rows = slice(c * SG_CHUNK, (c + 1) * SG_CHUNK)
        vln = _bf16(_layer_norm(_gelu(rest_ref[rows, R_V:R_V + SG_WIDTH]), sg_ref[...], sb_ref[...]))
        mixed = jnp.dot(sw_ref[SG_HEADS - 1], vln, preferred_element_type=jnp.float32)
        for hd in range(SG_HEADS - 1):
            mh = jnp.dot(sw_ref[hd], vln, preferred_element_type=jnp.float32)
            mixed = jnp.where(lane // HEAD_DIM == hd, mh, mixed)
        sgu = _gelu(rest_ref[rows, R_U:R_U + SG_WIDTH]) * (mixed + sbias_ref[...])
        sgu = sgu * _silu(rest_ref[rows, R_GSG:R_GSG + SG_WIDTH])
        cat_ref[rows, ATT_WIDTH + CONV_WIDTH:D_MIX] = _bf16(sgu)

    mix = o_ref[...] + jnp.dot(cat_ref[:, ATT_WIDTH + CONV_WIDTH:D_MIX], wout_ref[ATT_WIDTH + CONV_WIDTH:D_MIX, :],
                               preferred_element_type=jnp.float32)
    ms = jnp.mean(mix * mix, axis=-1, keepdims=True)
    o_ref[...] = x_ref[...] + mix * lax.rsqrt(ms + EPS) * gpost_ref[...]


def _post(x, rest, att, wout, layer, gpost, dw, dwb, cg, cb, sg, sb, sw, sbias):
    B, S, _ = x.shape
    ns = S // TM
    hb = TM // HALO
    const2 = lambda b, s: (0, 0)
    const3 = lambda b, s: (0, 0, 0)
    aconv_blk = R_ACONV // (2 * CONV_WIDTH)
    return pl.pallas_call(
        _post_kernel,
        grid=(B, ns),
        in_specs=[
            pl.BlockSpec((None, TM, D_MODEL), lambda b, s: (b, s, 0)),
            pl.BlockSpec((None, TM, REST_WIDTH), lambda b, s: (b, s, 0)),
            pl.BlockSpec((None, HALO, 2 * CONV_WIDTH),
                         lambda b, s: (b, jnp.maximum(s * hb - 1, 0), aconv_blk)),
            pl.BlockSpec((None, HALO, 2 * CONV_WIDTH),
                         lambda b, s: (b, jnp.minimum((s + 1) * hb, S // HALO - 1), aconv_blk)),
            pl.BlockSpec((None, TM, ATT_WIDTH), lambda b, s: (b, s, 0)),
            pl.BlockSpec((None, D_MIX, D_MODEL), lambda b, s: (layer, 0, 0)),
            pl.BlockSpec((1, D_MODEL), const2),
            pl.BlockSpec((CONV_KERNEL, CONV_WIDTH), const2),
            pl.BlockSpec((1, CONV_WIDTH), const2),
            pl.BlockSpec((1, CONV_WIDTH), const2),
            pl.BlockSpec((1, CONV_WIDTH), const2),
            pl.BlockSpec((1, SG_WIDTH), const2),
            pl.BlockSpec((1, SG_WIDTH), const2),
            pl.BlockSpec((SG_HEADS, SG_CHUNK, SG_CHUNK), const3),
            pl.BlockSpec((SG_CHUNK, SG_WIDTH), const2),
        ],
        out_specs=pl.BlockSpec((None, TM, D_MODEL), lambda b, s: (b, s, 0)),
        out_shape=jax.ShapeDtypeStruct((B, S, D_MODEL), jnp.float32),
        scratch_shapes=[
            pltpu.VMEM((TM + 2 * HALO, CONV_WIDTH), jnp.float32),
            pltpu.VMEM((SUBLANES - 1, TM + 2 * HALO - SUBLANES, CONV_WIDTH), jnp.float32),
            pltpu.VMEM((TM, D_MIX), jnp.bfloat16),
        ],
        compiler_params=pltpu.CompilerParams(
            dimension_semantics=("arbitrary", "arbitrary"), vmem_limit_bytes=VMEM_LIMIT),
        name="post",
    )(x, rest, rest, rest, att, wout, gpost, dw, dwb, cg, cb, sg, sb, sw, sbias)


def _rope_tables(S):
    half = ROPE_AXIS_DIM // 2
    t = jnp.arange(S, dtype=jnp.int32)
    row = (t // GRID_W).astype(jnp.float32)
    col = (t % GRID_W).astype(jnp.float32)
    inv_freq = ROPE_THETA ** (-jnp.arange(half, dtype=jnp.float32) / half)
    ang_r = inv_freq[:, None] * row[None, :]
    ang_c = inv_freq[:, None] * col[None, :]
    cos_tt = jnp.concatenate([jnp.cos(ang_r)] * 2 + [jnp.cos(ang_c)] * 2, axis=0)
    sin_tt = jnp.concatenate([-jnp.sin(ang_r), jnp.sin(ang_r), -jnp.sin(ang_c), jnp.sin(ang_c)], axis=0)
    return cos_tt, sin_tt


def kernel(x, pre_norm, post_norm, w_in, w_out, q_norm, k_norm, conv_dw, conv_dw_b,
           conv_ln_g, conv_ln_b, sg_ln_g, sg_ln_b, sg_w, sg_b):
    B, S, _ = x.shape
    depth = w_in.shape[0]
    cos_tt, sin_tt = _rope_tables(S)
    row2 = lambda a: a.reshape(1, -1)
    col_lanes = lambda a: jnp.broadcast_to(a[:, None], (HEAD_DIM, LANES))
    w_in_bf, w_out_bf = _bf16(w_in), _bf16(w_out)
    for l in range(depth):
        qt, k, vt, rest = _in_proj(
            x, row2(pre_norm[l]), w_in_bf, l, col_lanes(q_norm[l]), col_lanes(k_norm[l]), cos_tt, sin_tt)
        att = _attention(qt, k, vt)
        sbias = jnp.repeat(sg_b[l].T, HEAD_DIM, axis=1)
        x = _post(x, rest, att, w_out_bf, l, row2(post_norm[l]),
                  conv_dw[l], row2(conv_dw_b[l]), row2(conv_ln_g[l]), row2(conv_ln_b[l]),
                  row2(sg_ln_g[l]), row2(sg_ln_b[l]), _bf16(sg_w[l]), sbias)
    return x
```

```python
import math

import jax
import jax.numpy as jnp
from jax import lax
from jax.experimental import pallas as pl
from jax.experimental.pallas import tpu as pltpu

D_MODEL = 1024
HEAD_DIM = 64
GRID_W = 64
EPS = 1e-6
ATT_HEADS = 8
ATT_KV_HEADS = 2
GQA = ATT_HEADS // ATT_KV_HEADS
ATT_WIDTH = ATT_HEADS * HEAD_DIM
KV_WIDTH = ATT_KV_HEADS * HEAD_DIM
ROPE_THETA = 10000.0
ROPE_AXIS_DIM = HEAD_DIM // 2
CONV_WIDTH = 256
CONV_KERNEL = 31
CONV_PAD = CONV_KERNEL // 2
SG_HEADS = 4
SG_WIDTH = SG_HEADS * HEAD_DIM
SG_CHUNK = 128
D_MIX = ATT_WIDTH + CONV_WIDTH + SG_WIDTH
QKV_WIDTH = ATT_WIDTH + 2 * KV_WIDTH
REST_WIDTH = ATT_WIDTH + 3 * CONV_WIDTH + 3 * SG_WIDTH
D_IN = QKV_WIDTH + REST_WIDTH

R_GATT = 0
R_ACONV = ATT_WIDTH
R_GCONV = R_ACONV + 2 * CONV_WIDTH
R_U = R_GCONV + CONV_WIDTH
R_V = R_U + SG_WIDTH
R_GSG = R_V + SG_WIDTH

LANES = 128
SUBLANES = 8
HALO = 16
TM = 512
TM_IN = 1024
TQ = 1024
TK = 256
ATT_UNROLL = 31
NQ = GQA * TQ
LANE_GROUP = 256
V_ROWS = HEAD_DIM + 16
SCORE_SCALE = HEAD_DIM ** -0.5 * math.log2(math.e)
VMEM_LIMIT = 56 * 1024 * 1024


def _bf16(x):
    return x.astype(jnp.bfloat16)


def _head_norm_rope(xh, gain, cost, sint):
    quarter = ROPE_AXIS_DIM // 2
    xn = xh * lax.rsqrt(jnp.mean(xh * xh, axis=0, keepdims=True) + EPS) * gain
    partner = jnp.concatenate([xn[quarter:2 * quarter], xn[0:quarter],
                               xn[3 * quarter:4 * quarter], xn[2 * quarter:3 * quarter]], axis=0)
    return xn * cost + partner * sint


def _in_proj_kernel(x_ref, g_ref, w_ref, gq_ref, gk_ref, cost_ref, sint_ref,
                    qt_ref, k_ref, vt_ref, rest_ref):
    x = x_ref[...]
    rstd = lax.rsqrt(jnp.mean(x * x, axis=-1, keepdims=True) + EPS)
    h = _bf16(x * g_ref[...])

    q = jnp.dot(h, w_ref[:, 0:ATT_WIDTH], preferred_element_type=jnp.float32) * rstd
    cost = cost_ref[...]
    sint = sint_ref[...]
    gq = jnp.concatenate([gq_ref[...]] * (TM_IN // LANES), axis=1)
    gk = jnp.concatenate([gk_ref[...]] * (TM_IN // LANES), axis=1)
    for j in range(ATT_WIDTH // LANES):
        qt = q[:, j * LANES:(j + 1) * LANES].T
        for hh in range(LANES // HEAD_DIM):
            row0 = j * LANES + hh * HEAD_DIM
            r = _head_norm_rope(qt[hh * HEAD_DIM:(hh + 1) * HEAD_DIM, :], gq, cost, sint)
            qt_ref[row0:row0 + HEAD_DIM, :] = _bf16(r * SCORE_SCALE)

    kv = jnp.dot(h, w_ref[:, ATT_WIDTH:QKV_WIDTH], preferred_element_type=jnp.float32) * rstd
    kt = kv[:, 0:KV_WIDTH].T
    kr = jnp.concatenate([_head_norm_rope(kt[hd * HEAD_DIM:(hd + 1) * HEAD_DIM, :], gk, cost, sint)
                          for hd in range(ATT_KV_HEADS)], axis=0)
    k_ref[...] = _bf16(kr.T)

    vt = _bf16(kv[:, KV_WIDTH:2 * KV_WIDTH].T)
    for c in range(TM_IN // TK):
        for hd in range(ATT_KV_HEADS):
            vt_ref[c, hd, 0:HEAD_DIM, :] = vt[hd * HEAD_DIM:(hd + 1) * HEAD_DIM, c * TK:(c + 1) * TK]
            vt_ref[c, hd, HEAD_DIM:V_ROWS, :] = jnp.ones((V_ROWS - HEAD_DIM, TK), jnp.bfloat16)

    rest_ref[...] = jnp.dot(h, w_ref[:, QKV_WIDTH:D_IN], preferred_element_type=jnp.float32) * rstd


def _in_proj(x, g, w, layer, gq, gk, cos_tt, sin_tt):
    B, S, _ = x.shape
    ns = S // TM_IN
    const = lambda b, s: (0, 0)
    return pl.pallas_call(
        _in_proj_kernel,
        grid=(B, ns),
        in_specs=[
            pl.BlockSpec((None, TM_IN, D_MODEL), lambda b, s: (b, s, 0)),
            pl.BlockSpec((1, D_MODEL), const),
            pl.BlockSpec((None, D_MODEL, D_IN), lambda b, s: (layer, 0, 0)),
            pl.BlockSpec((HEAD_DIM, LANES), const),
            pl.BlockSpec((HEAD_DIM, LANES), const),
            pl.BlockSpec((HEAD_DIM, TM_IN), lambda b, s: (0, s)),
            pl.BlockSpec((HEAD_DIM, TM_IN), lambda b, s: (0, s)),
        ],
        out_specs=[
            pl.BlockSpec((None, ATT_WIDTH, TM_IN), lambda b, s: (b, 0, s)),
            pl.BlockSpec((None, TM_IN, KV_WIDTH), lambda b, s: (b, s, 0)),
            pl.BlockSpec((None, TM_IN // TK, ATT_KV_HEADS, V_ROWS, TK), lambda b, s: (b, s, 0, 0, 0)),
            pl.BlockSpec((None, TM_IN, REST_WIDTH), lambda b, s: (b, s, 0)),
        ],
        out_shape=[
            jax.ShapeDtypeStruct((B, ATT_WIDTH, S), jnp.bfloat16),
            jax.ShapeDtypeStruct((B, S, KV_WIDTH), jnp.bfloat16),
            jax.ShapeDtypeStruct((B, S // TK, ATT_KV_HEADS, V_ROWS, TK), jnp.bfloat16),
            jax.ShapeDtypeStruct((B, S, REST_WIDTH), jnp.float32),
        ],
        compiler_params=pltpu.CompilerParams(
            dimension_semantics=("arbitrary", "arbitrary"), vmem_limit_bytes=VMEM_LIMIT),
        name="in_proj",
    )(x, g, w, gq, gk, cos_tt, sin_tt)


def _attention_kernel(qt_ref, k_ref, vt_ref, o_ref, qext_ref, s_ref, p_ref, acc_ref, m_ref, mx_ref):
    kv = pl.program_id(1)
    n = k_ref.shape[0] // TK
    for g in range(GQA):
        qg = qt_ref[g * HEAD_DIM:(g + 1) * HEAD_DIM, :]
        zero = jnp.zeros_like(qg)
        qext_ref[0:HEAD_DIM, g * TQ:(g + 1) * TQ] = jnp.where(kv == 0, qg, zero)
        qext_ref[HEAD_DIM:2 * HEAD_DIM, g * TQ:(g + 1) * TQ] = jnp.where(kv == 1, qg, zero)

    def scores(j, g, cols, kblk=None):
        if kblk is None:
            kblk = k_ref[pl.ds(pl.multiple_of(j * TK, TK), TK), :]
        s = jnp.dot(kblk, qext_ref[:, cols], preferred_element_type=jnp.float32)
        s_ref[g] = s
        mx_ref[:, cols] = jnp.max(s, axis=0, keepdims=True)

    def values(j, g):
        return jnp.dot(vt_ref[j], p_ref[g], preferred_element_type=jnp.float32)

    groups = [(g, slice(g * LANE_GROUP, (g + 1) * LANE_GROUP)) for g in range(NQ // LANE_GROUP)]

    def stage(j, first=False, last=False):
        kblk = None if last else k_ref[pl.ds(pl.multiple_of((j + 1) * TK, TK), TK), :]
        for g, cols in groups:
            if first:
                m_new = mx_ref[:, cols]
            else:
                pv = values(j - 1, g)
                m_old = m_ref[:, cols]
                m_new = jnp.maximum(m_old, mx_ref[:, cols])
                acc_ref[:, cols] = jnp.exp2(m_old - m_new) * (acc_ref[:, cols] + pv)
            m_ref[:, cols] = m_new
            p_ref[g] = _bf16(jnp.exp2(s_ref[g] - m_new))
            if not last:
                scores(j + 1, g, cols, kblk)

    acc_ref[...] = jnp.zeros_like(acc_ref)
    for g, cols in groups:
        scores(0, g, cols)
    stage(0, first=True)

    def step(j, carry):
        stage(j)
        return carry

    assert (n - 2) % ATT_UNROLL == 0
    lax.fori_loop(1, n - 1, step, 0, unroll=ATT_UNROLL)
    stage(n - 1, last=True)
    for g, cols in groups:
        acc_ref[:, cols] = acc_ref[:, cols] + values(n - 1, g)
    o = acc_ref[0:HEAD_DIM, :] / acc_ref[HEAD_DIM:HEAD_DIM + 1, :]
    o4 = jnp.concatenate([o[:, g * TQ:(g + 1) * TQ] for g in range(GQA)], axis=0)
    o_ref[...] = o4.T


def _attention(qt, k, vt):
    B, _, S = qt.shape
    nk = vt.shape[1]
    assert vt.shape[-1] == TK
    return pl.pallas_call(
        _attention_kernel,
        grid=(B, ATT_KV_HEADS, S // TQ),
        in_specs=[
            pl.BlockSpec((None, GQA * HEAD_DIM, TQ), lambda b, kv, qi: (b, kv, qi)),
            pl.BlockSpec((None, S, KV_WIDTH), lambda b, kv, qi: (b, 0, 0)),
            pl.BlockSpec((None, nk, None, V_ROWS, TK), lambda b, kv, qi: (b, 0, kv, 0, 0)),
        ],
        out_specs=pl.BlockSpec((None, TQ, GQA * HEAD_DIM), lambda b, kv, qi: (b, qi, kv)),
        out_shape=jax.ShapeDtypeStruct((B, S, ATT_WIDTH), jnp.float32),
        scratch_shapes=[
            pltpu.VMEM((KV_WIDTH, NQ), jnp.bfloat16),
            pltpu.VMEM((NQ // LANE_GROUP, TK, LANE_GROUP), jnp.float32),
            pltpu.VMEM((NQ // LANE_GROUP, TK, LANE_GROUP), jnp.bfloat16),
            pltpu.VMEM((V_ROWS, NQ), jnp.float32),
            pltpu.VMEM((1, NQ), jnp.float32),
            pltpu.VMEM((1, NQ), jnp.float32),
        ],
        compiler_params=pltpu.CompilerParams(
            dimension_semantics=("arbitrary", "arbitrary", "arbitrary"),
            vmem_limit_bytes=VMEM_LIMIT),
        name="attention",
    )(qt, k, vt)


def _silu(x):
    h = 0.5 * x
    return h + h * jnp.tanh(h)


def _gelu(x):
    return 0.5 * x * (1.0 + lax.erf(x * (2.0 ** -0.5)))


def _layer_norm(x, g, b):
    mu = jnp.mean(x, axis=-1, keepdims=True)
    xc = x - mu
    var = jnp.mean(xc * xc, axis=-1, keepdims=True)
    return xc * lax.rsqrt(var + EPS) * g + b


def _glu(a):
    return a[:, :CONV_WIDTH] * jax.nn.sigmoid(a[:, CONV_WIDTH:])


CONV_ROWS = 64


def _post_kernel(x_ref, rest_ref, prev_ref, next_ref, att_ref, wout_ref, gpost_ref,
                 dw_ref, dwb_ref, cg_ref, cb_ref, sg_ref, sb_ref, sw_ref, sbias_ref,
                 o_ref, hext_ref, hsh_ref, cat_ref):
    si = pl.program_id(1)
    last = pl.num_programs(1) - 1

    cat_ref[:, 0:ATT_WIDTH] = _bf16(att_ref[...] * _silu(rest_ref[:, R_GATT:R_GATT + ATT_WIDTH]))
    o_ref[...] = jnp.dot(cat_ref[:, 0:ATT_WIDTH], wout_ref[0:ATT_WIDTH, :], preferred_element_type=jnp.float32)

    hext_ref[0:HALO, :] = jnp.where(si > 0, _glu(prev_ref[...]), 0.0)
    hext_ref[HALO:HALO + TM, :] = _glu(rest_ref[:, R_ACONV:R_ACONV + 2 * CONV_WIDTH])
    hext_ref[HALO + TM:2 * HALO + TM, :] = jnp.where(si < last, _glu(next_ref[...]), 0.0)
    for r in range(1, SUBLANES):
        hsh_ref[r - 1] = hext_ref[r:r + TM + 2 * HALO - SUBLANES, :]
    for rb in range(TM // CONV_ROWS):
        acc = jnp.zeros((CONV_ROWS, CONV_WIDTH), jnp.float32) + dwb_ref[...]
        for j in range(CONV_KERNEL):
            r = (j + HALO - CONV_PAD) % SUBLANES
            start = rb * CONV_ROWS + j + HALO - CONV_PAD - r
            tap = (hext_ref[start:start + CONV_ROWS, :] if r == 0
                   else hsh_ref[r - 1, start:start + CONV_ROWS, :])
            acc = acc + tap * dw_ref[j:j + 1, :]
        rows = slice(rb * CONV_ROWS, (rb + 1) * CONV_ROWS)
        c = _silu(_layer_norm(acc, cg_ref[...], cb_ref[...]))
        c = c * _silu(rest_ref[rows, R_GCONV:R_GCONV + CONV_WIDTH])
        cat_ref[rows, ATT_WIDTH:ATT_WIDTH + CONV_WIDTH] = _bf16(c)
    o_ref[...] += jnp.dot(cat_ref[:, ATT_WIDTH:ATT_WIDTH + CONV_WIDTH],
                          wout_ref[ATT_WIDTH:ATT_WIDTH + CONV_WIDTH, :], preferred_element_type=jnp.float32)

    lane = lax.broadcasted_iota(jnp.int32, (1, SG_WIDTH), 1)
    for c in range(TM // SG_CHUNK):
        rows = slice(c * SG_CHUNK, (c + 1) * SG_CHUNK)
        vln = _bf16(_layer_norm(_gelu(rest_ref[rows, R_V:R_V + SG_WIDTH]), sg_ref[...], sb_ref[...]))
        mixed = jnp.dot(sw_ref[SG_HEADS - 1], vln, preferred_element_type=jnp.float32)
        for hd in range(SG_HEADS - 1):
            mh = jnp.dot(sw_ref[hd], vln, preferred_element_type=jnp.float32)
            mixed = jnp.where(lane // HEAD_DIM == hd, mh, mixed)
        sgu = _gelu(rest_ref[rows, R_U:R_U + SG_WIDTH]) * (mixed + sbias_ref[...])
        sgu = sgu * _silu(rest_ref[rows, R_GSG:R_GSG + SG_WIDTH])
        cat_ref[rows, ATT_WIDTH + CONV_WIDTH:D_MIX] = _bf16(sgu)

    mix = o_ref[...] + jnp.dot(cat_ref[:, ATT_WIDTH + CONV_WIDTH:D_MIX], wout_ref[ATT_WIDTH + CONV_WIDTH:D_MIX, :],
                               preferred_element_type=jnp.float32)
    ms = jnp.mean(mix * mix, axis=-1, keepdims=True)
    o_ref[...] = x_ref[...] + mix * lax.rsqrt(ms + EPS) * gpost_ref[...]


def _post(x, rest, att, wout, layer, gpost, dw, dwb, cg, cb, sg, sb, sw, sbias):
    B, S, _ = x.shape
    ns = S // TM
    hb = TM // HALO
    const2 = lambda b, s: (0, 0)
    const3 = lambda b, s: (0, 0, 0)
    aconv_blk = R_ACONV // (2 * CONV_WIDTH)
    return pl.pallas_call(
        _post_kernel,
        grid=(B, ns),
        in_specs=[
            pl.BlockSpec((None, TM, D_MODEL), lambda b, s: (b, s, 0)),
            pl.BlockSpec((None, TM, REST_WIDTH), lambda b, s: (b, s, 0)),
            pl.BlockSpec((None, HALO, 2 * CONV_WIDTH),
                         lambda b, s: (b, jnp.maximum(s * hb - 1, 0), aconv_blk)),
            pl.BlockSpec((None, HALO, 2 * CONV_WIDTH),
                         lambda b, s: (b, jnp.minimum((s + 1) * hb, S // HALO - 1), aconv_blk)),
            pl.BlockSpec((None, TM, ATT_WIDTH), lambda b, s: (b, s, 0)),
            pl.BlockSpec((None, D_MIX, D_MODEL), lambda b, s: (layer, 0, 0)),
            pl.BlockSpec((1, D_MODEL), const2),
            pl.BlockSpec((CONV_KERNEL, CONV_WIDTH), const2),
            pl.BlockSpec((1, CONV_WIDTH), const2),
            pl.BlockSpec((1, CONV_WIDTH), const2),
            pl.BlockSpec((1, CONV_WIDTH), const2),
            pl.BlockSpec((1, SG_WIDTH), const2),
            pl.BlockSpec((1, SG_WIDTH), const2),
            pl.BlockSpec((SG_HEADS, SG_CHUNK, SG_CHUNK), const3),
            pl.BlockSpec((SG_CHUNK, SG_WIDTH), const2),
        ],
        out_specs=pl.BlockSpec((None, TM, D_MODEL), lambda b, s: (b, s, 0)),
        out_shape=jax.ShapeDtypeStruct((B, S, D_MODEL), jnp.float32),
        scratch_shapes=[
            pltpu.VMEM((TM + 2 * HALO, CONV_WIDTH), jnp.float32),
            pltpu.VMEM((SUBLANES - 1, TM + 2 * HALO - SUBLANES, CONV_WIDTH), jnp.float32),
            pltpu.VMEM((TM, D_MIX), jnp.bfloat16),
        ],
        compiler_params=pltpu.CompilerParams(
            dimension_semantics=("arbitrary", "arbitrary"), vmem_limit_bytes=VMEM_LIMIT),
        name="post",
    )(x, rest, rest, rest, att, wout, gpost, dw, dwb, cg, cb, sg, sb, sw, sbias)


def _rope_tables(S):
    half = ROPE_AXIS_DIM // 2
    t = jnp.arange(S, dtype=jnp.int32)
    row = (t // GRID_W).astype(jnp.float32)
    col = (t % GRID_W).astype(jnp.float32)
    inv_freq = ROPE_THETA ** (-jnp.arange(half, dtype=jnp.float32) / half)
    ang_r = inv_freq[:, None] * row[None, :]
    ang_c = inv_freq[:, None] * col[None, :]
    cos_tt = jnp.concatenate([jnp.cos(ang_r)] * 2 + [jnp.cos(ang_c)] * 2, axis=0)
    sin_tt = jnp.concatenate([-jnp.sin(ang_r), jnp.sin(ang_r), -jnp.sin(ang_c), jnp.sin(ang_c)], axis=0)
    return cos_tt, sin_tt


def kernel(x, pre_norm, post_norm, w_in, w_out, q_norm, k_norm, conv_dw, conv_dw_b,
           conv_ln_g, conv_ln_b, sg_ln_g, sg_ln_b, sg_w, sg_b):
    B, S, _ = x.shape
    depth = w_in.shape[0]
    cos_tt, sin_tt = _rope_tables(S)
    row2 = lambda a: a.reshape(1, -1)
    col_lanes = lambda a: jnp.broadcast_to(a[:, None], (HEAD_DIM, LANES))
    w_in_bf, w_out_bf = _bf16(w_in), _bf16(w_out)
    for l in range(depth):
        qt, k, vt, rest = _in_proj(
            x, row2(pre_norm[l]), w_in_bf, l, col_lanes(q_norm[l]), col_lanes(k_norm[l]), cos_tt, sin_tt)
        att = _attention(qt, k, vt)
        sbias = jnp.repeat(sg_b[l].T, HEAD_DIM, axis=1)
        x = _post(x, rest, att, w_out_bf, l, row2(post_norm[l]),
                  conv_dw[l], row2(conv_dw_b[l]), row2(conv_ln_g[l]), row2(conv_ln_b[l]),
                  row2(sg_ln_g[l]), row2(sg_ln_b[l]), _bf16(sg_w[l]), sbias)
    return x
```

```python
import math

import jax
import jax.numpy as jnp
from jax import lax
from jax.experimental import pallas as pl
from jax.experimental.pallas import tpu as pltpu

D_MODEL = 1024
HEAD_DIM = 64
GRID_W = 64
EPS = 1e-6
ATT_HEADS = 8
ATT_KV_HEADS = 2
GQA = ATT_HEADS // ATT_KV_HEADS
ATT_WIDTH = ATT_HEADS * HEAD_DIM
KV_WIDTH = ATT_KV_HEADS * HEAD_DIM
ROPE_THETA = 10000.0
ROPE_AXIS_DIM = HEAD_DIM // 2
CONV_WIDTH = 256
CONV_KERNEL = 31
CONV_PAD = CONV_KERNEL // 2
SG_HEADS = 4
SG_WIDTH = SG_HEADS * HEAD_DIM
SG_CHUNK = 128
D_MIX = ATT_WIDTH + CONV_WIDTH + SG_WIDTH
QKV_WIDTH = ATT_WIDTH + 2 * KV_WIDTH
REST_WIDTH = ATT_WIDTH + 3 * CONV_WIDTH + 3 * SG_WIDTH
D_IN = QKV_WIDTH + REST_WIDTH

R_GATT = 0
R_ACONV = ATT_WIDTH
R_GCONV = R_ACONV + 2 * CONV_WIDTH
R_U = R_GCONV + CONV_WIDTH
R_V = R_U + SG_WIDTH
R_GSG = R_V + SG_WIDTH

LANES = 128
SUBLANES = 8
HALO = 16
TM = 512
TM_IN = 1024
TQ = 1024
TK = 256
ATT_UNROLL = 31
NQ = GQA * TQ
LANE_GROUP = 256
V_ROWS = HEAD_DIM + SUBLANES
SCORE_SCALE = HEAD_DIM ** -0.5 * math.log2(math.e)
VMEM_LIMIT = 56 * 1024 * 1024


def _bf16(x):
    return x.astype(jnp.bfloat16)


def _head_norm_rope(xh, gain, cost, sint):
    quarter = ROPE_AXIS_DIM // 2
    xn = xh * lax.rsqrt(jnp.mean(xh * xh, axis=0, keepdims=True) + EPS) * gain
    partner = jnp.concatenate([xn[quarter:2 * quarter], xn[0:quarter],
                               xn[3 * quarter:4 * quarter], xn[2 * quarter:3 * quarter]], axis=0)
    return xn * cost + partner * sint


def _in_proj_kernel(x_ref, g_ref, w_ref, gq_ref, gk_ref, cost_ref, sint_ref,
                    qt_ref, k_ref, vt_ref, rest_ref):
    x = x_ref[...]
    rstd = lax.rsqrt(jnp.mean(x * x, axis=-1, keepdims=True) + EPS)
    h = _bf16(x * g_ref[...])

    q = jnp.dot(h, w_ref[:, 0:ATT_WIDTH], preferred_element_type=jnp.float32) * rstd
    cost = cost_ref[...]
    sint = sint_ref[...]
    gq = jnp.concatenate([gq_ref[...]] * (TM_IN // LANES), axis=1)
    gk = jnp.concatenate([gk_ref[...]] * (TM_IN // LANES), axis=1)
    for j in range(ATT_WIDTH // LANES):
        qt = q[:, j * LANES:(j + 1) * LANES].T
        for hh in range(LANES // HEAD_DIM):
            row0 = j * LANES + hh * HEAD_DIM
            r = _head_norm_rope(qt[hh * HEAD_DIM:(hh + 1) * HEAD_DIM, :], gq, cost, sint)
            qt_ref[row0:row0 + HEAD_DIM, :] = _bf16(r * SCORE_SCALE)

    kv = jnp.dot(h, w_ref[:, ATT_WIDTH:QKV_WIDTH], preferred_element_type=jnp.float32) * rstd
    kt = kv[:, 0:KV_WIDTH].T
    kr = jnp.concatenate([_head_norm_rope(kt[hd * HEAD_DIM:(hd + 1) * HEAD_DIM, :], gk, cost, sint)
                          for hd in range(ATT_KV_HEADS)], axis=0)
    k_ref[...] = _bf16(kr.T)

    vt = kv[:, KV_WIDTH:2 * KV_WIDTH].T
    for c in range(TM_IN // TK):
        for hd in range(ATT_KV_HEADS):
            vt_ref[c, hd, 0:HEAD_DIM, :] = vt[hd * HEAD_DIM:(hd + 1) * HEAD_DIM, c * TK:(c + 1) * TK]
            vt_ref[c, hd, HEAD_DIM:V_ROWS, :] = jnp.ones((V_ROWS - HEAD_DIM, TK), jnp.float32)

    rest_ref[...] = jnp.dot(h, w_ref[:, QKV_WIDTH:D_IN], preferred_element_type=jnp.float32) * rstd


def _in_proj(x, g, w, layer, gq, gk, cos_tt, sin_tt):
    B, S, _ = x.shape
    ns = S // TM_IN
    const = lambda b, s: (0, 0)
    return pl.pallas_call(
        _in_proj_kernel,
        grid=(B, ns),
        in_specs=[
            pl.BlockSpec((None, TM_IN, D_MODEL), lambda b, s: (b, s, 0)),
            pl.BlockSpec((1, D_MODEL), const),
            pl.BlockSpec((None, D_MODEL, D_IN), lambda b, s: (layer, 0, 0)),
            pl.BlockSpec((HEAD_DIM, LANES), const),
            pl.BlockSpec((HEAD_DIM, LANES), const),
            pl.BlockSpec((HEAD_DIM, TM_IN), lambda b, s: (0, s)),
            pl.BlockSpec((HEAD_DIM, TM_IN), lambda b, s: (0, s)),
        ],
        out_specs=[
            pl.BlockSpec((None, ATT_WIDTH, TM_IN), lambda b, s: (b, 0, s)),
            pl.BlockSpec((None, TM_IN, KV_WIDTH), lambda b, s: (b, s, 0)),
            pl.BlockSpec((None, TM_IN // TK, ATT_KV_HEADS, V_ROWS, TK), lambda b, s: (b, s, 0, 0, 0)),
            pl.BlockSpec((None, TM_IN, REST_WIDTH), lambda b, s: (b, s, 0)),
        ],
        out_shape=[
            jax.ShapeDtypeStruct((B, ATT_WIDTH, S), jnp.bfloat16),
            jax.ShapeDtypeStruct((B, S, KV_WIDTH), jnp.bfloat16),
            jax.ShapeDtypeStruct((B, S // TK, ATT_KV_HEADS, V_ROWS, TK), jnp.float32),
            jax.ShapeDtypeStruct((B, S, REST_WIDTH), jnp.float32),
        ],
        compiler_params=pltpu.CompilerParams(
            dimension_semantics=("arbitrary", "arbitrary"), vmem_limit_bytes=VMEM_LIMIT),
        name="in_proj",
    )(x, g, w, gq, gk, cos_tt, sin_tt)


def _attention_kernel(qt_ref, k_ref, vt_ref, o_ref, qext_ref, s_ref, p_ref, acc_ref, m_ref, mx_ref):
    kv = pl.program_id(1)
    n = k_ref.shape[0] // TK
    for g in range(GQA):
        qg = qt_ref[g * HEAD_DIM:(g + 1) * HEAD_DIM, :]
        zero = jnp.zeros_like(qg)
        qext_ref[0:HEAD_DIM, g * TQ:(g + 1) * TQ] = jnp.where(kv == 0, qg, zero)
        qext_ref[HEAD_DIM:2 * HEAD_DIM, g * TQ:(g + 1) * TQ] = jnp.where(kv == 1, qg, zero)

    def scores(j, g, cols):
        kblk = k_ref[pl.ds(pl.multiple_of(j * TK, TK), TK), :]
        s = jnp.dot(kblk, qext_ref[:, cols], preferred_element_type=jnp.float32)
        s_ref[g] = s
        mx_ref[:, cols] = jnp.max(s, axis=0, keepdims=True)

    def values(j, g):
        return jnp.dot(vt_ref[j], p_ref[g], preferred_element_type=jnp.float32)

    groups = [(g, slice(g * LANE_GROUP, (g + 1) * LANE_GROUP)) for g in range(NQ // LANE_GROUP)]

    def stage(j, first=False, last=False):
        for g, cols in groups:
            if first:
                m_new = mx_ref[:, cols]
            else:
                pv = values(j - 1, g)
                m_old = m_ref[:, cols]
                m_new = jnp.maximum(m_old, mx_ref[:, cols])
                acc_ref[:, cols] = jnp.exp2(m_old - m_new) * (acc_ref[:, cols] + pv)
            m_ref[:, cols] = m_new
            p_ref[g] = jnp.exp2(s_ref[g] - m_new)
            if not last:
                scores(j + 1, g, cols)

    acc_ref[...] = jnp.zeros_like(acc_ref)
    for g, cols in groups:
        scores(0, g, cols)
    stage(0, first=True)

    def step(j, carry):
        stage(j)
        return carry

    assert (n - 2) % ATT_UNROLL == 0
    lax.fori_loop(1, n - 1, step, 0, unroll=ATT_UNROLL)
    stage(n - 1, last=True)
    for g, cols in groups:
        acc_ref[:, cols] = acc_ref[:, cols] + values(n - 1, g)
    o = acc_ref[0:HEAD_DIM, :] / acc_ref[HEAD_DIM:HEAD_DIM + 1, :]
    o4 = jnp.concatenate([o[:, g * TQ:(g + 1) * TQ] for g in range(GQA)], axis=0)
    o_ref[...] = o4.T


def _attention(qt, k, vt):
    B, _, S = qt.shape
    nk = vt.shape[1]
    assert vt.shape[-1] == TK
    return pl.pallas_call(
        _attention_kernel,
        grid=(B, ATT_KV_HEADS, S // TQ),
        in_specs=[
            pl.BlockSpec((None, GQA * HEAD_DIM, TQ), lambda b, kv, qi: (b, kv, qi)),
            pl.BlockSpec((None, S, KV_WIDTH), lambda b, kv, qi: (b, 0, 0)),
            pl.BlockSpec((None, nk, None, V_ROWS, TK), lambda b, kv, qi: (b, 0, kv, 0, 0)),
        ],
        out_specs=pl.BlockSpec((None, TQ, GQA * HEAD_DIM), lambda b, kv, qi: (b, qi, kv)),
        out_shape=jax.ShapeDtypeStruct((B, S, ATT_WIDTH), jnp.float32),
        scratch_shapes=[
            pltpu.VMEM((KV_WIDTH, NQ), jnp.bfloat16),
            pltpu.VMEM((NQ // LANE_GROUP, TK, LANE_GROUP), jnp.float32),
            pltpu.VMEM((NQ // LANE_GROUP, TK, LANE_GROUP), jnp.float32),
            pltpu.VMEM((V_ROWS, NQ), jnp.float32),
            pltpu.VMEM((1, NQ), jnp.float32),
            pltpu.VMEM((1, NQ), jnp.float32),
        ],
        compiler_params=pltpu.CompilerParams(
            dimension_semantics=("arbitrary", "arbitrary", "arbitrary"),
            vmem_limit_bytes=VMEM_LIMIT),
        name="attention",
    )(qt, k, vt)


def _silu(x):
    h = 0.5 * x
    return h + h * jnp.tanh(h)


def _gelu(x):
    return 0.5 * x * (1.0 + lax.erf(x * (2.0 ** -0.5)))


def _layer_norm(x, g, b):
    mu = jnp.mean(x, axis=-1, keepdims=True)
    xc = x - mu
    var = jnp.mean(xc * xc, axis=-1, keepdims=True)
    return xc * lax.rsqrt(var + EPS) * g + b


def _glu(a):
    return a[:, :CONV_WIDTH] * jax.nn.sigmoid(a[:, CONV_WIDTH:])


CONV_ROWS = 64


def _post_kernel(x_ref, rest_ref, prev_ref, next_ref, att_ref, wout_ref, gpost_ref,
                 dw_ref, dwb_ref, cg_ref, cb_ref, sg_ref, sb_ref, sw_ref, sbias_ref,
                 o_ref, hext_ref, hsh_ref, cat_ref):
    si = pl.program_id(1)
    last = pl.num_programs(1) - 1

    cat_ref[:, 0:ATT_WIDTH] = _bf16(att_ref[...] * _silu(rest_ref[:, R_GATT:R_GATT + ATT_WIDTH]))
    o_ref[...] = jnp.dot(cat_ref[:, 0:ATT_WIDTH], wout_ref[0:ATT_WIDTH, :], preferred_element_type=jnp.float32)

    hext_ref[0:HALO, :] = jnp.where(si > 0, _glu(prev_ref[...]), 0.0)
    hext_ref[HALO:HALO + TM, :] = _glu(rest_ref[:, R_ACONV:R_ACONV + 2 * CONV_WIDTH])
    hext_ref[HALO + TM:2 * HALO + TM, :] = jnp.where(si < last, _glu(next_ref[...]), 0.0)
    for r in range(1, SUBLANES):
        hsh_ref[r - 1] = hext_ref[r:r + TM + 2 * HALO - SUBLANES, :]
    for rb in range(TM // CONV_ROWS):
        acc = jnp.zeros((CONV_ROWS, CONV_WIDTH), jnp.float32) + dwb_ref[...]
        for j in range(CONV_KERNEL):
            r = (j + HALO - CONV_PAD) % SUBLANES
            start = rb * CONV_ROWS + j + HALO - CONV_PAD - r
            tap = (hext_ref[start:start + CONV_ROWS, :] if r == 0
                   else hsh_ref[r - 1, start:start + CONV_ROWS, :])
            acc = acc + tap * dw_ref[j:j + 1, :]
        rows = slice(rb * CONV_ROWS, (rb + 1) * CONV_ROWS)
        c = _silu(_layer_norm(acc, cg_ref[...], cb_ref[...]))
        c = c * _silu(rest_ref[rows, R_GCONV:R_GCONV + CONV_WIDTH])
        cat_ref[rows, ATT_WIDTH:ATT_WIDTH + CONV_WIDTH] = _bf16(c)
    o_ref[...] += jnp.dot(cat_ref[:, ATT_WIDTH:ATT_WIDTH + CONV_WIDTH],
                          wout_ref[ATT_WIDTH:ATT_WIDTH + CONV_WIDTH, :], preferred_element_type=jnp.float32)

    lane = lax.broadcasted_iota(jnp.int32, (1, SG_WIDTH), 1)
    for c in range(TM // SG_CHUNK):
        rows = slice(c * SG_CHUNK, (c + 1) * SG_CHUNK)
        vln = _bf16(_layer_norm(_gelu(rest_ref[rows, R_V:R_V + SG_WIDTH]), sg_ref[...], sb_ref[...]))
        mixed = jnp.dot(sw_ref[SG_HEADS - 1], vln, preferred_element_type=jnp.float32)
        for hd in range(SG_HEADS - 1):
            mh = jnp.dot(sw_ref[hd], vln, preferred_element_type=jnp.float32)
            mixed = jnp.where(lane // HEAD_DIM == hd, mh, mixed)
        sgu = _gelu(rest_ref[rows, R_U:R_U + SG_WIDTH]) * (mixed + sbias_ref[...])
        sgu = sgu * _silu(rest_ref[rows, R_GSG:R_GSG + SG_WIDTH])
        cat_ref[rows, ATT_WIDTH + CONV_WIDTH:D_MIX] = _bf16(sgu)

    mix = o_ref[...] + jnp.dot(cat_ref[:, ATT_WIDTH + CONV_WIDTH:D_MIX], wout_ref[ATT_WIDTH + CONV_WIDTH:D_MIX, :],
                               preferred_element_type=jnp.float32)
    ms = jnp.mean(mix * mix, axis=-1, keepdims=True)
    o_ref[...] = x_ref[...] + mix * lax.rsqrt(ms + EPS) * gpost_ref[...]


def _post(x, rest, att, wout, layer, gpost, dw, dwb, cg, cb, sg, sb, sw, sbias):
    B, S, _ = x.shape
    ns = S // TM
    hb = TM // HALO
    const2 = lambda b, s: (0, 0)
    const3 = lambda b, s: (0, 0, 0)
    aconv_blk = R_ACONV // (2 * CONV_WIDTH)
    return pl.pallas_call(
        _post_kernel,
        grid=(B, ns),
        in_specs=[
            pl.BlockSpec((None, TM, D_MODEL), lambda b, s: (b, s, 0)),
            pl.BlockSpec((None, TM, REST_WIDTH), lambda b, s: (b, s, 0)),
            pl.BlockSpec((None, HALO, 2 * CONV_WIDTH),
                         lambda b, s: (b, jnp.maximum(s * hb - 1, 0), aconv_blk)),
            pl.BlockSpec((None, HALO, 2 * CONV_WIDTH),
                         lambda b, s: (b, jnp.minimum((s + 1) * hb, S // HALO - 1), aconv_blk)),
            pl.BlockSpec((None, TM, ATT_WIDTH), lambda b, s: (b, s, 0)),
            pl.BlockSpec((None, D_MIX, D_MODEL), lambda b, s: (layer, 0, 0)),
            pl.BlockSpec((1, D_MODEL), const2),
            pl.BlockSpec((CONV_KERNEL, CONV_WIDTH), const2),
            pl.BlockSpec((1, CONV_WIDTH), const2),
            pl.BlockSpec((1, CONV_WIDTH), const2),
            pl.BlockSpec((1, CONV_WIDTH), const2),
            pl.BlockSpec((1, SG_WIDTH), const2),
            pl.BlockSpec((1, SG_WIDTH), const2),
            pl.BlockSpec((SG_HEADS, SG_CHUNK, SG_CHUNK), const3),
            pl.BlockSpec((SG_CHUNK, SG_WIDTH), const2),
        ],
        out_specs=pl.BlockSpec((None, TM, D_MODEL), lambda b, s: (b, s, 0)),
        out_shape=jax.ShapeDtypeStruct((B, S, D_MODEL), jnp.float32),
        scratch_shapes=[
            pltpu.VMEM((TM + 2 * HALO, CONV_WIDTH), jnp.float32),
            pltpu.VMEM((SUBLANES - 1, TM + 2 * HALO - SUBLANES, CONV_WIDTH), jnp.float32),
            pltpu.VMEM((TM, D_MIX), jnp.bfloat16),
        ],
        compiler_params=pltpu.CompilerParams(
            dimension_semantics=("arbitrary", "arbitrary"), vmem_limit_bytes=VMEM_LIMIT),
        name="post",
    )(x, rest, rest, rest, att, wout, gpost, dw, dwb, cg, cb, sg, sb, sw, sbias)


def _rope_tables(S):
    half = ROPE_AXIS_DIM // 2
    t = jnp.arange(S, dtype=jnp.int32)
    row = (t // GRID_W).astype(jnp.float32)
    col = (t % GRID_W).astype(jnp.float32)
    inv_freq = ROPE_THETA ** (-jnp.arange(half, dtype=jnp.float32) / half)
    ang_r = inv_freq[:, None] * row[None, :]
    ang_c = inv_freq[:, None] * col[None, :]
    cos_tt = jnp.concatenate([jnp.cos(ang_r)] * 2 + [jnp.cos(ang_c)] * 2, axis=0)
    sin_tt = jnp.concatenate([-jnp.sin(ang_r), jnp.sin(ang_r), -jnp.sin(ang_c), jnp.sin(ang_c)], axis=0)
    return cos_tt, sin_tt


def kernel(x, pre_norm, post_norm, w_in, w_out, q_norm, k_norm, conv_dw, conv_dw_b,
           conv_ln_g, conv_ln_b, sg_ln_g, sg_ln_b, sg_w, sg_b):
    B, S, _ = x.shape
    depth = w_in.shape[0]
    cos_tt, sin_tt = _rope_tables(S)
    row2 = lambda a: a.reshape(1, -1)
    col_lanes = lambda a: jnp.broadcast_to(a[:, None], (HEAD_DIM, LANES))
    w_in_bf, w_out_bf = _bf16(w_in), _bf16(w_out)
    for l in range(depth):
        qt, k, vt, rest = _in_proj(
            x, row2(pre_norm[l]), w_in_bf, l, col_lanes(q_norm[l]), col_lanes(k_norm[l]), cos_tt, sin_tt)
        att = _attention(qt, k, vt)
        sbias = jnp.repeat(sg_b[l].T, HEAD_DIM, axis=1)
        x = _post(x, rest, att, w_out_bf, l, row2(post_norm[l]),
                  conv_dw[l], row2(conv_dw_b[l]), row2(conv_ln_g[l]), row2(conv_ln_b[l]),
                  row2(sg_ln_g[l]), row2(sg_ln_b[l]), _bf16(sg_w[l]), sbias)
    return x
```
